```python
import math
import jax
import jax.numpy as jnp
from jax import lax
import numpy as np

D_MODEL = 1024
BATCH = 4
SEQ = 8192
DEPTH = 1

CTX_LEN = 256
GRID_W = 64
EXPAND = 2
D_MIX = EXPAND * D_MODEL
D_S5 = D_MIX // 2
D_LRU = D_MIX - D_S5
S5_GROUP = 16
S5_GROUPS = D_S5 // S5_GROUP
S5_STATE = 64
LRU_HEADS = 16
LRU_HEAD_DIM = D_LRU // LRU_HEADS
LRU_C = 8.0
CONV_W = 4
CONV_PAD = (1, 2)
N_DIR = 2
EPS = 1e-6

kernel_name = "hybrid_s5_rglru_prefix_block"


def rmsnorm(x, g):
    xf = x.astype(jnp.float32)
    y = xf * lax.rsqrt(jnp.mean(xf * xf, axis=-1, keepdims=True) + EPS)
    return (y * g.astype(jnp.float32)).astype(x.dtype)


def adaln(cvec, w, b):
    m = jax.nn.silu(cvec) @ w + b
    return jnp.split(m, 3, axis=-1)


def to_col_major(x, rows):
    b, l, e = x.shape
    return x.reshape(b, rows, GRID_W, e).transpose(0, 2, 1, 3).reshape(b, l, e)


def to_row_major(x, rows):
    b, l, e = x.shape
    return x.reshape(b, GRID_W, rows, e).transpose(0, 2, 1, 3).reshape(b, l, e)


def s5_discretize(a_re, a_im, log_step, b_re, b_im):
    step = jnp.exp(log_step.astype(jnp.float32))[:, None]
    ar = a_re.astype(jnp.float32)
    ai = a_im.astype(jnp.float32)
    mag = jnp.exp(ar * step)
    abar_re = mag * jnp.cos(ai * step)
    abar_im = mag * jnp.sin(ai * step)
    den = ar * ar + ai * ai
    nr = abar_re - 1.0
    ni = abar_im
    coef_re = (nr * ar + ni * ai) / den
    coef_im = (ni * ar - nr * ai) / den
    br = b_re.astype(jnp.float32)
    bi = b_im.astype(jnp.float32)
    bbar_re = coef_re[..., None] * br - coef_im[..., None] * bi
    bbar_im = coef_re[..., None] * bi + coef_im[..., None] * br
    return abar_re, abar_im, bbar_re, bbar_im


def _complex_combine(first, second):
    a1r, a1i, b1r, b1i = first
    a2r, a2i, b2r, b2i = second
    ar = a2r * a1r - a2i * a1i
    ai = a2r * a1i + a2i * a1r
    br = a2r[:, None] * b1r - a2i[:, None] * b1i + b2r
    bi = a2r[:, None] * b1i + a2i[:, None] * b1r + b2i
    return ar, ai, br, bi


def _real_combine(first, second):
    a1, b1 = first
    a2, b2 = second
    return a2 * a1, a2 * b1 + b2


def s5_scan(ug, abar_re, abar_im, bbar_re, bbar_im, h0, reverse):
    length = ug.shape[1]
    bu_re = jnp.einsum("blgh,gnh->lbgn", ug, bbar_re)
    bu_im = jnp.einsum("blgh,gnh->lbgn", ug, bbar_im)
    shape = (length,) + abar_re.shape
    acr, aci, hr, hi = lax.associative_scan(
        _complex_combine,
        (jnp.broadcast_to(abar_re, shape), jnp.broadcast_to(abar_im, shape), bu_re, bu_im),
        reverse=reverse, axis=0)
    if h0 is not None:
        h0r, h0i = h0
        hr = hr + acr[:, None] * h0r - aci[:, None] * h0i
        hi = hi + acr[:, None] * h0i + aci[:, None] * h0r
    return hr, hi


def s5_branch(u, a_re, a_im, log_step, b_re, b_im, c_re, c_im, d_skip, w_glu, b_glu, init, need_out):
    bsz, length, _ = u.shape
    ug = u.astype(jnp.float32).reshape(bsz, length, S5_GROUPS, S5_GROUP)
    y = None
    finals = []
    for d, rev in enumerate((False, True)):
        abr, abi, bbr, bbi = s5_discretize(a_re[d], a_im[d], log_step[d], b_re[d], b_im[d])
        h0 = None if init is None else init[d]
        hr, hi = s5_scan(ug, abr, abi, bbr, bbi, h0, rev)
        end = 0 if rev else length - 1
        finals.append((hr[end], hi[end]))
        if need_out:
            cr = c_re[d].astype(jnp.float32)
            ci = c_im[d].astype(jnp.float32)
            yd = jnp.einsum("lbgn,ghn->blgh", hr, cr) - jnp.einsum("lbgn,ghn->blgh", hi, ci)
            y = yd if y is None else y + yd
    if not need_out:
        return None, finals
    y = y.reshape(bsz, length, D_S5) + d_skip.astype(jnp.float32) * u.astype(jnp.float32)
    y = jax.nn.gelu(y).astype(u.dtype)
    y = y * jax.nn.sigmoid(y @ w_glu + b_glu)
    return y, finals


def centred_dwconv(x, w, b):
    y = lax.conv_general_dilated(
        x, w.astype(x.dtype)[:, None, :], window_strides=(1,), padding=(CONV_PAD,),
        dimension_numbers=("NWC", "WIO", "NWC"), feature_group_count=x.shape[-1])
    return y + b.astype(x.dtype)


def rglru_scan(xc, w_a, b_a, w_x, b_x, lam, h0, reverse):
    bsz, length, e = xc.shape
    xh = xc.reshape(bsz, length, LRU_HEADS, LRU_HEAD_DIM)
    r = jax.nn.sigmoid(jnp.einsum("blhi,hij->blhj", xh, w_a).reshape(bsz, length, e) + b_a)
    gi = jax.nn.sigmoid(jnp.einsum("blhi,hij->blhj", xh, w_x).reshape(bsz, length, e) + b_x)
    log_a = -LRU_C * r.astype(jnp.float32) * jax.nn.softplus(-lam.astype(jnp.float32))
    a = jnp.exp(log_a)
    bx = jnp.sqrt(-jnp.expm1(2.0 * log_a)) * (gi * xc).astype(jnp.float32)
    acum, h = lax.associative_scan(_real_combine, (a, bx), reverse=reverse, axis=1)
    if h0 is not None:
        h = h + acum * h0[:, None, :]
    return h


def rglru_branch(u, conv_w, conv_b, w_a, b_a, w_x, b_x, lam, init, need_out):
    xc = centred_dwconv(u, conv_w, conv_b)
    y = None
    finals = []
    for d, rev in enumerate((False, True)):
        h0 = None if init is None else init[d]
        h = rglru_scan(xc, w_a[d], b_a[d], w_x[d], b_x[d], lam[d], h0, rev)
        finals.append(h[:, 0] if rev else h[:, -1])
        if need_out:
            y = h if y is None else y + h
    if not need_out:
        return None, finals
    return y.astype(u.dtype), finals


def mix_out(y_s5, g_s5, y_lru, g_lru, w_out):
    y = jnp.concatenate([y_s5 * jax.nn.silu(g_s5), y_lru * jax.nn.silu(g_lru)], axis=-1)
    return y @ w_out


def setup_inputs(seed: int = 0) -> dict:
    key = jax.random.key(seed)
    ks = jax.random.split(key, 27)
    f32 = jnp.float32

    def nrm(k, shape, scale):
        return jax.random.normal(k, shape, f32) * scale

    x = nrm(ks[0], (BATCH, SEQ, D_MODEL), 1.0)
    c = nrm(ks[1], (BATCH, D_MODEL), 1.0)
    ctx = nrm(ks[2], (BATCH, CTX_LEN, D_MODEL), 1.0)
    c_ctx = nrm(ks[3], (D_MODEL,), 1.0)
    w_mod = nrm(ks[4], (DEPTH, D_MODEL, 3 * D_MODEL), D_MODEL ** -0.5)
    b_mod = nrm(ks[5], (DEPTH, 3 * D_MODEL), 0.02)
    norm_g = 1.0 + nrm(ks[6], (DEPTH, D_MODEL), 0.01)
    w_in = nrm(ks[7], (DEPTH, D_MODEL, 2 * D_MIX), D_MODEL ** -0.5)
    sd = (DEPTH, N_DIR, S5_GROUPS, S5_STATE)
    s5_a_re = -0.5 + nrm(ks[8], sd, 0.01)
    s5_a_im = math.pi * jnp.arange(S5_STATE, dtype=f32) + nrm(ks[9], sd, 0.01)
    s5_log_step = jax.random.uniform(ks[10], (DEPTH, N_DIR, S5_GROUPS), f32, math.log(1e-3), math.log(1e-1))
    bshape = (DEPTH, N_DIR, S5_GROUPS, S5_STATE, S5_GROUP)
    s5_b_re = nrm(ks[11], bshape, (2 * S5_GROUP) ** -0.5)
    s5_b_im = nrm(ks[12], bshape, (2 * S5_GROUP) ** -0.5)
    cshape = (DEPTH, N_DIR, S5_GROUPS, S5_GROUP, S5_STATE)
    s5_c_re = nrm(ks[13], cshape, (2 * S5_STATE) ** -0.5)
    s5_c_im = nrm(ks[14], cshape, (2 * S5_STATE) ** -0.5)
    s5_d = nrm(ks[15], (DEPTH, D_S5), 1.0)
    s5_w_glu = nrm(ks[16], (DEPTH, D_S5, D_S5), D_S5 ** -0.5)
    s5_b_glu = nrm(ks[17], (DEPTH, D_S5), 0.02)
    lru_conv_w = nrm(ks[18], (DEPTH, CONV_W, D_LRU), CONV_W ** -0.5)
    lru_conv_b = nrm(ks[19], (DEPTH, D_LRU), 0.02)
    gshape = (DEPTH, N_DIR, LRU_HEADS, LRU_HEAD_DIM, LRU_HEAD_DIM)
    lru_w_a = nrm(ks[20], gshape, LRU_HEAD_DIM ** -0.5)
    lru_b_a = nrm(ks[21], (DEPTH, N_DIR, D_LRU), 0.02)
    lru_w_x = nrm(ks[22], gshape, LRU_HEAD_DIM ** -0.5)
    lru_b_x = nrm(ks[23], (DEPTH, N_DIR, D_LRU), 0.02)
    a_pow = jax.random.uniform(ks[24], (DEPTH, N_DIR, D_LRU), f32, 0.9, 0.999)
    a_base = a_pow ** (1.0 / LRU_C)
    lru_lam = jnp.log(a_base) - jnp.log1p(-a_base)
    w_out = nrm(ks[25], (DEPTH, D_MIX, D_MODEL), D_MIX ** -0.5)
    final_g = 1.0 + nrm(ks[26], (D_MODEL,), 0.01)
    return {
        "x": x, "c": c, "ctx": ctx, "c_ctx": c_ctx,
        "w_mod": w_mod, "b_mod": b_mod, "norm_g": norm_g, "w_in": w_in,
        "s5_a_re": s5_a_re, "s5_a_im": s5_a_im, "s5_log_step": s5_log_step,
        "s5_b_re": s5_b_re, "s5_b_im": s5_b_im, "s5_c_re": s5_c_re, "s5_c_im": s5_c_im,
        "s5_d": s5_d, "s5_w_glu": s5_w_glu, "s5_b_glu": s5_b_glu,
        "lru_conv_w": lru_conv_w, "lru_conv_b": lru_conv_b,
        "lru_w_a": lru_w_a, "lru_b_a": lru_b_a, "lru_w_x": lru_w_x, "lru_b_x": lru_b_x,
        "lru_lam": lru_lam, "w_out": w_out, "final_g": final_g,
    }


def reference(x, c, ctx, c_ctx, w_mod, b_mod, norm_g, w_in,
              s5_a_re, s5_a_im, s5_log_step, s5_b_re, s5_b_im, s5_c_re, s5_c_im,
              s5_d, s5_w_glu, s5_b_glu,
              lru_conv_w, lru_conv_b, lru_w_a, lru_b_a, lru_w_x, lru_b_x, lru_lam,
              w_out, final_g):
    rows = x.shape[1] // GRID_W
    cuts = [D_S5, 2 * D_S5, 2 * D_S5 + D_LRU]
    h, hc = x, ctx
    for layer in range(DEPTH):
        last = layer == DEPTH - 1
        s5p = (s5_a_re[layer], s5_a_im[layer], s5_log_step[layer], s5_b_re[layer], s5_b_im[layer],
               s5_c_re[layer], s5_c_im[layer], s5_d[layer], s5_w_glu[layer], s5_b_glu[layer])
        lrup = (lru_conv_w[layer], lru_conv_b[layer], lru_w_a[layer], lru_b_a[layer],
                lru_w_x[layer], lru_b_x[layer], lru_lam[layer])

        sh_c, sc_c, gt_c = adaln(c_ctx, w_mod[layer], b_mod[layer])
        zc = (rmsnorm(hc, norm_g[layer]) * (1.0 + sc_c) + sh_c) @ w_in[layer]
        uc_s5, gc_s5, uc_lru, gc_lru = jnp.split(zc, cuts, axis=-1)
        yc_s5, fin_s5 = s5_branch(uc_s5, *s5p, init=None, need_out=not last)
        yc_lru, fin_lru = rglru_branch(uc_lru, *lrup, init=None, need_out=not last)

        sh, sc, gt = adaln(c, w_mod[layer], b_mod[layer])
        z = (rmsnorm(h, norm_g[layer]) * (1.0 + sc[:, None]) + sh[:, None]) @ w_in[layer]
        u_s5, g_s5, u_lru, g_lru = jnp.split(z, cuts, axis=-1)
        y_s5, _ = s5_branch(u_s5, *s5p, init=fin_s5, need_out=True)
        y_lru, _ = rglru_branch(to_col_major(u_lru, rows), *lrup, init=fin_lru, need_out=True)
        y_lru = to_row_major(y_lru, rows)
        h_next = h + gt[:, None] * mix_out(y_s5, g_s5, y_lru, g_lru, w_out[layer])
        if not last:
            hc = hc + gt_c * mix_out(yc_s5, gc_s5, yc_lru, gc_lru, w_out[layer])
        h = h_next
    return rmsnorm(h, final_g)
```

```python
import functools
import math

import jax
import jax.numpy as jnp
from jax import lax
from jax.experimental import pallas as pl
from jax.experimental.pallas import tpu as pltpu

F32 = jnp.float32
BF16 = jnp.bfloat16

EPS = 1e-6
GRID_W = 64
LRU_C = 8.0
S5_H = 16
S5_N = 64
S5_T = 16
S5_SEG = 256
LRU_HEAD = 64
GATE_TILE = 256
VMEM_LIMIT = 56 * 1024 * 1024


def _cparams(sem):
    return pltpu.CompilerParams(dimension_semantics=sem, vmem_limit_bytes=VMEM_LIMIT)


def _sigmoid(z):
    return 0.5 * (jnp.tanh(0.5 * z) + 1.0)


def _gelu_tanh(y):
    return 0.5 * y * (1.0 + jnp.tanh(math.sqrt(2.0 / math.pi) * (y + 0.044715 * (y * y * y))))


def _norm_mod(x, gs, sh):
    ms = jnp.mean(x * x, axis=-1, keepdims=True)
    return x * lax.rsqrt(ms + EPS) * gs + sh


def _mod_kernel(c_ref, w_ref, b_ref, o_ref):
    c = c_ref[...]
    s = c * _sigmoid(c)
    o_ref[...] = jnp.dot(s, w_ref[...], preferred_element_type=F32) + b_ref[...]


def _modulation(c8, w_mod, b_mod):
    d = c8.shape[1]
    n = w_mod.shape[1]
    nb = n // d
    return pl.pallas_call(
        _mod_kernel,
        grid=(nb,),
        in_specs=[pl.BlockSpec((8, d), lambda i: (0, 0)),
                  pl.BlockSpec((d, d), lambda i: (0, i)),
                  pl.BlockSpec((1, d), lambda i: (0, i))],
        out_specs=pl.BlockSpec((8, d), lambda i: (0, i)),
        out_shape=jax.ShapeDtypeStruct((8, n), F32),
        compiler_params=_cparams(("arbitrary",)),
    )(c8, w_mod, b_mod.reshape(1, n))


def _s5_inproj_kernel(x_ref, gs_ref, sh_ref, wt_ref, o_ref):
    ns, t_blk, _ = x_ref.shape
    groups = o_ref.shape[0]
    gs = gs_ref[...]
    sh = sh_ref[...]
    wt = wt_ref[...]
    for t in range(0, t_blk, 2):
        xa = _norm_mod(x_ref[:, t, :], gs, sh).astype(BF16)
        xb = _norm_mod(x_ref[:, t + 1, :], gs, sh).astype(BF16)
        xs = jnp.concatenate([xa, xb], axis=0)
        ut = lax.dot_general(wt, xs, (((1,), (1,)), ((), ())), preferred_element_type=F32)
        o_ref[:, t * S5_H:(t + 1) * S5_H, :] = ut[:, :ns].reshape(groups, S5_H, ns).astype(BF16)
        o_ref[:, (t + 1) * S5_H:(t + 2) * S5_H, :] = ut[:, ns:].reshape(groups, S5_H, ns).astype(BF16)


def _s5_inproj(x3, gs_rows, sh_rows, wt):
    ns, seg, d = x3.shape
    e = wt.shape[0]
    groups = e // S5_H
    nj = seg // S5_T
    return pl.pallas_call(
        _s5_inproj_kernel,
        grid=(nj,),
        in_specs=[pl.BlockSpec((ns, S5_T, d), lambda j: (0, j, 0)),
                  pl.BlockSpec((ns, d), lambda j: (0, 0)),
                  pl.BlockSpec((ns, d), lambda j: (0, 0)),
                  pl.BlockSpec((e, d), lambda j: (0, 0))],
        out_specs=pl.BlockSpec((groups, S5_T * S5_H, ns), lambda j: (0, 0, j)),
        out_shape=jax.ShapeDtypeStruct((groups, S5_T * S5_H, nj * ns), BF16),
        compiler_params=_cparams(("arbitrary",)),
    )(x3, gs_rows, sh_rows, wt)


def _cmul_add(dr, di, hr, hi, sr, si):
    return dr * hr - di * hi + sr, dr * hi + di * hr + si


def _lane_scan(vr, vi, dr, di, slot, count, reverse):
    n = vr.shape[1]
    k = 1
    while k < count:
        if reverse:
            sr = pltpu.roll(vr, n - k, 1)
            si = pltpu.roll(vi, n - k, 1)
            keep = slot < count - k
        else:
            sr = pltpu.roll(vr, k, 1)
            si = pltpu.roll(vi, k, 1)
            keep = slot >= k
        sr = jnp.where(keep, sr, 0.0)
        si = jnp.where(keep, si, 0.0)
        vr, vi = _cmul_add(dr, di, sr, si, vr, vi)
        dr, di = dr * dr - di * di, 2.0 * dr * di
        k *= 2
    return vr, vi


def _s5_core_kernel(xt_ref, xc_ref, toep_ref, bpow_ref, cpow_ref, d_ref, o_ref, s_ref, *, nseg, nctx):
    n = S5_N
    ns = xc_ref.shape[-1]
    nj = xt_ref.shape[-1] // ns
    bpow = bpow_ref[...]
    dcol = d_ref[...]
    dfr = jnp.broadcast_to(dcol[0:n], (n, ns))
    dfi = jnp.broadcast_to(dcol[n:2 * n], (n, ns))
    drr = jnp.broadcast_to(dcol[2 * n:3 * n], (n, ns))
    dri = jnp.broadcast_to(dcol[3 * n:4 * n], (n, ns))
    slot = lax.broadcasted_iota(jnp.int32, (n, ns), 1) % nseg

    sc = jnp.dot(bpow, xc_ref[...], preferred_element_type=F32)
    valid = slot < nctx
    cfr, cfi = _lane_scan(jnp.where(valid, sc[0:n], 0.0), jnp.where(valid, sc[n:2 * n], 0.0),
                          dfr, dfi, slot, nseg, False)
    crr, cri = _lane_scan(jnp.where(valid, sc[2 * n:3 * n], 0.0), jnp.where(valid, sc[3 * n:4 * n], 0.0),
                          drr, dri, slot, nseg, True)

    step = 2 if nj % 2 == 0 else 1
    for j in range(0, nj, step):
        cols = slice(j * ns, (j + step) * ns)
        s_ref[:, cols] = jnp.dot(bpow, xt_ref[:, cols], preferred_element_type=F32)

    zero = jnp.zeros((n, ns), F32)
    efr, efi, err, eri = zero, zero, zero, zero
    for j in range(nj):
        cols = slice(j * ns, (j + 1) * ns)
        efr, efi = _cmul_add(dfr, dfi, efr, efi, s_ref[0:n, cols], s_ref[n:2 * n, cols])
    for j in range(nj - 1, -1, -1):
        cols = slice(j * ns, (j + 1) * ns)
        err, eri = _cmul_add(drr, dri, err, eri, s_ref[2 * n:3 * n, cols], s_ref[3 * n:4 * n, cols])

    sfr, sfi, srr, sri = dfr, dfi, drr, dri
    k = 1
    while k < nj:
        sfr, sfi = sfr * sfr - sfi * sfi, 2.0 * sfr * sfi
        srr, sri = srr * srr - sri * sri, 2.0 * srr * sri
        k *= 2

    first = slot == 0
    last = slot == nseg - 1
    vfr = jnp.where(first, pltpu.roll(cfr, ns - (nctx - 1), 1) if nctx > 1 else cfr, pltpu.roll(efr, 1, 1))
    vfi = jnp.where(first, pltpu.roll(cfi, ns - (nctx - 1), 1) if nctx > 1 else cfi, pltpu.roll(efi, 1, 1))
    vrr = jnp.where(last, pltpu.roll(crr, nseg - 1, 1) if nseg > 1 else crr, pltpu.roll(err, ns - 1, 1))
    vri = jnp.where(last, pltpu.roll(cri, nseg - 1, 1) if nseg > 1 else cri, pltpu.roll(eri, ns - 1, 1))
    hfr, hfi = _lane_scan(vfr, vfi, sfr, sfi, slot, nseg, False)
    hrr, hri = _lane_scan(vrr, vri, srr, sri, slot, nseg, True)

    for j in range(nj):
        cols = slice(j * ns, (j + 1) * ns)
        nr, ni = _cmul_add(dfr, dfi, hfr, hfi, s_ref[0:n, cols], s_ref[n:2 * n, cols])
        s_ref[0:n, cols] = hfr
        s_ref[n:2 * n, cols] = hfi
        hfr, hfi = nr, ni
    for j in range(nj - 1, -1, -1):
        cols = slice(j * ns, (j + 1) * ns)
        nr, ni = _cmul_add(drr, dri, hrr, hri, s_ref[2 * n:3 * n, cols], s_ref[3 * n:4 * n, cols])
        s_ref[2 * n:3 * n, cols] = hrr
        s_ref[3 * n:4 * n, cols] = hri
        hrr, hri = nr, ni

    toep = toep_ref[...]
    cpow = cpow_ref[...]
    for j in range(0, nj, step):
        cols = slice(j * ns, (j + step) * ns)
        y = (jnp.dot(toep, xt_ref[:, cols], preferred_element_type=F32)
             + jnp.dot(cpow, s_ref[:, cols].astype(BF16), preferred_element_type=F32))
        for q in range(step):
            o_ref[j + q] = y[:, q * ns:(q + 1) * ns].reshape(S5_T, S5_H, ns)


def _s5_core(xt, xct, toep, bpow, cpow, dvec, nseg, nctx):
    groups, rows, lanes = xt.shape
    ns = xct.shape[-1]
    nj = lanes // ns
    kern = functools.partial(_s5_core_kernel, nseg=nseg, nctx=nctx)
    return pl.pallas_call(
        kern,
        grid=(groups,),
        in_specs=[pl.BlockSpec((None, rows, lanes), lambda g: (g, 0, 0)),
                  pl.BlockSpec((None, rows, ns), lambda g: (g, 0, 0)),
                  pl.BlockSpec((None, rows, rows), lambda g: (g, 0, 0)),
                  pl.BlockSpec((None, rows, rows), lambda g: (g, 0, 0)),
                  pl.BlockSpec((None, rows, rows), lambda g: (g, 0, 0)),
                  pl.BlockSpec((None, rows, 1), lambda g: (g, 0, 0))],
        out_specs=pl.BlockSpec((nj, S5_T, None, S5_H, ns), lambda g: (0, 0, g, 0, 0)),
        out_shape=jax.ShapeDtypeStruct((nj, S5_T, groups, S5_H, ns), F32),
        scratch_shapes=[pltpu.VMEM((rows, lanes), F32)],
        compiler_params=_cparams(("arbitrary",)),
    )(xt, xct, toep, bpow, cpow, dvec)


def _s5_post_kernel(y_ref, x_ref, gs_ref, sh_ref, wg_ref, wglu_ref, bglu_ref, wo_ref, o_ref):
    t_blk = y_ref.shape[0]
    ns = x_ref.shape[0]
    gs = gs_ref[...]
    sh = sh_ref[...]
    ys = jnp.concatenate([y_ref[t].T for t in range(t_blk)], axis=0)
    xs = jnp.concatenate([_norm_mod(x_ref[:, t, :], gs, sh).astype(BF16) for t in range(t_blk)], axis=0)
    gate = jnp.dot(xs, wg_ref[...], preferred_element_type=F32)
    yg = _gelu_tanh(ys)
    z = jnp.dot(yg.astype(BF16), wglu_ref[...], preferred_element_type=F32) + bglu_ref[...]
    m = yg * _sigmoid(z) * (gate * _sigmoid(gate))
    p = jnp.dot(m.astype(BF16), wo_ref[...], preferred_element_type=F32)
    for t in range(t_blk):
        o_ref[:, t, :] = p[t * ns:(t + 1) * ns]


def _s5_post(y4, x3, gs_rows, sh_rows, wg, wglu, bglu, wo, t_blk):
    nj, t_all, e, ns = y4.shape
    _, seg, d = x3.shape
    nt = t_all // t_blk
    return pl.pallas_call(
        _s5_post_kernel,
        grid=(nj, nt),
        in_specs=[pl.BlockSpec((None, t_blk, e, ns), lambda j, q: (j, q, 0, 0)),
                  pl.BlockSpec((ns, t_blk, d), lambda j, q: (0, j * nt + q, 0)),
                  pl.BlockSpec((ns, d), lambda j, q: (0, 0)),
                  pl.BlockSpec((ns, d), lambda j, q: (0, 0)),
                  pl.BlockSpec((d, e), lambda j, q: (0, 0)),
                  pl.BlockSpec((e, e), lambda j, q: (0, 0)),
                  pl.BlockSpec((1, e), lambda j, q: (0, 0)),
                  pl.BlockSpec((e, d), lambda j, q: (0, 0))],
        out_specs=pl.BlockSpec((ns, t_blk, d), lambda j, q: (0, j * nt + q, 0)),
        out_shape=jax.ShapeDtypeStruct((ns, seg, d), F32),
        compiler_params=_cparams(("arbitrary", "arbitrary")),
    )(y4, x3, gs_rows, sh_rows, wg, wglu, bglu, wo)


def _softplus(z):
    return jnp.maximum(z, 0.0) + jnp.log1p(jnp.exp(-jnp.abs(z)))


def _gate_matmul(xc, wg_ref):
    xb = xc.astype(BF16)
    nq = wg_ref.shape[0]
    return [jnp.dot(xb[:, q * GATE_TILE:(q + 1) * GATE_TILE], wg_ref[q], preferred_element_type=F32)
            for q in range(nq)]


def _gate_cols(zs, k):
    return jnp.concatenate([z[:, k * GATE_TILE:(k + 1) * GATE_TILE] for z in zs], axis=1)


def _lru_coeffs(xc, za, zx, ba, bx, sp_c):
    r = _sigmoid(za + ba)
    gi = _sigmoid(zx + bx)
    a = jnp.exp(-(r * sp_c))
    return a, jnp.sqrt(1.0 - a * a) * (gi * xc)


def _lru_ctx_kernel(x_ref, gs_ref, sh_ref, wu_ref, cw_ref, cb_ref, wg_ref, gb_ref, lam_ref,
                    ff_ref, fr_ref, af_s, bf_s, ar_s, br_s):
    rows = x_ref.shape[0]
    npos = rows // 8
    xn = _norm_mod(x_ref[...], gs_ref[...], sh_ref[...]).astype(BF16)
    u = jnp.dot(xn, wu_ref[...], preferred_element_type=F32)
    pos = lax.broadcasted_iota(jnp.int32, u.shape, 0) // 8
    um1 = jnp.where(pos >= 1, pltpu.roll(u, 8, 0), 0.0)
    up1 = jnp.where(pos < npos - 1, pltpu.roll(u, rows - 8, 0), 0.0)
    up2 = jnp.where(pos < npos - 2, pltpu.roll(u, rows - 16, 0), 0.0)
    cw = cw_ref[...]
    xc = cw[0:1] * um1 + cw[1:2] * u + cw[2:3] * up1 + cw[3:4] * up2 + cb_ref[...]
    zs = _gate_matmul(xc, wg_ref)
    gb = gb_ref[...]
    sp = LRU_C * _softplus(-lam_ref[...])
    a, b = _lru_coeffs(xc, _gate_cols(zs, 0), _gate_cols(zs, 1), gb[0:1], gb[1:2], sp[0:1])
    af_s[...] = a
    bf_s[...] = b
    a, b = _lru_coeffs(xc, _gate_cols(zs, 2), _gate_cols(zs, 3), gb[2:3], gb[3:4], sp[1:2])
    ar_s[...] = a
    br_s[...] = b

    def body(p, carry):
        hf, hr, pr = carry
        r0 = pl.multiple_of(p * 8, 8)
        hf = af_s[pl.ds(r0, 8), :] * hf + bf_s[pl.ds(r0, 8), :]
        hr = hr + pr * br_s[pl.ds(r0, 8), :]
        pr = pr * ar_s[pl.ds(r0, 8), :]
        return hf, hr, pr

    zero = jnp.zeros((8, u.shape[1]), F32)
    hf, hr, _ = lax.fori_loop(0, npos, body, (zero, zero, zero + 1.0))
    ff_ref[...] = hf
    fr_ref[...] = hr


def _lru_ctx(xp, gs, sh, wu, cw, cb, wg, gb, lam):
    rows, d = xp.shape
    e = wu.shape[1]
    nq = e // GATE_TILE
    full = lambda a: pl.BlockSpec(a.shape, lambda q, nd=a.ndim: (0,) * nd)
    cols = lambda a: pl.BlockSpec((a.shape[0], GATE_TILE), lambda q: (0, q))
    return pl.pallas_call(
        _lru_ctx_kernel,
        grid=(nq,),
        in_specs=[full(xp), full(gs), full(sh), cols(wu), cols(cw), cols(cb),
                  pl.BlockSpec((1,) + wg.shape[1:], lambda q: (q, 0, 0)), cols(gb), cols(lam)],
        out_specs=[pl.BlockSpec((8, GATE_TILE), lambda q: (0, q)), pl.BlockSpec((8, GATE_TILE), lambda q: (0, q))],
        out_shape=[jax.ShapeDtypeStruct((8, e), F32), jax.ShapeDtypeStruct((8, e), F32)],
        scratch_shapes=[pltpu.VMEM((rows, GATE_TILE), F32) for _ in range(4)],
        compiler_params=_cparams(("arbitrary",)),
    )(xp, gs, sh, wu, cw, cb, wg, gb, lam)


def _lru_pass_a_kernel(x_ref, gs_ref, sh_ref, wu_ref, cw_ref, cb_ref, wg_ref, gb_ref, lam_ref,
                       xc_ref, pf_ref, hf_ref, pr_ref, hr_ref, ring_ref, *, nrow):
    i = pl.program_id(0)
    nb, w, d = x_ref.shape
    rows = nb * w
    xn = _norm_mod(x_ref[...].reshape(rows, d), gs_ref[...], sh_ref[...]).astype(BF16)
    u = jnp.dot(xn, wu_ref[...], preferred_element_type=F32)
    ring_ref[i & 3] = u

    @pl.when(i == 0)
    def _():
        col = lax.broadcasted_iota(jnp.int32, u.shape, 0) % w
        ring_ref[0] = jnp.where(col == 0, 0.0, pltpu.roll(u, 1, 0))

    @pl.when(i > nrow)
    def _():
        col = lax.broadcasted_iota(jnp.int32, u.shape, 0) % w
        ring_ref[i & 3] = jnp.where(col == w - 1, 0.0, pltpu.roll(u, rows - 1, 0))

    @pl.when(i >= 3)
    def _():
        cw = cw_ref[...]
        xc = (cw[0:1] * ring_ref[(i - 3) & 3] + cw[1:2] * ring_ref[(i - 2) & 3]
              + cw[2:3] * ring_ref[(i - 1) & 3] + cw[3:4] * ring_ref[i & 3] + cb_ref[...])
        xc_ref[...] = xc
        zs = _gate_matmul(xc, wg_ref)
        gb = gb_ref[...]
        sp = LRU_C * _softplus(-lam_ref[...])
        af, bf = _lru_coeffs(xc, _gate_cols(zs, 0), _gate_cols(zs, 1), gb[0:1], gb[1:2], sp[0:1])
        ar, br = _lru_coeffs(xc, _gate_cols(zs, 2), _gate_cols(zs, 3), gb[2:3], gb[3:4], sp[1:2])

        @pl.when(i == 3)
        def _():
            pf_ref[...] = af
            hf_ref[...] = bf
            pr_ref[...] = ar
            hr_ref[...] = br

        @pl.when(i > 3)
        def _():
            hf_ref[...] = af * hf_ref[...] + bf
            pf_ref[...] = pf_ref[...] * af
            hr_ref[...] = hr_ref[...] + pr_ref[...] * br
            pr_ref[...] = pr_ref[...] * ar


def _lru_pass_a(x4, gs_rows, sh_rows, wu, cw, cb, wg, gb, lam):
    nb, nrow, w, d = x4.shape
    e = wu.shape[1]
    rows = nb * w
    kern = functools.partial(_lru_pass_a_kernel, nrow=nrow)
    const = lambda a: pl.BlockSpec(a.shape, lambda i, nd=a.ndim: (0,) * nd)
    acc = pl.BlockSpec((rows, e), lambda i: (0, 0))
    acc_shape = jax.ShapeDtypeStruct((rows, e), F32)
    return pl.pallas_call(
        kern,
        grid=(nrow + 3,),
        in_specs=[pl.BlockSpec((nb, None, w, d), lambda i: (0, (i + nrow - 1) % nrow, 0, 0)),
                  const(gs_rows), const(sh_rows), const(wu), const(cw), const(cb), const(wg), const(gb),
                  const(lam)],
        out_specs=[pl.BlockSpec((None, rows, e), lambda i: (jnp.maximum(i - 3, 0), 0, 0)), acc, acc, acc, acc],
        out_shape=[jax.ShapeDtypeStruct((nrow, rows, e), F32), acc_shape, acc_shape, acc_shape, acc_shape],
        scratch_shapes=[pltpu.VMEM((4, rows, e), F32)],
        compiler_params=_cparams(("arbitrary",)),
    )(x4, gs_rows, sh_rows, wu, cw, cb, wg, gb, lam)


def _lru_stitch_kernel(pf_ref, hf_ref, pr_ref, hr_ref, ff_ref, fr_ref, sf_ref, sr_ref, *, w):
    rows = pf_ref.shape[0]
    col = lax.broadcasted_iota(jnp.int32, pf_ref.shape, 0) % w
    p, h = pf_ref[...], hf_ref[...]
    k = 1
    while k < w:
        keep = col >= k
        ps = jnp.where(keep, pltpu.roll(p, k, 0), 1.0)
        hs = jnp.where(keep, pltpu.roll(h, k, 0), 0.0)
        p, h = p * ps, p * hs + h
        k *= 2
    fin = ff_ref[...]
    sf_ref[...] = jnp.where(col == 0, fin, pltpu.roll(p, 1, 0) * fin + pltpu.roll(h, 1, 0))
    p, h = pr_ref[...], hr_ref[...]
    k = 1
    while k < w:
        keep = col < w - k
        ps = jnp.where(keep, pltpu.roll(p, rows - k, 0), 1.0)
        hs = jnp.where(keep, pltpu.roll(h, rows - k, 0), 0.0)
        p, h = p * ps, p * hs + h
        k *= 2
    fin = fr_ref[...]
    sr_ref[...] = jnp.where(col == w - 1, fin, pltpu.roll(p, rows - 1, 0) * fin + pltpu.roll(h, rows - 1, 0))


def _lru_stitch(pf, hf, pr, hr, ff_rows, fr_rows, w):
    rows, e = pf.shape
    big = pl.BlockSpec((rows, e), lambda i: (0, 0))
    shp = jax.ShapeDtypeStruct((rows, e), F32)
    return pl.pallas_call(
        functools.partial(_lru_stitch_kernel, w=w),
        grid=(1,),
        in_specs=[big] * 6,
        out_specs=[big, big],
        out_shape=[shp, shp],
        compiler_params=_cparams(("arbitrary",)),
    )(pf, hf, pr, hr, ff_rows, fr_rows)


def _lru_fwd_kernel(xc_ref, wg_ref, gb_ref, lam_ref, hs_ref, o_ref, h_ref):
    i = pl.program_id(0)

    @pl.when(i == 0)
    def _():
        h_ref[...] = hs_ref[...]

    xc = xc_ref[...]
    zs = _gate_matmul(xc, wg_ref)
    gb = gb_ref[...]
    sp = LRU_C * _softplus(-lam_ref[...])
    a, b = _lru_coeffs(xc, _gate_cols(zs, 0), _gate_cols(zs, 1), gb[0:1], gb[1:2], sp)
    h = a * h_ref[...] + b
    h_ref[...] = h
    o_ref[...] = h


def _lru_fwd(xc, wg, gb, lam, hs):
    nrow, rows, e = xc.shape
    const = lambda a: pl.BlockSpec(a.shape, lambda i, nd=a.ndim: (0,) * nd)
    return pl.pallas_call(
        _lru_fwd_kernel,
        grid=(nrow,),
        in_specs=[pl.BlockSpec((None, rows, e), lambda i: (i, 0, 0)), const(wg), const(gb), const(lam), const(hs)],
        out_specs=pl.BlockSpec((None, rows, e), lambda i: (i, 0, 0)),
        out_shape=jax.ShapeDtypeStruct((nrow, rows, e), F32),
        scratch_shapes=[pltpu.VMEM((rows, e), F32)],
        compiler_params=_cparams(("arbitrary",)),
    )(xc, wg, gb, lam, hs)


def _lru_rev_out_kernel(xc_ref, hf_ref, x_ref, p_ref, gs_ref, sh_ref, gt_ref, wg_ref, gb_ref, lam_ref, hs_ref,
                        wgl_ref, wo_ref, fg_ref, o_ref, h_ref):
    i = pl.program_id(0)
    nb, w, d = x_ref.shape
    rows = nb * w

    @pl.when(i == 0)
    def _():
        h_ref[...] = hs_ref[...]

    xc = xc_ref[...]
    zs = _gate_matmul(xc, wg_ref)
    gb = gb_ref[...]
    sp = LRU_C * _softplus(-lam_ref[...])
    a, b = _lru_coeffs(xc, _gate_cols(zs, 0), _gate_cols(zs, 1), gb[0:1], gb[1:2], sp)
    h = a * h_ref[...] + b
    h_ref[...] = h
    y = hf_ref[...] + h

    x = x_ref[...].reshape(rows, d)
    xn = _norm_mod(x, gs_ref[...], sh_ref[...]).astype(BF16)
    gate = jnp.dot(xn, wgl_ref[...], preferred_element_type=F32)
    m = y * (gate * _sigmoid(gate))
    mix = p_ref[...].reshape(rows, d) + jnp.dot(m.astype(BF16), wo_ref[...], preferred_element_type=F32)
    res = x + gt_ref[...] * mix
    ms = jnp.mean(res * res, axis=-1, keepdims=True)
    o_ref[...] = (res * lax.rsqrt(ms + EPS) * fg_ref[...]).reshape(nb, w, d)


def _lru_rev_out(xc, hf, x4, p4, gs_rows, sh_rows, gt_rows, wg, gb, lam, hs, wgl, wo, fg):
    nb, nrow, w, d = x4.shape
    _, rows, e = xc.shape
    const = lambda a: pl.BlockSpec(a.shape, lambda i, nd=a.ndim: (0,) * nd)
    rev3 = pl.BlockSpec((None, rows, e), lambda i: (nrow - 1 - i, 0, 0))
    rev4 = pl.BlockSpec((nb, None, w, d), lambda i: (0, nrow - 1 - i, 0, 0))
    return pl.pallas_call(
        _lru_rev_out_kernel,
        grid=(nrow,),
        in_specs=[rev3, rev3, rev4, rev4, const(gs_rows), const(sh_rows), const(gt_rows), const(wg), const(gb),
                  const(lam), const(hs), const(wgl), const(wo), const(fg)],
        out_specs=rev4,
        out_shape=jax.ShapeDtypeStruct((nb, nrow, w, d), F32),
        scratch_shapes=[pltpu.VMEM((rows, e), F32)],
        compiler_params=_cparams(("arbitrary",)),
    )(xc, hf, x4, p4, gs_rows, sh_rows, gt_rows, wg, gb, lam, hs, wgl, wo, fg)


def _s5_matrices(a_re, a_im, log_step, b_re, b_im, c_re, c_im, d_skip):
    hp = lax.Precision.HIGHEST
    t_len = S5_T
    ndir, groups, n = a_re.shape
    h = b_re.shape[-1]
    step = jnp.exp(log_step.astype(F32))[..., None]
    lr = a_re.astype(F32) * step
    li = a_im.astype(F32) * step
    k = jnp.arange(t_len + 1, dtype=F32)[:, None, None, None]
    mag = jnp.exp(k * lr)
    pw_r = mag * jnp.cos(k * li)
    pw_i = mag * jnp.sin(k * li)
    ar = a_re.astype(F32)
    ai = a_im.astype(F32)
    den = ar * ar + ai * ai
    nr = pw_r[1] - 1.0
    ni = pw_i[1]
    coef_r = (nr * ar + ni * ai) / den
    coef_i = (ni * ar - nr * ai) / den
    br = b_re.astype(F32)
    bi = b_im.astype(F32)
    bb_r = coef_r[..., None] * br - coef_i[..., None] * bi
    bb_i = coef_r[..., None] * bi + coef_i[..., None] * br
    cr = c_re.astype(F32)
    ci = c_im.astype(F32)
    cp_r = cr[None] * pw_r[:, :, :, None, :] - ci[None] * pw_i[:, :, :, None, :]
    cp_i = cr[None] * pw_i[:, :, :, None, :] + ci[None] * pw_r[:, :, :, None, :]
    kern = (jnp.einsum("kdgpn,dgnh->kdgph", cp_r[:t_len], bb_r, precision=hp)
            - jnp.einsum("kdgpn,dgnh->kdgph", cp_i[:t_len], bb_i, precision=hp))
    tt = jnp.arange(t_len)
    lag = tt[:, None] - tt[None, :]
    kf = jnp.where((lag >= 0)[:, :, None, None, None], kern[jnp.clip(lag, 0, t_len - 1), 0], 0.0)
    kr = jnp.where((lag <= 0)[:, :, None, None, None], kern[jnp.clip(-lag, 0, t_len - 1), 1], 0.0)
    toep = jnp.transpose(kf + kr, (2, 0, 3, 1, 4))
    eye_t = jnp.eye(t_len, dtype=F32)[None, :, None, :, None]
    eye_h = jnp.eye(h, dtype=F32)[None, None, :, None, :]
    toep = toep + d_skip.astype(F32).reshape(groups, 1, h, 1, 1) * eye_t * eye_h
    toep = toep.reshape(groups, t_len * h, t_len * h)
    bp_r = pw_r[:, :, :, :, None] * bb_r[None] - pw_i[:, :, :, :, None] * bb_i[None]
    bp_i = pw_r[:, :, :, :, None] * bb_i[None] + pw_i[:, :, :, :, None] * bb_r[None]

    def inj(p, d, order):
        q = jnp.transpose(p[order, d], (1, 2, 0, 3))
        return q.reshape(groups, n, t_len * h)

    fwd_order = jnp.arange(t_len - 1, -1, -1)
    rev_order = jnp.arange(t_len)
    bpow = jnp.concatenate([inj(bp_r, 0, fwd_order), inj(bp_i, 0, fwd_order),
                            inj(bp_r, 1, rev_order), inj(bp_i, 1, rev_order)], axis=1)

    def rd(p, d, order):
        q = jnp.transpose(p[order, d], (1, 0, 2, 3))
        return q.reshape(groups, t_len * h, n)

    f_ord = jnp.arange(1, t_len + 1)
    r_ord = jnp.arange(t_len, 0, -1)
    cpow = jnp.concatenate([rd(cp_r, 0, f_ord), -rd(cp_i, 0, f_ord),
                            rd(cp_r, 1, r_ord), -rd(cp_i, 1, r_ord)], axis=2)
    dvec = jnp.concatenate([pw_r[t_len, 0], pw_i[t_len, 0], pw_r[t_len, 1], pw_i[t_len, 1]], axis=1)[..., None]
    return toep.astype(BF16), bpow.astype(BF16), cpow.astype(BF16), dvec


def _gate_tiles(w_a, w_x):
    ndir, heads, hd, _ = w_a.shape
    per = GATE_TILE // hd
    nq = heads // per

    def tiles(w):
        wq = w.reshape(nq, per, hd, hd)
        eye = jnp.eye(per, dtype=w.dtype)
        blk = wq[:, :, :, None, :] * eye[None, :, None, :, None]
        return blk.reshape(nq, GATE_TILE, GATE_TILE)

    return jnp.concatenate([tiles(w_a[0]), tiles(w_x[0]), tiles(w_a[1]), tiles(w_x[1])], axis=2).astype(BF16)


def kernel(x, c, ctx, c_ctx, w_mod, b_mod, norm_g, w_in, s5_a_re, s5_a_im, s5_log_step, s5_b_re, s5_b_im,
           s5_c_re, s5_c_im, s5_d, s5_w_glu, s5_b_glu, lru_conv_w, lru_conv_b, lru_w_a, lru_b_a, lru_w_x,
           lru_b_x, lru_lam, w_out, final_g):
    bsz, seq, d = x.shape
    nctx_tok = ctx.shape[1]
    e = s5_w_glu.shape[-1]
    assert w_mod.shape[0] == 1, "single-layer block"
    assert bsz <= 8 and seq % S5_SEG == 0 and seq % GRID_W == 0 and nctx_tok % S5_T == 0
    nseg = seq // S5_SEG
    ns = bsz * nseg
    nctx = nctx_tok // S5_T
    assert nctx <= nseg
    nrow = seq // GRID_W

    c8 = jnp.zeros((8, d), F32).at[:bsz].set(c).at[bsz].set(c_ctx)
    mod = _modulation(c8, w_mod[0], b_mod[0])
    sh, sc, gt = mod[:, :d], mod[:, d:2 * d], mod[:, 2 * d:]
    gs = norm_g[0][None, :] * (1.0 + sc)

    w_in0 = w_in[0]
    wt_us5 = w_in0[:, :e].T.astype(BF16)
    w_gs5 = w_in0[:, e:2 * e].astype(BF16)
    w_ulru = w_in0[:, 2 * e:3 * e].astype(BF16)
    w_glru = w_in0[:, 3 * e:].astype(BF16)
    w_glu = s5_w_glu[0].astype(BF16)
    wo_s5 = w_out[0][:e].astype(BF16)
    wo_lru = w_out[0][e:].astype(BF16)

    toep, bpow, cpow, dvec = _s5_matrices(s5_a_re[0], s5_a_im[0], s5_log_step[0], s5_b_re[0], s5_b_im[0],
                                          s5_c_re[0], s5_c_im[0], s5_d[0])
    x3 = x.reshape(ns, S5_SEG, d)
    gs_seg = jnp.repeat(gs[:bsz], nseg, axis=0)
    sh_seg = jnp.repeat(sh[:bsz], nseg, axis=0)
    xt = _s5_inproj(x3, gs_seg, sh_seg, wt_us5)
    ctx_pad = jnp.zeros((bsz, nseg, S5_T, d), F32).at[:, :nctx].set(ctx.reshape(bsz, nctx, S5_T, d))
    gs_c = jnp.broadcast_to(gs[bsz:bsz + 1], (ns, d))
    sh_c = jnp.broadcast_to(sh[bsz:bsz + 1], (ns, d))
    xct = _s5_inproj(ctx_pad.reshape(ns, S5_T, d), gs_c, sh_c, wt_us5)
    y5 = _s5_core(xt, xct, toep, bpow, cpow, dvec, nseg, nctx)
    nj = S5_SEG // S5_T
    p3 = _s5_post(y5.reshape(nj, S5_T, e, ns), x3, gs_seg, sh_seg, w_gs5, w_glu,
                  s5_b_glu[0].reshape(1, e), wo_s5, t_blk=8)

    wg_all = _gate_tiles(lru_w_a[0], lru_w_x[0])
    gb_all = jnp.stack([lru_b_a[0, 0], lru_b_x[0, 0], lru_b_a[0, 1], lru_b_x[0, 1]])
    cw = lru_conv_w[0]
    cb = lru_conv_b[0].reshape(1, e)
    lam = lru_lam[0]
    ctx_p = jnp.zeros((8, nctx_tok, d), F32).at[:bsz].set(ctx).transpose(1, 0, 2).reshape(nctx_tok * 8, d)
    ff, fr = _lru_ctx(ctx_p, gs[bsz:bsz + 1], sh[bsz:bsz + 1], w_ulru, cw, cb, wg_all, gb_all, lam)
    x4 = x.reshape(bsz, nrow, GRID_W, d)
    gs_col = jnp.repeat(gs[:bsz], GRID_W, axis=0)
    sh_col = jnp.repeat(sh[:bsz], GRID_W, axis=0)
    gt_col = jnp.repeat(gt[:bsz], GRID_W, axis=0)
    xc, pf, hf, pr, hr = _lru_pass_a(x4, gs_col, sh_col, w_ulru, cw, cb, wg_all, gb_all, lam)
    hs_f, hs_r = _lru_stitch(pf, hf, pr, hr, jnp.repeat(ff[:bsz], GRID_W, axis=0),
                             jnp.repeat(fr[:bsz], GRID_W, axis=0), GRID_W)
    h_f = _lru_fwd(xc, wg_all[:, :, :2 * GATE_TILE], gb_all[0:2], lam[0:1], hs_f)
    out4 = _lru_rev_out(xc, h_f, x4, p3.reshape(bsz, nrow, GRID_W, d), gs_col, sh_col, gt_col,
                        wg_all[:, :, 2 * GATE_TILE:], gb_all[2:4], lam[1:2], hs_r, w_glru, wo_lru,
                        final_g.reshape(1, d))
    return out4.reshape(bsz, seq, d)
```

```python
import functools
import math

import jax
import jax.numpy as jnp
from jax import lax
from jax.experimental import pallas as pl
from jax.experimental.pallas import tpu as pltpu

F32 = jnp.float32
BF16 = jnp.bfloat16

EPS = 1e-6
TINY = 1e-30
GRID_W = 64
LRU_C = 8.0
S5_H = 16
S5_N = 64
S5_T = 16
S5_SEG = 256
LRU_HEAD = 64
GATE_TILE = 256
VMEM_LIMIT = 56 * 1024 * 1024


def _cparams(sem):
    return pltpu.CompilerParams(dimension_semantics=sem, vmem_limit_bytes=VMEM_LIMIT)


def _sigmoid(z):
    return 0.5 * (jnp.tanh(0.5 * z) + 1.0)


def _gelu_tanh(y):
    return 0.5 * y * (1.0 + jnp.tanh(math.sqrt(2.0 / math.pi) * (y + 0.044715 * (y * y * y))))


def _norm_mod(x, gs, sh):
    ms = jnp.mean(x * x, axis=-1, keepdims=True)
    return x * lax.rsqrt(ms + EPS) * gs + sh


def _mod_kernel(c_ref, w_ref, b_ref, o_ref):
    c = c_ref[...]
    s = c * _sigmoid(c)
    o_ref[...] = jnp.dot(s, w_ref[...], preferred_element_type=F32) + b_ref[...]


def _modulation(c8, w_mod, b_mod):
    d = c8.shape[1]
    n = w_mod.shape[1]
    nb = n // d
    return pl.pallas_call(
        _mod_kernel,
        grid=(nb,),
        in_specs=[pl.BlockSpec((8, d), lambda i: (0, 0)),
                  pl.BlockSpec((d, d), lambda i: (0, i)),
                  pl.BlockSpec((1, d), lambda i: (0, i))],
        out_specs=pl.BlockSpec((8, d), lambda i: (0, i)),
        out_shape=jax.ShapeDtypeStruct((8, n), F32),
        compiler_params=_cparams(("arbitrary",)),
    )(c8, w_mod, b_mod.reshape(1, n))


def _s5_inproj_kernel(x_ref, gs_ref, sh_ref, wt_ref, o_ref):
    ns, d = gs_ref.shape
    groups = o_ref.shape[0]
    gs = gs_ref[...]
    sh = sh_ref[...]
    wt = wt_ref[...]
    for t in range(0, S5_T, 2):
        xs = jnp.concatenate([_norm_mod(x_ref[:, t * d:(t + 1) * d], gs, sh),
                              _norm_mod(x_ref[:, (t + 1) * d:(t + 2) * d], gs, sh)], axis=0)
        ut = lax.dot_general(wt, xs.astype(BF16), (((1,), (1,)), ((), ())), preferred_element_type=F32)
        o_ref[:, t * S5_H:(t + 1) * S5_H, :] = ut[:, :ns].reshape(groups, S5_H, ns).astype(BF16)
        o_ref[:, (t + 1) * S5_H:(t + 2) * S5_H, :] = ut[:, ns:].reshape(groups, S5_H, ns).astype(BF16)


def _s5_inproj(x2, gs_rows, sh_rows, wt):
    ns, d = gs_rows.shape
    seg = x2.shape[1] // d
    e = wt.shape[0]
    groups = e // S5_H
    nj = seg // S5_T
    return pl.pallas_call(
        _s5_inproj_kernel,
        grid=(nj,),
        in_specs=[pl.BlockSpec((ns, S5_T * d), lambda j: (0, j)),
                  pl.BlockSpec((ns, d), lambda j: (0, 0)),
                  pl.BlockSpec((ns, d), lambda j: (0, 0)),
                  pl.BlockSpec((e, d), lambda j: (0, 0))],
        out_specs=pl.BlockSpec((groups, S5_T * S5_H, ns), lambda j: (0, 0, j)),
        out_shape=jax.ShapeDtypeStruct((groups, S5_T * S5_H, nj * ns), BF16),
        compiler_params=_cparams(("arbitrary",)),
        name="s5_inproj",
    )(x2, gs_rows, sh_rows, wt)


def _cmul_add(dr, di, hr, hi, sr, si):
    return dr * hr - di * hi + sr, dr * hi + di * hr + si


def _lane_scan(vr, vi, dr, di, slot, count, reverse):
    n = vr.shape[1]
    k = 1
    while k < count:
        if reverse:
            sr = pltpu.roll(vr, n - k, 1)
            si = pltpu.roll(vi, n - k, 1)
            keep = slot < count - k
        else:
            sr = pltpu.roll(vr, k, 1)
            si = pltpu.roll(vi, k, 1)
            keep = slot >= k
        sr = jnp.where(keep, sr, 0.0)
        si = jnp.where(keep, si, 0.0)
        vr, vi = _cmul_add(dr, di, sr, si, vr, vi)
        dr, di = dr * dr - di * di, 2.0 * dr * di
        k *= 2
    return vr, vi


def _csq(r, i):
    return r * r - i * i, 2.0 * r * i


def _cpow_bits(e, pows):
    pr = jnp.ones(e.shape, F32)
    pi = jnp.zeros(e.shape, F32)
    for bit, (qr, qi) in pows:
        on = (e & bit) != 0
        pr, pi = jnp.where(on, pr * qr - pi * qi, pr), jnp.where(on, pr * qi + pi * qr, pi)
    return pr, pi


def _s5_discretize(ar, ai, ls):
    step = jnp.exp(ls)
    mag = jnp.exp(ar * step)
    pr = mag * jnp.cos(ai * step)
    pi = mag * jnp.sin(ai * step)
    den = ar * ar + ai * ai
    nr = pr - 1.0
    return pr, pi, (nr * ar + pi * ai) / den, (pi * ar - nr * ai) / den


def _s5_operators(pcol_ref, prow_ref, bt_ref, ct1_ref, ct2_ref, dsk_ref):
    n, t_len, th = S5_N, S5_T, S5_T * S5_H
    hp = lax.Precision.HIGHEST
    lane_t = lax.broadcasted_iota(jnp.int32, (1, th), 1) // S5_H
    bits = [1 << b for b in range(t_len.bit_length() - 1)]

    blocks, bbs, decay = [], [], []
    for d in range(2):
        pr, pi, cr, ci = _s5_discretize(pcol_ref[3 * d], pcol_ref[3 * d + 1], pcol_ref[3 * d + 2])
        pows, q = [], (pr, pi)
        for bit in bits:
            pows.append((bit, q))
            q = _csq(*q)
        decay.append(q)
        btr = bt_ref[2 * d * n:(2 * d + 1) * n, :]
        bti = bt_ref[(2 * d + 1) * n:(2 * d + 2) * n, :]
        bbr = cr * btr - ci * bti
        bbi = cr * bti + ci * btr
        e = (t_len - 1 - lane_t) if d == 0 else lane_t
        wr, wi = _cpow_bits(jnp.broadcast_to(e, (n, th)), pows)
        blocks += [wr * bbr - wi * bbi, wr * bbi + wi * bbr]
        bbs.append(jnp.concatenate([bbr, bbi], axis=0))
    bpow = jnp.concatenate(blocks, axis=0).astype(BF16)

    qr, qi, _, _ = _s5_discretize(prow_ref[0:1], prow_ref[1:2], prow_ref[2:3])
    pows, q = [], (qr, qi)
    for bit in bits:
        pows.append((bit, q))
        q = _csq(*q)
    q_t = q
    row_t = lax.broadcasted_iota(jnp.int32, (th, 1), 0) // S5_H
    is_f = lax.broadcasted_iota(jnp.int32, (1, 4 * n), 1) < 2 * n
    e_k = jnp.where(is_f, row_t, (t_len - row_t) % t_len)
    kr, ki = _cpow_bits(e_k, pows)
    nr = jnp.where(is_f, kr * qr - ki * qi, jnp.where(row_t == 0, q_t[0], kr))
    ni = jnp.where(is_f, kr * qi + ki * qr, jnp.where(row_t == 0, q_t[1], ki))
    ct1 = jnp.concatenate([ct1_ref[...]] * t_len, axis=0)
    ct2 = jnp.concatenate([ct2_ref[...]] * t_len, axis=0)
    cpow = (ct1 * nr + ct2 * ni).astype(BF16)
    qk = ct1 * kr + ct2 * ki
    mf = jnp.dot(qk[:, :2 * n], bbs[0], precision=hp, preferred_element_type=F32)
    mr = jnp.dot(qk[:, 2 * n:], bbs[1], precision=hp, preferred_element_type=F32)
    m = (jnp.where(row_t <= t_len - 1 - lane_t, mf, 0.0)
         + jnp.where((row_t == 0) | (row_t >= t_len - lane_t), mr, 0.0))
    for bit in bits:
        m = jnp.where((lane_t & bit) != 0, pltpu.roll(m, S5_H * bit, 0), m)
    row = lax.broadcasted_iota(jnp.int32, (th, th), 0)
    col = lax.broadcasted_iota(jnp.int32, (th, th), 1)
    toep = (m + jnp.where(row == col, dsk_ref[...], 0.0)).astype(BF16)
    return toep, bpow, cpow, decay


def _s5_core_kernel(xt_ref, xc_ref, pcol_ref, prow_ref, bt_ref, ct1_ref, ct2_ref, dsk_ref, o_ref, s_ref, *,
                    nseg, nctx):
    n = S5_N
    ns = xc_ref.shape[-1]
    nj = xt_ref.shape[-1] // ns
    toep, bpow, cpow, decay = _s5_operators(pcol_ref, prow_ref, bt_ref, ct1_ref, ct2_ref, dsk_ref)
    dfr = jnp.broadcast_to(decay[0][0], (n, ns))
    dfi = jnp.broadcast_to(decay[0][1], (n, ns))
    drr = jnp.broadcast_to(decay[1][0], (n, ns))
    dri = jnp.broadcast_to(decay[1][1], (n, ns))
    slot = lax.broadcasted_iota(jnp.int32, (n, ns), 1) % nseg

    sc = jnp.dot(bpow, xc_ref[...], preferred_element_type=F32)
    valid = slot < nctx
    cfr, cfi = _lane_scan(jnp.where(valid, sc[0:n], 0.0), jnp.where(valid, sc[n:2 * n], 0.0),
                          dfr, dfi, slot, nseg, False)
    crr, cri = _lane_scan(jnp.where(valid, sc[2 * n:3 * n], 0.0), jnp.where(valid, sc[3 * n:4 * n], 0.0),
                          drr, dri, slot, nseg, True)

    step = 2 if nj % 2 == 0 else 1
    for j in range(0, nj, step):
        cols = slice(j * ns, (j + step) * ns)
        s_ref[:, cols] = jnp.dot(bpow, xt_ref[:, cols], preferred_element_type=F32)

    zero = jnp.zeros((n, ns), F32)
    efr, efi, err, eri = zero, zero, zero, zero
    for j in range(nj):
        cols = slice(j * ns, (j + 1) * ns)
        efr, efi = _cmul_add(dfr, dfi, efr, efi, s_ref[0:n, cols], s_ref[n:2 * n, cols])
    for j in range(nj - 1, -1, -1):
        cols = slice(j * ns, (j + 1) * ns)
        err, eri = _cmul_add(drr, dri, err, eri, s_ref[2 * n:3 * n, cols], s_ref[3 * n:4 * n, cols])

    sfr, sfi, srr, sri = dfr, dfi, drr, dri
    k = 1
    while k < nj:
        sfr, sfi = sfr * sfr - sfi * sfi, 2.0 * sfr * sfi
        srr, sri = srr * srr - sri * sri, 2.0 * srr * sri
        k *= 2

    first = slot == 0
    last = slot == nseg - 1
    vfr = jnp.where(first, pltpu.roll(cfr, ns - (nctx - 1), 1) if nctx > 1 else cfr, pltpu.roll(efr, 1, 1))
    vfi = jnp.where(first, pltpu.roll(cfi, ns - (nctx - 1), 1) if nctx > 1 else cfi, pltpu.roll(efi, 1, 1))
    vrr = jnp.where(last, pltpu.roll(crr, nseg - 1, 1) if nseg > 1 else crr, pltpu.roll(err, ns - 1, 1))
    vri = jnp.where(last, pltpu.roll(cri, nseg - 1, 1) if nseg > 1 else cri, pltpu.roll(eri, ns - 1, 1))
    hfr, hfi = _lane_scan(vfr, vfi, sfr, sfi, slot, nseg, False)
    hrr, hri = _lane_scan(vrr, vri, srr, sri, slot, nseg, True)

    for j in range(nj):
        cols = slice(j * ns, (j + 1) * ns)
        nr, ni = _cmul_add(dfr, dfi, hfr, hfi, s_ref[0:n, cols], s_ref[n:2 * n, cols])
        s_ref[0:n, cols] = hfr
        s_ref[n:2 * n, cols] = hfi
        hfr, hfi = nr, ni
    for j in range(nj - 1, -1, -1):
        cols = slice(j * ns, (j + 1) * ns)
        nr, ni = _cmul_add(drr, dri, hrr, hri, s_ref[2 * n:3 * n, cols], s_ref[3 * n:4 * n, cols])
        s_ref[2 * n:3 * n, cols] = hrr
        s_ref[3 * n:4 * n, cols] = hri
        hrr, hri = nr, ni

    for j in range(0, nj, step):
        cols = slice(j * ns, (j + step) * ns)
        y = (jnp.dot(toep, xt_ref[:, cols], preferred_element_type=F32)
             + jnp.dot(cpow, s_ref[:, cols].astype(BF16), preferred_element_type=F32))
        for q in range(step):
            o_ref[j + q] = y[:, q * ns:(q + 1) * ns].reshape(S5_T, S5_H, ns)


def _s5_core(xt, xct, params, nseg, nctx):
    groups, rows, lanes = xt.shape
    ns = xct.shape[-1]
    nj = lanes // ns
    kern = functools.partial(_s5_core_kernel, nseg=nseg, nctx=nctx)
    per_group = lambda a: pl.BlockSpec((None,) + a.shape[1:], lambda g, nd=a.ndim: (g,) + (0,) * (nd - 1))
    return pl.pallas_call(
        kern,
        grid=(groups,),
        in_specs=[per_group(xt), per_group(xct)] + [per_group(a) for a in params],
        out_specs=pl.BlockSpec((nj, S5_T, None, S5_H, ns), lambda g: (0, 0, g, 0, 0)),
        out_shape=jax.ShapeDtypeStruct((nj, S5_T, groups, S5_H, ns), F32),
        scratch_shapes=[pltpu.VMEM((rows, lanes), F32)],
        compiler_params=_cparams(("arbitrary",)),
        name="s5_core",
    )(xt, xct, *params)


def _s5_param_layout(a_re, a_im, log_step, b_re, b_im, c_re, c_im, d_skip):
    ndir, groups, n = a_re.shape
    h = b_re.shape[-1]
    ls = jnp.broadcast_to(log_step[..., None], a_re.shape)
    pcol = jnp.stack([a_re[0], a_im[0], ls[0], a_re[1], a_im[1], ls[1]], axis=1)[..., None]
    lay = lambda v: jnp.concatenate([v[0], v[0], v[1], v[1]], axis=-1)
    prow = jnp.stack([lay(a_re), lay(a_im), lay(ls)], axis=1)
    tile_t = lambda v: jnp.tile(v, (1, 1, S5_T))
    bt = jnp.concatenate([tile_t(b_re[0]), tile_t(b_im[0]), tile_t(b_re[1]), tile_t(b_im[1])], axis=1)
    ct1 = jnp.concatenate([c_re[0], -c_im[0], c_re[1], -c_im[1]], axis=-1)
    ct2 = jnp.concatenate([-c_im[0], -c_re[0], -c_im[1], -c_re[1]], axis=-1)
    dsk = jnp.tile(d_skip.reshape(groups, 1, h), (1, 1, S5_T))
    return tuple(v.astype(F32) for v in (pcol, prow, bt, ct1, ct2, dsk))


def _s5_post_kernel(y_ref, x_ref, gs_ref, sh_ref, wg_ref, wglu_ref, bglu_ref, o_ref):
    t_blk, e, ns = y_ref.shape
    d = gs_ref.shape[1]
    gs = gs_ref[...]
    sh = sh_ref[...]
    ys = jnp.concatenate([y_ref[t].T for t in range(t_blk)], axis=0)
    xs = jnp.concatenate([_norm_mod(x_ref[:, t * d:(t + 1) * d], gs, sh) for t in range(t_blk)], axis=0)
    hg = jnp.dot(xs.astype(BF16), wg_ref[...], preferred_element_type=F32)
    yg = _gelu_tanh(ys)
    zh = jnp.dot(yg.astype(BF16), wglu_ref[...], preferred_element_type=F32) + bglu_ref[...]
    m = ((yg * (0.5 * jnp.tanh(zh) + 0.5)) * (hg * jnp.tanh(hg) + hg)).astype(BF16)
    for t in range(t_blk):
        o_ref[:, t * e:(t + 1) * e] = m[t * ns:(t + 1) * ns]


def _s5_post(y4, x2, gs_rows, sh_rows, wg, wglu, bglu, t_blk):
    nj, t_all, e, ns = y4.shape
    d = gs_rows.shape[1]
    seg = x2.shape[1] // d
    nt = t_all // t_blk
    return pl.pallas_call(
        _s5_post_kernel,
        grid=(nj, nt),
        in_specs=[pl.BlockSpec((None, t_blk, e, ns), lambda j, q: (j, q, 0, 0)),
                  pl.BlockSpec((ns, t_blk * d), lambda j, q: (0, j * nt + q)),
                  pl.BlockSpec((ns, d), lambda j, q: (0, 0)),
                  pl.BlockSpec((ns, d), lambda j, q: (0, 0)),
                  pl.BlockSpec((d, e), lambda j, q: (0, 0)),
                  pl.BlockSpec((e, e), lambda j, q: (0, 0)),
                  pl.BlockSpec((1, e), lambda j, q: (0, 0))],
        out_specs=pl.BlockSpec((ns, t_blk * e), lambda j, q: (0, j * nt + q)),
        out_shape=jax.ShapeDtypeStruct((ns, seg * e), BF16),
        compiler_params=_cparams(("arbitrary", "arbitrary")),
        name="s5_post",
    )(y4, x2, gs_rows, sh_rows, wg, wglu, bglu)


def _softplus(z):
    return jnp.maximum(z, 0.0) + jnp.log1p(jnp.exp(-jnp.abs(z)))


def _gate_matmul(xc, wg_ref):
    xb = xc.astype(BF16)
    nq = wg_ref.shape[0]
    return [jnp.dot(xb[:, q * GATE_TILE:(q + 1) * GATE_TILE], wg_ref[q], preferred_element_type=F32)
            for q in range(nq)]


def _gate_cols(zs, k):
    return jnp.concatenate([z[:, k * GATE_TILE:(k + 1) * GATE_TILE] for z in zs], axis=1)


def _lru_decay_scale(lam_row):
    return (-0.5 * LRU_C * math.log2(math.e)) * _softplus(-lam_row)


def _lru_coeffs(hx, zs, k, gbh, c2):
    ta = jnp.tanh(_gate_cols(zs, 2 * k) + gbh[2 * k:2 * k + 1])
    tx = jnp.tanh(_gate_cols(zs, 2 * k + 1) + gbh[2 * k + 1:2 * k + 2])
    a = jnp.exp2(ta * c2 + c2)
    om = 1.0 - a * a
    return a, (om * lax.rsqrt(jnp.maximum(om, TINY))) * (tx * hx + hx)


def _lru_ctx_kernel(x_ref, gs_ref, sh_ref, wu_ref, cw_ref, cb_ref, wg_ref, gb_ref, lam_ref,
                    ff_ref, fr_ref, af_s, bf_s, ar_s, br_s):
    rows = x_ref.shape[0]
    npos = rows // 8
    xn = _norm_mod(x_ref[...], gs_ref[...], sh_ref[...]).astype(BF16)
    u = jnp.dot(xn, wu_ref[...], preferred_element_type=F32)
    pos = lax.broadcasted_iota(jnp.int32, u.shape, 0) // 8
    um1 = jnp.where(pos >= 1, pltpu.roll(u, 8, 0), 0.0)
    up1 = jnp.where(pos < npos - 1, pltpu.roll(u, rows - 8, 0), 0.0)
    up2 = jnp.where(pos < npos - 2, pltpu.roll(u, rows - 16, 0), 0.0)
    cw = cw_ref[...]
    xc = cw[0:1] * um1 + cw[1:2] * u + cw[2:3] * up1 + cw[3:4] * up2 + cb_ref[...]
    zs = _gate_matmul(xc, wg_ref)
    gbh = gb_ref[...]
    c2 = _lru_decay_scale(lam_ref[...])
    hx = 0.5 * xc
    a, b = _lru_coeffs(hx, zs, 0, gbh, c2[0:1])
    af_s[...] = a
    bf_s[...] = b
    a, b = _lru_coeffs(hx, zs, 1, gbh, c2[1:2])
    ar_s[...] = a
    br_s[...] = b

    def body(p, carry):
        hf, hr, pr = carry
        r0 = pl.multiple_of(p * 8, 8)
        hf = af_s[pl.ds(r0, 8), :] * hf + bf_s[pl.ds(r0, 8), :]
        hr = hr + pr * br_s[pl.ds(r0, 8), :]
        pr = pr * ar_s[pl.ds(r0, 8), :]
        return hf, hr, pr

    zero = jnp.zeros((8, u.shape[1]), F32)
    hf, hr, _ = lax.fori_loop(0, npos, body, (zero, zero, zero + 1.0))
    ff_ref[...] = hf
    fr_ref[...] = hr


def _lru_ctx(xp, gs, sh, wu, cw, cb, wg, gb, lam):
    rows, d = xp.shape
    e = wu.shape[1]
    nq = e // GATE_TILE
    full = lambda a: pl.BlockSpec(a.shape, lambda q, nd=a.ndim: (0,) * nd)
    cols = lambda a: pl.BlockSpec((a.shape[0], GATE_TILE), lambda q: (0, q))
    return pl.pallas_call(
        _lru_ctx_kernel,
        grid=(nq,),
        in_specs=[full(xp), full(gs), full(sh), cols(wu), cols(cw), cols(cb),
                  pl.BlockSpec((1,) + wg.shape[1:], lambda q: (q, 0, 0)), cols(gb), cols(lam)],
        out_specs=[pl.BlockSpec((8, GATE_TILE), lambda q: (0, q)), pl.BlockSpec((8, GATE_TILE), lambda q: (0, q))],
        out_shape=[jax.ShapeDtypeStruct((8, e), F32), jax.ShapeDtypeStruct((8, e), F32)],
        scratch_shapes=[pltpu.VMEM((rows, GATE_TILE), F32) for _ in range(4)],
        compiler_params=_cparams(("arbitrary",)),
    )(xp, gs, sh, wu, cw, cb, wg, gb, lam)


def _lru_pass_a_kernel(x_ref, gs_ref, sh_ref, wu_ref, cw_ref, cb_ref, wg_ref, gb_ref, lam_ref,
                       xc_ref, hl_ref, pc_ref, pr_ref, hr_ref, ring_ref, hs_ref, ps_ref, *, nrow):
    i = pl.program_id(0)
    nb, w, d = x_ref.shape
    rows = nb * w
    xn = _norm_mod(x_ref[...].reshape(rows, d), gs_ref[...], sh_ref[...]).astype(BF16)
    u = jnp.dot(xn, wu_ref[...], preferred_element_type=F32)
    ring_ref[i & 3] = u

    @pl.when(i == 0)
    def _():
        col = lax.broadcasted_iota(jnp.int32, u.shape, 0) % w
        ring_ref[0] = jnp.where(col == 0, 0.0, pltpu.roll(u, 1, 0))

    @pl.when(i > nrow)
    def _():
        col = lax.broadcasted_iota(jnp.int32, u.shape, 0) % w
        ring_ref[i & 3] = jnp.where(col == w - 1, 0.0, pltpu.roll(u, rows - 1, 0))

    @pl.when(i >= 3)
    def _():
        cw = cw_ref[...]
        xc = (cw[0:1] * ring_ref[(i - 3) & 3] + cw[1:2] * ring_ref[(i - 2) & 3]
              + cw[2:3] * ring_ref[(i - 1) & 3] + cw[3:4] * ring_ref[i & 3] + cb_ref[...])
        xc_ref[...] = xc
        zs = _gate_matmul(xc, wg_ref)
        gbh = gb_ref[...]
        c2 = _lru_decay_scale(lam_ref[...])
        hx = 0.5 * xc
        af, bf = _lru_coeffs(hx, zs, 0, gbh, c2[0:1])
        ar, br = _lru_coeffs(hx, zs, 1, gbh, c2[1:2])

        @pl.when(i == 3)
        def _():
            hl_ref[...] = bf
            pc_ref[...] = af
            hs_ref[...] = bf
            ps_ref[...] = af
            pr_ref[...] = ar
            hr_ref[...] = br

        @pl.when(i > 3)
        def _():
            h = af * hs_ref[...] + bf
            p = ps_ref[...] * af
            hl_ref[...] = h
            pc_ref[...] = p
            hs_ref[...] = h
            ps_ref[...] = p
            hr_ref[...] = hr_ref[...] + pr_ref[...] * br
            pr_ref[...] = pr_ref[...] * ar


def _lru_pass_a(x4, gs_rows, sh_rows, wu, cw, cb, wg, gb, lam):
    nb, nrow, w, d = x4.shape
    e = wu.shape[1]
    rows = nb * w
    kern = functools.partial(_lru_pass_a_kernel, nrow=nrow)
    const = lambda a: pl.BlockSpec(a.shape, lambda i, nd=a.ndim: (0,) * nd)
    acc = pl.BlockSpec((rows, e), lambda i: (0, 0))
    acc_shape = jax.ShapeDtypeStruct((rows, e), F32)
    per_row = pl.BlockSpec((None, rows, e), lambda i: (jnp.maximum(i - 3, 0), 0, 0))
    per_row_shape = jax.ShapeDtypeStruct((nrow, rows, e), F32)
    return pl.pallas_call(
        kern,
        grid=(nrow + 3,),
        in_specs=[pl.BlockSpec((nb, None, w, d), lambda i: (0, (i + nrow - 1) % nrow, 0, 0)),
                  const(gs_rows), const(sh_rows), const(wu), const(cw), const(cb), const(wg), const(gb),
                  const(lam)],
        out_specs=[per_row, per_row, per_row, acc, acc],
        out_shape=[per_row_shape, per_row_shape, per_row_shape, acc_shape, acc_shape],
        scratch_shapes=[pltpu.VMEM((4, rows, e), F32), pltpu.VMEM((rows, e), F32), pltpu.VMEM((rows, e), F32)],
        compiler_params=_cparams(("arbitrary",)),
        name="lru_pass_a",
    )(x4, gs_rows, sh_rows, wu, cw, cb, wg, gb, lam)


def _lru_stitch_kernel(pf_ref, hf_ref, pr_ref, hr_ref, ff_ref, fr_ref, sf_ref, sr_ref, *, w):
    rows = pf_ref.shape[0]
    col = lax.broadcasted_iota(jnp.int32, pf_ref.shape, 0) % w
    p, h = pf_ref[...], hf_ref[...]
    k = 1
    while k < w:
        keep = col >= k
        ps = jnp.where(keep, pltpu.roll(p, k, 0), 1.0)
        hs = jnp.where(keep, pltpu.roll(h, k, 0), 0.0)
        p, h = p * ps, p * hs + h
        k *= 2
    fin = ff_ref[...]
    sf_ref[...] = jnp.where(col == 0, fin, pltpu.roll(p, 1, 0) * fin + pltpu.roll(h, 1, 0))
    p, h = pr_ref[...], hr_ref[...]
    k = 1
    while k < w:
        keep = col < w - k
        ps = jnp.where(keep, pltpu.roll(p, rows - k, 0), 1.0)
        hs = jnp.where(keep, pltpu.roll(h, rows - k, 0), 0.0)
        p, h = p * ps, p * hs + h
        k *= 2
    fin = fr_ref[...]
    sr_ref[...] = jnp.where(col == w - 1, fin, pltpu.roll(p, rows - 1, 0) * fin + pltpu.roll(h, rows - 1, 0))


def _lru_stitch(pcum, hloc, pr, hr, ff_rows, fr_rows, w):
    nrow, rows, e = pcum.shape
    big = pl.BlockSpec((rows, e), lambda i: (0, 0))
    last = pl.BlockSpec((None, rows, e), lambda i: (nrow - 1, 0, 0))
    shp = jax.ShapeDtypeStruct((rows, e), F32)
    return pl.pallas_call(
        functools.partial(_lru_stitch_kernel, w=w),
        grid=(1,),
        in_specs=[last, last, big, big, big, big],
        out_specs=[big, big],
        out_shape=[shp, shp],
        compiler_params=_cparams(("arbitrary",)),
        name="lru_stitch",
    )(pcum, hloc, pr, hr, ff_rows, fr_rows)


def _lru_rev_out_kernel(xc_ref, hl_ref, pc_ref, x_ref, m5_ref, gs_ref, sh_ref, gt_ref, wg_ref, gb_ref, lam_ref,
                        hsf_ref, hsr_ref, wgl_ref, wo_ref, fg_ref, o_ref, h_ref):
    i = pl.program_id(0)
    nb, w, d = x_ref.shape
    rows = nb * w

    @pl.when(i == 0)
    def _():
        h_ref[...] = hsr_ref[...]

    xc = xc_ref[...]
    zs = _gate_matmul(xc, wg_ref)
    a, b = _lru_coeffs(0.5 * xc, zs, 0, gb_ref[...], _lru_decay_scale(lam_ref[...]))
    h = a * h_ref[...] + b
    h_ref[...] = h
    y = hl_ref[...] + pc_ref[...] * hsf_ref[...] + h

    x = x_ref[...].reshape(rows, d)
    xn = _norm_mod(x, gs_ref[...], sh_ref[...]).astype(BF16)
    hg = jnp.dot(xn, wgl_ref[...], preferred_element_type=F32)
    m = y * (hg * jnp.tanh(hg) + hg)
    e = m.shape[1]
    mix = (jnp.dot(m5_ref[...].reshape(rows, e), wo_ref[0:e, :], preferred_element_type=F32)
           + jnp.dot(m.astype(BF16), wo_ref[e:2 * e, :], preferred_element_type=F32))
    res = x + gt_ref[...] * mix
    ms = jnp.mean(res * res, axis=-1, keepdims=True)
    o_ref[...] = (res * lax.rsqrt(ms + EPS) * fg_ref[...]).reshape(nb, w, d)


def _lru_rev_out(xc, hloc, pcum, x4, m5, gs_rows, sh_rows, gt_rows, wg, gb, lam, hs_f, hs_r, wgl, wo, fg):
    nb, nrow, w, d = x4.shape
    _, rows, e = xc.shape
    const = lambda a: pl.BlockSpec(a.shape, lambda i, nd=a.ndim: (0,) * nd)
    rev3 = pl.BlockSpec((None, rows, e), lambda i: (nrow - 1 - i, 0, 0))
    rev4 = pl.BlockSpec((nb, None, w, d), lambda i: (0, nrow - 1 - i, 0, 0))
    rev4e = pl.BlockSpec((nb, None, w, e), lambda i: (0, nrow - 1 - i, 0, 0))
    return pl.pallas_call(
        _lru_rev_out_kernel,
        grid=(nrow,),
        in_specs=[rev3, rev3, rev3, rev4, rev4e, const(gs_rows), const(sh_rows), const(gt_rows), const(wg),
                  const(gb), const(lam), const(hs_f), const(hs_r), const(wgl), const(wo), const(fg)],
        out_specs=rev4,
        out_shape=jax.ShapeDtypeStruct((nb, nrow, w, d), F32),
        scratch_shapes=[pltpu.VMEM((rows, e), F32)],
        compiler_params=_cparams(("arbitrary",)),
        name="lru_rev_out",
    )(xc, hloc, pcum, x4, m5, gs_rows, sh_rows, gt_rows, wg, gb, lam, hs_f, hs_r, wgl, wo, fg)


def _gate_tiles(w_a, w_x):
    ndir, heads, hd, _ = w_a.shape
    per = GATE_TILE // hd
    nq = heads // per

    def tiles(w):
        wq = w.reshape(nq, per, hd, hd)
        eye = jnp.eye(per, dtype=w.dtype)
        blk = wq[:, :, :, None, :] * eye[None, :, None, :, None]
        return blk.reshape(nq, GATE_TILE, GATE_TILE)

    return jnp.concatenate([tiles(w_a[0]), tiles(w_x[0]), tiles(w_a[1]), tiles(w_x[1])], axis=2).astype(BF16)


def kernel(x, c, ctx, c_ctx, w_mod, b_mod, norm_g, w_in, s5_a_re, s5_a_im, s5_log_step, s5_b_re, s5_b_im,
           s5_c_re, s5_c_im, s5_d, s5_w_glu, s5_b_glu, lru_conv_w, lru_conv_b, lru_w_a, lru_b_a, lru_w_x,
           lru_b_x, lru_lam, w_out, final_g):
    bsz, seq, d = x.shape
    nctx_tok = ctx.shape[1]
    e = s5_w_glu.shape[-1]
    assert w_mod.shape[0] == 1, "single-layer block"
    assert bsz <= 8 and seq % S5_SEG == 0 and seq % GRID_W == 0 and nctx_tok % S5_T == 0
    nseg = seq // S5_SEG
    ns = bsz * nseg
    nctx = nctx_tok // S5_T
    assert nctx <= nseg
    nrow = seq // GRID_W

    c8 = jnp.zeros((8, d), F32).at[:bsz].set(c).at[bsz].set(c_ctx)
    mod = _modulation(c8, w_mod[0], b_mod[0])
    sh, sc, gt = mod[:, :d], mod[:, d:2 * d], mod[:, 2 * d:]
    gs = norm_g[0][None, :] * (1.0 + sc)

    w_in0 = w_in[0]
    wt_us5 = w_in0[:, :e].T.astype(BF16)
    w_gs5 = (0.5 * w_in0[:, e:2 * e]).astype(BF16)
    w_ulru = w_in0[:, 2 * e:3 * e].astype(BF16)
    w_glru = (0.5 * w_in0[:, 3 * e:]).astype(BF16)
    w_glu = (0.5 * s5_w_glu[0]).astype(BF16)
    wo = w_out[0].astype(BF16)

    s5_params = _s5_param_layout(s5_a_re[0], s5_a_im[0], s5_log_step[0], s5_b_re[0], s5_b_im[0],
                                 s5_c_re[0], s5_c_im[0], s5_d[0])
    x2 = x.reshape(ns, S5_SEG * d)
    gs_seg = jnp.repeat(gs[:bsz], nseg, axis=0)
    sh_seg = jnp.repeat(sh[:bsz], nseg, axis=0)
    xt = _s5_inproj(x2, gs_seg, sh_seg, wt_us5)
    ctx_pad = jnp.zeros((bsz, nseg, S5_T * d), F32).at[:, :nctx].set(ctx.reshape(bsz, nctx, S5_T * d))
    gs_c = jnp.broadcast_to(gs[bsz:bsz + 1], (ns, d))
    sh_c = jnp.broadcast_to(sh[bsz:bsz + 1], (ns, d))
    xct = _s5_inproj(ctx_pad.reshape(ns, S5_T * d), gs_c, sh_c, wt_us5)
    y5 = _s5_core(xt, xct, s5_params, nseg, nctx)
    nj = S5_SEG // S5_T
    m5 = _s5_post(y5.reshape(nj, S5_T, e, ns), x2, gs_seg, sh_seg, w_gs5, w_glu,
                  0.5 * s5_b_glu[0].reshape(1, e), t_blk=8)

    wg_all = _gate_tiles(0.5 * lru_w_a[0], 0.5 * lru_w_x[0])
    gb_all = 0.5 * jnp.stack([lru_b_a[0, 0], lru_b_x[0, 0], lru_b_a[0, 1], lru_b_x[0, 1]])
    cw = lru_conv_w[0]
    cb = lru_conv_b[0].reshape(1, e)
    lam = lru_lam[0]
    ctx_p = jnp.zeros((8, nctx_tok, d), F32).at[:bsz].set(ctx).transpose(1, 0, 2).reshape(nctx_tok * 8, d)
    ff, fr = _lru_ctx(ctx_p, gs[bsz:bsz + 1], sh[bsz:bsz + 1], w_ulru, cw, cb, wg_all, gb_all, lam)
    x4 = x.reshape(bsz, nrow, GRID_W, d)
    gs_col = jnp.repeat(gs[:bsz], GRID_W, axis=0)
    sh_col = jnp.repeat(sh[:bsz], GRID_W, axis=0)
    gt_col = jnp.repeat(gt[:bsz], GRID_W, axis=0)
    xc, hloc, pcum, pr, hr = _lru_pass_a(x4, gs_col, sh_col, w_ulru, cw, cb, wg_all, gb_all, lam)
    hs_f, hs_r = _lru_stitch(pcum, hloc, pr, hr, jnp.repeat(ff[:bsz], GRID_W, axis=0),
                             jnp.repeat(fr[:bsz], GRID_W, axis=0), GRID_W)
    out4 = _lru_rev_out(xc, hloc, pcum, x4, m5.reshape(bsz, nrow, GRID_W, e), gs_col, sh_col, gt_col,
                        wg_all[:, :, 2 * GATE_TILE:], gb_all[2:4], lam[1:2], hs_f, hs_r, w_glru, wo,
                        final_g.reshape(1, d))
    return out4.reshape(bsz, seq, d)
```

```python
import functools
import math

import jax
import jax.numpy as jnp
from jax import lax
from jax.experimental import pallas as pl
from jax.experimental.pallas import tpu as pltpu

F32 = jnp.float32
BF16 = jnp.bfloat16

EPS = 1e-6
TINY = 1e-30
GRID_W = 64
LRU_C = 8.0
S5_H = 16
S5_N = 64
S5_T = 16
S5_SEG = 256
S5_TB = 8
S5_PB = 16
LRU_HEAD = 64
GATE_TILE = 256
LRU_RC = 32
VMEM_LIMIT = 56 * 1024 * 1024


def _cparams(sem):
    return pltpu.CompilerParams(dimension_semantics=sem, vmem_limit_bytes=VMEM_LIMIT)


def _sigmoid(z):
    return 0.5 * (jnp.tanh(0.5 * z) + 1.0)


def _gelu_tanh(y):
    return 0.5 * y * (1.0 + jnp.tanh(math.sqrt(2.0 / math.pi) * (y + 0.044715 * (y * y * y))))


def _norm_mod(x, gs, sh):
    ms = jnp.mean(x * x, axis=-1, keepdims=True)
    return x * lax.rsqrt(ms + EPS) * gs + sh


def _mod_kernel(c_ref, w_ref, b_ref, o_ref):
    c = c_ref[...]
    s = c * _sigmoid(c)
    o_ref[...] = jnp.dot(s, w_ref[...], preferred_element_type=F32) + b_ref[...]


def _modulation(c8, w_mod, b_mod):
    d = c8.shape[1]
    n = w_mod.shape[1]
    nb = n // d
    return pl.pallas_call(
        _mod_kernel,
        grid=(nb,),
        in_specs=[pl.BlockSpec((8, d), lambda i: (0, 0)),
                  pl.BlockSpec((d, d), lambda i: (0, i)),
                  pl.BlockSpec((1, d), lambda i: (0, i))],
        out_specs=pl.BlockSpec((8, d), lambda i: (0, i)),
        out_shape=jax.ShapeDtypeStruct((8, n), F32),
        compiler_params=_cparams(("arbitrary",)),
    )(c8, w_mod, b_mod.reshape(1, n))


def _perm_matrix(n_a, n_b):
    n = n_a * n_b
    r = lax.broadcasted_iota(jnp.int32, (n, n), 0)
    c = lax.broadcasted_iota(jnp.int32, (n, n), 1)
    return ((r // n_b == c % n_a) & (r % n_b == c // n_a)).astype(BF16)


def _s5_inproj_kernel(x_ref, gs_ref, sh_ref, wt_ref, wg_ref, o_ref, hg_ref, xs_ref, *, nbatch):
    ns, tb, d = x_ref.shape
    groups = o_ref.shape[0]
    rows_b = (ns // nbatch) * tb
    blk = S5_PB * tb
    perm = _perm_matrix(tb, S5_PB)
    x = x_ref[...].reshape(ns * tb, d)
    xn = jnp.concatenate(
        [_norm_mod(x[b * rows_b:(b + 1) * rows_b], gs_ref[b:b + 1], sh_ref[b:b + 1]).astype(BF16)
         for b in range(nbatch)], axis=0)
    for k in range(ns // S5_PB):
        q = jnp.dot(perm, xn[k * blk:(k + 1) * blk], preferred_element_type=F32).astype(BF16)
        for t in range(tb):
            xs_ref[t * ns + k * S5_PB:t * ns + (k + 1) * S5_PB, :] = q[t * S5_PB:(t + 1) * S5_PB]
    wt = wt_ref[...]
    for t in range(0, tb, 2):
        ut = lax.dot_general(wt, xs_ref[t * ns:(t + 2) * ns, :], (((1,), (1,)), ((), ())),
                             preferred_element_type=F32)
        o_ref[:, t * S5_H:(t + 1) * S5_H, :] = ut[:, :ns].reshape(groups, S5_H, ns).astype(BF16)
        o_ref[:, (t + 1) * S5_H:(t + 2) * S5_H, :] = ut[:, ns:].reshape(groups, S5_H, ns).astype(BF16)
    hg = jnp.dot(xs_ref[...], wg_ref[...], preferred_element_type=F32)
    hg_ref[...] = hg.astype(BF16).reshape(tb, ns, hg.shape[1])


def _s5_inproj(x3, gs8, sh8, wt, wg, nbatch):
    ns, seg, d = x3.shape
    e = wt.shape[0]
    groups = e // S5_H
    nj = seg // S5_T
    per = S5_T // S5_TB
    return pl.pallas_call(
        functools.partial(_s5_inproj_kernel, nbatch=nbatch),
        grid=(seg // S5_TB,),
        in_specs=[pl.BlockSpec((ns, S5_TB, d), lambda i: (0, i, 0)),
                  pl.BlockSpec((8, d), lambda i: (0, 0)),
                  pl.BlockSpec((8, d), lambda i: (0, 0)),
                  pl.BlockSpec((e, d), lambda i: (0, 0)),
                  pl.BlockSpec((d, e), lambda i: (0, 0))],
        out_specs=[pl.BlockSpec((groups, S5_TB * S5_H, ns), lambda i: (0, i % per, i // per)),
                   pl.BlockSpec((S5_TB, ns, e), lambda i: (i, 0, 0))],
        out_shape=[jax.ShapeDtypeStruct((groups, S5_T * S5_H, nj * ns), BF16),
                   jax.ShapeDtypeStruct((seg, ns, e), BF16)],
        scratch_shapes=[pltpu.VMEM((S5_TB * ns, d), BF16)],
        compiler_params=_cparams(("arbitrary",)),
        name="s5_inproj",
    )(x3, gs8, sh8, wt, wg)


def _cmul_add(dr, di, hr, hi, sr, si):
    return dr * hr - di * hi + sr, dr * hi + di * hr + si


def _lane_scan(vr, vi, dr, di, slot, count, reverse):
    n = vr.shape[1]
    k = 1
    while k < count:
        if reverse:
            sr = pltpu.roll(vr, n - k, 1)
            si = pltpu.roll(vi, n - k, 1)
            keep = slot < count - k
        else:
            sr = pltpu.roll(vr, k, 1)
            si = pltpu.roll(vi, k, 1)
            keep = slot >= k
        sr = jnp.where(keep, sr, 0.0)
        si = jnp.where(keep, si, 0.0)
        vr, vi = _cmul_add(dr, di, sr, si, vr, vi)
        dr, di = dr * dr - di * di, 2.0 * dr * di
        k *= 2
    return vr, vi


def _csq(r, i):
    return r * r - i * i, 2.0 * r * i


def _cpow_bits(e, pows):
    pr = jnp.ones(e.shape, F32)
    pi = jnp.zeros(e.shape, F32)
    for bit, (qr, qi) in pows:
        on = (e & bit) != 0
        pr, pi = jnp.where(on, pr * qr - pi * qi, pr), jnp.where(on, pr * qi + pi * qr, pi)
    return pr, pi


def _s5_discretize(ar, ai, ls):
    step = jnp.exp(ls)
    mag = jnp.exp(ar * step)
    pr = mag * jnp.cos(ai * step)
    pi = mag * jnp.sin(ai * step)
    den = ar * ar + ai * ai
    nr = pr - 1.0
    return pr, pi, (nr * ar + pi * ai) / den, (pi * ar - nr * ai) / den


def _s5_operators(pcol_ref, prow_ref, bt_ref, ct1_ref, ct2_ref, dsk_ref):
    n, t_len, th = S5_N, S5_T, S5_T * S5_H
    hp = lax.Precision.HIGHEST
    lane_t = lax.broadcasted_iota(jnp.int32, (1, th), 1) // S5_H
    bits = [1 << b for b in range(t_len.bit_length() - 1)]

    blocks, bbs, decay = [], [], []
    for d in range(2):
        pr, pi, cr, ci = _s5_discretize(pcol_ref[3 * d], pcol_ref[3 * d + 1], pcol_ref[3 * d + 2])
        pows, q = [], (pr, pi)
        for bit in bits:
            pows.append((bit, q))
            q = _csq(*q)
        decay.append(q)
        btr = bt_ref[2 * d * n:(2 * d + 1) * n, :]
        bti = bt_ref[(2 * d + 1) * n:(2 * d + 2) * n, :]
        bbr = cr * btr - ci * bti
        bbi = cr * bti + ci * btr
        e = (t_len - 1 - lane_t) if d == 0 else lane_t
        wr, wi = _cpow_bits(jnp.broadcast_to(e, (n, th)), pows)
        blocks += [wr * bbr - wi * bbi, wr * bbi + wi * bbr]
        bbs.append(jnp.concatenate([bbr, bbi], axis=0))
    bpow = jnp.concatenate(blocks, axis=0).astype(BF16)

    qr, qi, _, _ = _s5_discretize(prow_ref[0:1], prow_ref[1:2], prow_ref[2:3])
    pows, q = [], (qr, qi)
    for bit in bits:
        pows.append((bit, q))
        q = _csq(*q)
    q_t = q
    row_t = lax.broadcasted_iota(jnp.int32, (th, 1), 0) // S5_H
    is_f = lax.broadcasted_iota(jnp.int32, (1, 4 * n), 1) < 2 * n
    e_k = jnp.where(is_f, row_t, (t_len - row_t) % t_len)
    kr, ki = _cpow_bits(e_k, pows)
    nr = jnp.where(is_f, kr * qr - ki * qi, jnp.where(row_t == 0, q_t[0], kr))
    ni = jnp.where(is_f, kr * qi + ki * qr, jnp.where(row_t == 0, q_t[1], ki))
    ct1 = jnp.concatenate([ct1_ref[...]] * t_len, axis=0)
    ct2 = jnp.concatenate([ct2_ref[...]] * t_len, axis=0)
    cpow = (ct1 * nr + ct2 * ni).astype(BF16)
    qk = ct1 * kr + ct2 * ki
    mf = jnp.dot(qk[:, :2 * n], bbs[0], precision=hp, preferred_element_type=F32)
    mr = jnp.dot(qk[:, 2 * n:], bbs[1], precision=hp, preferred_element_type=F32)
    m = (jnp.where(row_t <= t_len - 1 - lane_t, mf, 0.0)
         + jnp.where((row_t == 0) | (row_t >= t_len - lane_t), mr, 0.0))
    for bit in bits:
        m = jnp.where((lane_t & bit) != 0, pltpu.roll(m, S5_H * bit, 0), m)
    row = lax.broadcasted_iota(jnp.int32, (th, th), 0)
    col = lax.broadcasted_iota(jnp.int32, (th, th), 1)
    toep = (m + jnp.where(row == col, dsk_ref[...], 0.0)).astype(BF16)
    return toep, bpow, cpow, decay


def _s5_core_kernel(xt_ref, xc_ref, pcol_ref, prow_ref, bt_ref, ct1_ref, ct2_ref, dsk_ref, o_ref, s_ref, *,
                    nseg, nctx):
    n = S5_N
    ns = xc_ref.shape[-1]
    nj = xt_ref.shape[-1] // ns
    toep, bpow, cpow, decay = _s5_operators(pcol_ref, prow_ref, bt_ref, ct1_ref, ct2_ref, dsk_ref)
    dfr = jnp.broadcast_to(decay[0][0], (n, ns))
    dfi = jnp.broadcast_to(decay[0][1], (n, ns))
    drr = jnp.broadcast_to(decay[1][0], (n, ns))
    dri = jnp.broadcast_to(decay[1][1], (n, ns))
    slot = lax.broadcasted_iota(jnp.int32, (n, ns), 1) % nseg

    sc = jnp.dot(bpow, xc_ref[...], preferred_element_type=F32)
    valid = slot < nctx
    cfr, cfi = _lane_scan(jnp.where(valid, sc[0:n], 0.0), jnp.where(valid, sc[n:2 * n], 0.0),
                          dfr, dfi, slot, nseg, False)
    crr, cri = _lane_scan(jnp.where(valid, sc[2 * n:3 * n], 0.0), jnp.where(valid, sc[3 * n:4 * n], 0.0),
                          drr, dri, slot, nseg, True)

    step = 2 if nj % 2 == 0 else 1
    for j in range(0, nj, step):
        cols = slice(j * ns, (j + step) * ns)
        s_ref[:, cols] = jnp.dot(bpow, xt_ref[:, cols], preferred_element_type=F32)

    zero = jnp.zeros((n, ns), F32)
    efr, efi, err, eri = zero, zero, zero, zero
    for j in range(nj):
        cols = slice(j * ns, (j + 1) * ns)
        efr, efi = _cmul_add(dfr, dfi, efr, efi, s_ref[0:n, cols], s_ref[n:2 * n, cols])
    for j in range(nj - 1, -1, -1):
        cols = slice(j * ns, (j + 1) * ns)
        err, eri = _cmul_add(drr, dri, err, eri, s_ref[2 * n:3 * n, cols], s_ref[3 * n:4 * n, cols])

    sfr, sfi, srr, sri = dfr, dfi, drr, dri
    k = 1
    while k < nj:
        sfr, sfi = sfr * sfr - sfi * sfi, 2.0 * sfr * sfi
        srr, sri = srr * srr - sri * sri, 2.0 * srr * sri
        k *= 2

    first = slot == 0
    last = slot == nseg - 1
    vfr = jnp.where(first, pltpu.roll(cfr, ns - (nctx - 1), 1) if nctx > 1 else cfr, pltpu.roll(efr, 1, 1))
    vfi = jnp.where(first, pltpu.roll(cfi, ns - (nctx - 1), 1) if nctx > 1 else cfi, pltpu.roll(efi, 1, 1))
    vrr = jnp.where(last, pltpu.roll(crr, nseg - 1, 1) if nseg > 1 else crr, pltpu.roll(err, ns - 1, 1))
    vri = jnp.where(last, pltpu.roll(cri, nseg - 1, 1) if nseg > 1 else cri, pltpu.roll(eri, ns - 1, 1))
    hfr, hfi = _lane_scan(vfr, vfi, sfr, sfi, slot, nseg, False)
    hrr, hri = _lane_scan(vrr, vri, srr, sri, slot, nseg, True)

    for j in range(nj):
        cols = slice(j * ns, (j + 1) * ns)
        nr, ni = _cmul_add(dfr, dfi, hfr, hfi, s_ref[0:n, cols], s_ref[n:2 * n, cols])
        s_ref[0:n, cols] = hfr
        s_ref[n:2 * n, cols] = hfi
        hfr, hfi = nr, ni
    for j in range(nj - 1, -1, -1):
        cols = slice(j * ns, (j + 1) * ns)
        nr, ni = _cmul_add(drr, dri, hrr, hri, s_ref[2 * n:3 * n, cols], s_ref[3 * n:4 * n, cols])
        s_ref[2 * n:3 * n, cols] = hrr
        s_ref[3 * n:4 * n, cols] = hri
        hrr, hri = nr, ni

    for j in range(0, nj, step):
        cols = slice(j * ns, (j + step) * ns)
        y = (jnp.dot(toep, xt_ref[:, cols], preferred_element_type=F32)
             + jnp.dot(cpow, s_ref[:, cols].astype(BF16), preferred_element_type=F32))
        for q in range(step):
            o_ref[j + q] = y[:, q * ns:(q + 1) * ns].reshape(S5_T, S5_H, ns)


def _s5_core(xt, xct, params, nseg, nctx):
    groups, rows, lanes = xt.shape
    ns = xct.shape[-1]
    nj = lanes // ns
    kern = functools.partial(_s5_core_kernel, nseg=nseg, nctx=nctx)
    per_group = lambda a: pl.BlockSpec((None,) + a.shape[1:], lambda g, nd=a.ndim: (g,) + (0,) * (nd - 1))
    return pl.pallas_call(
        kern,
        grid=(groups,),
        in_specs=[per_group(xt), per_group(xct)] + [per_group(a) for a in params],
        out_specs=pl.BlockSpec((nj, S5_T, None, S5_H, ns), lambda g: (0, 0, g, 0, 0)),
        out_shape=jax.ShapeDtypeStruct((nj, S5_T, groups, S5_H, ns), F32),
        scratch_shapes=[pltpu.VMEM((rows, lanes), F32)],
        compiler_params=_cparams(("arbitrary",)),
        name="s5_core",
    )(xt, xct, *params)


def _s5_param_layout(a_re, a_im, log_step, b_re, b_im, c_re, c_im, d_skip):
    ndir, groups, n = a_re.shape
    h = b_re.shape[-1]
    ls = jnp.broadcast_to(log_step[..., None], a_re.shape)
    pcol = jnp.stack([a_re[0], a_im[0], ls[0], a_re[1], a_im[1], ls[1]], axis=1)[..., None]
    lay = lambda v: jnp.concatenate([v[0], v[0], v[1], v[1]], axis=-1)
    prow = jnp.stack([lay(a_re), lay(a_im), lay(ls)], axis=1)
    tile_t = lambda v: jnp.tile(v, (1, 1, S5_T))
    bt = jnp.concatenate([tile_t(b_re[0]), tile_t(b_im[0]), tile_t(b_re[1]), tile_t(b_im[1])], axis=1)
    ct1 = jnp.concatenate([c_re[0], -c_im[0], c_re[1], -c_im[1]], axis=-1)
    ct2 = jnp.concatenate([-c_im[0], -c_re[0], -c_im[1], -c_re[1]], axis=-1)
    dsk = jnp.tile(d_skip.reshape(groups, 1, h), (1, 1, S5_T))
    return tuple(v.astype(F32) for v in (pcol, prow, bt, ct1, ct2, dsk))


def _s5_post_kernel(y_ref, hg_ref, wglu_ref, bglu_ref, o_ref, m_ref):
    q = pl.program_id(1)
    tb, e, ns = y_ref.shape
    ys = jnp.concatenate([y_ref[t].T for t in range(tb)], axis=0)
    yg = _gelu_tanh(ys)
    zh = jnp.dot(yg.astype(BF16), wglu_ref[...], preferred_element_type=F32) + bglu_ref[...]
    hg = hg_ref[...].reshape(tb * ns, e).astype(F32)
    m = ((yg * (0.5 * jnp.tanh(zh) + 0.5)) * (hg * jnp.tanh(hg) + hg)).astype(BF16)
    nq = S5_T // tb
    for half in range(nq):
        @pl.when(q == half)
        def _(half=half):
            m_ref[half * tb * ns:(half + 1) * tb * ns, :] = m

    @pl.when(q == nq - 1)
    def _():
        perm = _perm_matrix(S5_PB, S5_T)
        for k in range(ns // S5_PB):
            g = jnp.concatenate([m_ref[t * ns + k * S5_PB:t * ns + (k + 1) * S5_PB, :] for t in range(S5_T)], axis=0)
            r = jnp.dot(perm, g, preferred_element_type=F32).astype(BF16)
            o_ref[k * S5_PB:(k + 1) * S5_PB] = r.reshape(S5_PB, S5_T, e)


def _s5_post(y4, hg, wglu, bglu):
    nj, t_all, e, ns = y4.shape
    nt = t_all // S5_TB
    return pl.pallas_call(
        _s5_post_kernel,
        grid=(nj, nt),
        in_specs=[pl.BlockSpec((None, S5_TB, e, ns), lambda j, q: (j, q, 0, 0)),
                  pl.BlockSpec((S5_TB, ns, e), lambda j, q: (j * nt + q, 0, 0)),
                  pl.BlockSpec((e, e), lambda j, q: (0, 0)),
                  pl.BlockSpec((1, e), lambda j, q: (0, 0))],
        out_specs=pl.BlockSpec((ns, S5_T, e), lambda j, q: (0, j, 0)),
        out_shape=jax.ShapeDtypeStruct((ns, nj * S5_T, e), BF16),
        scratch_shapes=[pltpu.VMEM((S5_T * ns, e), BF16)],
        compiler_params=_cparams(("arbitrary", "arbitrary")),
        name="s5_post",
    )(y4, hg, wglu, bglu)


def _softplus(z):
    return jnp.maximum(z, 0.0) + jnp.log1p(jnp.exp(-jnp.abs(z)))


def _gate_matmul(xc, wg_ref):
    xb = xc.astype(BF16)
    nq = wg_ref.shape[0]
    return [jnp.dot(xb[:, q * GATE_TILE:(q + 1) * GATE_TILE], wg_ref[q], preferred_element_type=F32)
            for q in range(nq)]


def _gate_cols(zs, k):
    return jnp.concatenate([z[:, k * GATE_TILE:(k + 1) * GATE_TILE] for z in zs], axis=1)


def _lru_decay_scale(lam_row):
    return (-0.5 * LRU_C * math.log2(math.e)) * _softplus(-lam_row)


def _lru_gate_math(hx, za, zx, ba, bx, c2):
    ta = jnp.tanh(za + ba)
    tx = jnp.tanh(zx + bx)
    a = jnp.exp2(ta * c2 + c2)
    om = 1.0 - a * a
    return a, (om * lax.rsqrt(jnp.maximum(om, TINY))) * (tx * hx + hx)


def _lru_coeffs(hx, zs, k, gbh, c2):
    return _lru_gate_math(hx, _gate_cols(zs, 2 * k), _gate_cols(zs, 2 * k + 1),
                          gbh[2 * k:2 * k + 1], gbh[2 * k + 1:2 * k + 2], c2)


def _norm_mod_rows(x_ref, gs_ref, sh_ref, dst_ref):
    nb, w, _ = x_ref.shape
    for b in range(nb):
        dst_ref[b * w:(b + 1) * w, :] = _norm_mod(x_ref[b], gs_ref[b:b + 1], sh_ref[b:b + 1]).astype(BF16)


def _lru_ctx_kernel(x_ref, gs_ref, sh_ref, wu_ref, cw_ref, cb_ref, wg_ref, gb_ref, lam_ref,
                    ff_ref, fr_ref, af_s, bf_s, ar_s, br_s):
    rows = x_ref.shape[0]
    npos = rows // 8
    xn = _norm_mod(x_ref[...], gs_ref[...], sh_ref[...]).astype(BF16)
    u = jnp.dot(xn, wu_ref[...], preferred_element_type=F32)
    pos = lax.broadcasted_iota(jnp.int32, u.shape, 0) // 8
    um1 = jnp.where(pos >= 1, pltpu.roll(u, 8, 0), 0.0)
    up1 = jnp.where(pos < npos - 1, pltpu.roll(u, rows - 8, 0), 0.0)
    up2 = jnp.where(pos < npos - 2, pltpu.roll(u, rows - 16, 0), 0.0)
    cw = cw_ref[...]
    xc = cw[0:1] * um1 + cw[1:2] * u + cw[2:3] * up1 + cw[3:4] * up2 + cb_ref[...]
    zs = _gate_matmul(xc, wg_ref)
    gbh = gb_ref[...]
    c2 = _lru_decay_scale(lam_ref[...])
    hx = 0.5 * xc
    a, b = _lru_coeffs(hx, zs, 0, gbh, c2[0:1])
    af_s[...] = a
    bf_s[...] = b
    a, b = _lru_coeffs(hx, zs, 1, gbh, c2[1:2])
    ar_s[...] = a
    br_s[...] = b

    def body(p, carry):
        hf, hr, pr = carry
        r0 = pl.multiple_of(p * 8, 8)
        hf = af_s[pl.ds(r0, 8), :] * hf + bf_s[pl.ds(r0, 8), :]
        hr = hr + pr * br_s[pl.ds(r0, 8), :]
        pr = pr * ar_s[pl.ds(r0, 8), :]
        return hf, hr, pr

    zero = jnp.zeros((8, u.shape[1]), F32)
    hf, hr, _ = lax.fori_loop(0, npos, body, (zero, zero, zero + 1.0))
    ff_ref[...] = hf
    fr_ref[...] = hr


def _lru_ctx(xp, gs, sh, wu, cw, cb, wg, gb, lam):
    rows, d = xp.shape
    e = wu.shape[1]
    nq = e // GATE_TILE
    full = lambda a: pl.BlockSpec(a.shape, lambda q, nd=a.ndim: (0,) * nd)
    cols = lambda a: pl.BlockSpec((a.shape[0], GATE_TILE), lambda q: (0, q))
    return pl.pallas_call(
        _lru_ctx_kernel,
        grid=(nq,),
        in_specs=[full(xp), full(gs), full(sh), cols(wu), cols(cw), cols(cb),
                  pl.BlockSpec((1,) + wg.shape[1:], lambda q: (q, 0, 0)), cols(gb), cols(lam)],
        out_specs=[pl.BlockSpec((8, GATE_TILE), lambda q: (0, q)), pl.BlockSpec((8, GATE_TILE), lambda q: (0, q))],
        out_shape=[jax.ShapeDtypeStruct((8, e), F32), jax.ShapeDtypeStruct((8, e), F32)],
        scratch_shapes=[pltpu.VMEM((rows, GATE_TILE), F32) for _ in range(4)],
        compiler_params=_cparams(("arbitrary",)),
    )(xp, gs, sh, wu, cw, cb, wg, gb, lam)


def _lru_pass_a_kernel(x_ref, gs_ref, sh_ref, wu_ref, cw_ref, cb_ref, wg_ref, gb_ref, lam_ref,
                       xc_ref, hl_ref, pc_ref, pr_ref, hr_ref, ring_ref, hs_ref, ps_ref, xb_ref, xcb_ref, z_ref, *,
                       nrow):
    i = pl.program_id(0)
    nb, w, d = x_ref.shape
    rows = nb * w
    nq = wg_ref.shape[0]
    gt = GATE_TILE
    _norm_mod_rows(x_ref, gs_ref, sh_ref, xb_ref)
    u = jnp.dot(xb_ref[...], wu_ref[...], preferred_element_type=F32)
    ring_ref[i & 3] = u

    @pl.when(i == 0)
    def _():
        col = lax.broadcasted_iota(jnp.int32, u.shape, 0) % w
        ring_ref[0] = jnp.where(col == 0, 0.0, pltpu.roll(u, 1, 0))

    @pl.when(i > nrow)
    def _():
        col = lax.broadcasted_iota(jnp.int32, u.shape, 0) % w
        ring_ref[i & 3] = jnp.where(col == w - 1, 0.0, pltpu.roll(u, rows - 1, 0))

    @pl.when(i == 3)
    def _():
        hs_ref[...] = jnp.zeros_like(hs_ref)
        ps_ref[...] = jnp.ones_like(ps_ref)
        hr_ref[...] = jnp.zeros_like(hr_ref)
        pr_ref[...] = jnp.ones_like(pr_ref)

    @pl.when(i >= 3)
    def _():
        cw = cw_ref[...]
        cb = cb_ref[...]
        s0, s1, s2, s3 = (i - 3) & 3, (i - 2) & 3, (i - 1) & 3, i & 3
        for r0 in range(0, rows, LRU_RC):
            rs = slice(r0, r0 + LRU_RC)
            xc = (cw[0:1] * ring_ref[s0, rs, :] + cw[1:2] * ring_ref[s1, rs, :]
                  + cw[2:3] * ring_ref[s2, rs, :] + cw[3:4] * ring_ref[s3, rs, :] + cb)
            xc_ref[rs, :] = xc
            xcb_ref[rs, :] = xc.astype(BF16)
        for q in range(nq):
            z_ref[q] = jnp.dot(xcb_ref[:, q * gt:(q + 1) * gt], wg_ref[q], preferred_element_type=F32)
        gbh = gb_ref[...]
        c2 = _lru_decay_scale(lam_ref[...])
        for q in range(nq):
            cs = slice(q * gt, (q + 1) * gt)
            for r0 in range(0, rows, LRU_RC):
                rs = slice(r0, r0 + LRU_RC)
                hx = 0.5 * xc_ref[rs, cs]
                af, bf = _lru_gate_math(hx, z_ref[q, rs, 0:gt], z_ref[q, rs, gt:2 * gt],
                                        gbh[0:1, cs], gbh[1:2, cs], c2[0:1, cs])
                ar, br = _lru_gate_math(hx, z_ref[q, rs, 2 * gt:3 * gt], z_ref[q, rs, 3 * gt:4 * gt],
                                        gbh[2:3, cs], gbh[3:4, cs], c2[1:2, cs])
                h = af * hs_ref[rs, cs] + bf
                p = ps_ref[rs, cs] * af
                hl_ref[rs, cs] = h
                pc_ref[rs, cs] = p
                hs_ref[rs, cs] = h
                ps_ref[rs, cs] = p
                prod = pr_ref[rs, cs]
                hr_ref[rs, cs] = hr_ref[rs, cs] + prod * br
                pr_ref[rs, cs] = prod * ar


def _lru_pass_a(x4, gs8, sh8, wu, cw, cb, wg, gb, lam):
    nb, nrow, w, d = x4.shape
    e = wu.shape[1]
    rows = nb * w
    kern = functools.partial(_lru_pass_a_kernel, nrow=nrow)
    const = lambda a: pl.BlockSpec(a.shape, lambda i, nd=a.ndim: (0,) * nd)
    acc = pl.BlockSpec((rows, e), lambda i: (0, 0))
    acc_shape = jax.ShapeDtypeStruct((rows, e), F32)
    per_row = pl.BlockSpec((None, rows, e), lambda i: (jnp.maximum(i - 3, 0), 0, 0))
    per_row_shape = jax.ShapeDtypeStruct((nrow, rows, e), F32)
    return pl.pallas_call(
        kern,
        grid=(nrow + 3,),
        in_specs=[pl.BlockSpec((nb, None, w, d), lambda i: (0, (i + nrow - 1) % nrow, 0, 0)),
                  const(gs8), const(sh8), const(wu), const(cw), const(cb), const(wg), const(gb), const(lam)],
        out_specs=[per_row, per_row, per_row, acc, acc],
        out_shape=[per_row_shape, per_row_shape, per_row_shape, acc_shape, acc_shape],
        scratch_shapes=[pltpu.VMEM((4, rows, e), F32), pltpu.VMEM((rows, e), F32), pltpu.VMEM((rows, e), F32),
                        pltpu.VMEM((rows, d), BF16), pltpu.VMEM((rows, e), BF16),
                        pltpu.VMEM((wg.shape[0], rows, wg.shape[2]), F32)],
        compiler_params=_cparams(("arbitrary",)),
        name="lru_pass_a",
    )(x4, gs8, sh8, wu, cw, cb, wg, gb, lam)


def _lru_stitch_kernel(pf_ref, hf_ref, pr_ref, hr_ref, ff_ref, fr_ref, sf_ref, sr_ref, *, w):
    rows = pf_ref.shape[0]
    col = lax.broadcasted_iota(jnp.int32, pf_ref.shape, 0) % w
    p, h = pf_ref[...], hf_ref[...]
    k = 1
    while k < w:
        keep = col >= k
        ps = jnp.where(keep, pltpu.roll(p, k, 0), 1.0)
        hs = jnp.where(keep, pltpu.roll(h, k, 0), 0.0)
        p, h = p * ps, p * hs + h
        k *= 2
    fin = ff_ref[...]
    sf_ref[...] = jnp.where(col == 0, fin, pltpu.roll(p, 1, 0) * fin + pltpu.roll(h, 1, 0))
    p, h = pr_ref[...], hr_ref[...]
    k = 1
    while k < w:
        keep = col < w - k
        ps = jnp.where(keep, pltpu.roll(p, rows - k, 0), 1.0)
        hs = jnp.where(keep, pltpu.roll(h, rows - k, 0), 0.0)
        p, h = p * ps, p * hs + h
        k *= 2
    fin = fr_ref[...]
    sr_ref[...] = jnp.where(col == w - 1, fin, pltpu.roll(p, rows - 1, 0) * fin + pltpu.roll(h, rows - 1, 0))


def _lru_stitch(pcum, hloc, pr, hr, ff_rows, fr_rows, w):
    nrow, rows, e = pcum.shape
    big = pl.BlockSpec((rows, e), lambda i: (0, 0))
    last = pl.BlockSpec((None, rows, e), lambda i: (nrow - 1, 0, 0))
    shp = jax.ShapeDtypeStruct((rows, e), F32)
    return pl.pallas_call(
        functools.partial(_lru_stitch_kernel, w=w),
        grid=(1,),
        in_specs=[last, last, big, big, big, big],
        out_specs=[big, big],
        out_shape=[shp, shp],
        compiler_params=_cparams(("arbitrary",)),
        name="lru_stitch",
    )(pcum, hloc, pr, hr, ff_rows, fr_rows)


def _lru_rev_out_kernel(xc_ref, hl_ref, pc_ref, x_ref, m5_ref, gs_ref, sh_ref, gt_ref, wg_ref, gb_ref, lam_ref,
                        hsf_ref, hsr_ref, wgl_ref, wo_ref, fg_ref, o_ref, h_ref, xb_ref, xcb_ref, z_ref, g_ref,
                        mb_ref, mix_ref):
    i = pl.program_id(0)
    nb, w, d = x_ref.shape
    rows = nb * w
    e = xc_ref.shape[-1]
    nq = wg_ref.shape[0]
    gt = GATE_TILE

    @pl.when(i == 0)
    def _():
        h_ref[...] = hsr_ref[...]

    _norm_mod_rows(x_ref, gs_ref, sh_ref, xb_ref)
    g_ref[...] = jnp.dot(xb_ref[...], wgl_ref[...], preferred_element_type=F32)
    for r0 in range(0, rows, LRU_RC):
        xcb_ref[r0:r0 + LRU_RC, :] = xc_ref[r0:r0 + LRU_RC, :].astype(BF16)
    for q in range(nq):
        z_ref[q] = jnp.dot(xcb_ref[:, q * gt:(q + 1) * gt], wg_ref[q], preferred_element_type=F32)
    gbh = gb_ref[...]
    c2 = _lru_decay_scale(lam_ref[...])
    for q in range(nq):
        cs = slice(q * gt, (q + 1) * gt)
        for r0 in range(0, rows, LRU_RC):
            rs = slice(r0, r0 + LRU_RC)
            a, b = _lru_gate_math(0.5 * xc_ref[rs, cs], z_ref[q, rs, 0:gt], z_ref[q, rs, gt:2 * gt],
                                  gbh[0:1, cs], gbh[1:2, cs], c2[:, cs])
            h = a * h_ref[rs, cs] + b
            h_ref[rs, cs] = h
            y = hl_ref[rs, cs] + pc_ref[rs, cs] * hsf_ref[rs, cs] + h
            hg = g_ref[rs, cs]
            mb_ref[rs, cs] = (y * (hg * jnp.tanh(hg) + hg)).astype(BF16)
    mix_ref[...] = (jnp.dot(m5_ref[...].reshape(rows, e), wo_ref[0:e, :], preferred_element_type=F32)
                    + jnp.dot(mb_ref[...], wo_ref[e:2 * e, :], preferred_element_type=F32))
    fg = fg_ref[...]
    for b in range(nb):
        for r0 in range(0, w, LRU_RC):
            res = x_ref[b, r0:r0 + LRU_RC, :] + gt_ref[b:b + 1] * mix_ref[b * w + r0:b * w + r0 + LRU_RC, :]
            ms = jnp.mean(res * res, axis=-1, keepdims=True)
            o_ref[b, r0:r0 + LRU_RC, :] = res * lax.rsqrt(ms + EPS) * fg


def _lru_rev_out(xc, hloc, pcum, x4, m5, gs8, sh8, gt8, wg, gb, lam, hs_f, hs_r, wgl, wo, fg):
    nb, nrow, w, d = x4.shape
    _, rows, e = xc.shape
    const = lambda a: pl.BlockSpec(a.shape, lambda i, nd=a.ndim: (0,) * nd)
    rev3 = pl.BlockSpec((None, rows, e), lambda i: (nrow - 1 - i, 0, 0))
    rev4 = pl.BlockSpec((nb, None, w, d), lambda i: (0, nrow - 1 - i, 0, 0))
    rev4e = pl.BlockSpec((nb, None, w, e), lambda i: (0, nrow - 1 - i, 0, 0))
    return pl.pallas_call(
        _lru_rev_out_kernel,
        grid=(nrow,),
        in_specs=[rev3, rev3, rev3, rev4, rev4e, const(gs8), const(sh8), const(gt8), const(wg),
                  const(gb), const(lam), const(hs_f), const(hs_r), const(wgl), const(wo), const(fg)],
        out_specs=rev4,
        out_shape=jax.ShapeDtypeStruct((nb, nrow, w, d), F32),
        scratch_shapes=[pltpu.VMEM((rows, e), F32), pltpu.VMEM((rows, d), BF16), pltpu.VMEM((rows, e), BF16),
                        pltpu.VMEM((wg.shape[0], rows, wg.shape[2]), F32), pltpu.VMEM((rows, e), F32),
                        pltpu.VMEM((rows, e), BF16), pltpu.VMEM((rows, d), F32)],
        compiler_params=_cparams(("arbitrary",)),
        name="lru_rev_out",
    )(xc, hloc, pcum, x4, m5, gs8, sh8, gt8, wg, gb, lam, hs_f, hs_r, wgl, wo, fg)


def _gate_tiles(w_a, w_x):
    ndir, heads, hd, _ = w_a.shape
    per = GATE_TILE // hd
    nq = heads // per

    def tiles(w):
        wq = w.reshape(nq, per, hd, hd)
        eye = jnp.eye(per, dtype=w.dtype)
        blk = wq[:, :, :, None, :] * eye[None, :, None, :, None]
        return blk.reshape(nq, GATE_TILE, GATE_TILE)

    return jnp.concatenate([tiles(w_a[0]), tiles(w_x[0]), tiles(w_a[1]), tiles(w_x[1])], axis=2).astype(BF16)


def kernel(x, c, ctx, c_ctx, w_mod, b_mod, norm_g, w_in, s5_a_re, s5_a_im, s5_log_step, s5_b_re, s5_b_im,
           s5_c_re, s5_c_im, s5_d, s5_w_glu, s5_b_glu, lru_conv_w, lru_conv_b, lru_w_a, lru_b_a, lru_w_x,
           lru_b_x, lru_lam, w_out, final_g):
    bsz, seq, d = x.shape
    nctx_tok = ctx.shape[1]
    e = s5_w_glu.shape[-1]
    assert w_mod.shape[0] == 1, "single-layer block"
    assert bsz <= 8 and seq % S5_SEG == 0 and seq % GRID_W == 0 and nctx_tok % S5_T == 0
    nseg = seq // S5_SEG
    ns = bsz * nseg
    nctx = nctx_tok // S5_T
    assert nctx <= nseg
    nrow = seq // GRID_W

    c8 = jnp.zeros((8, d), F32).at[:bsz].set(c).at[bsz].set(c_ctx)
    mod = _modulation(c8, w_mod[0], b_mod[0])
    sh, sc, gt = mod[:, :d], mod[:, d:2 * d], mod[:, 2 * d:]
    gs = norm_g[0][None, :] * (1.0 + sc)

    w_in0 = w_in[0]
    wt_us5 = w_in0[:, :e].T.astype(BF16)
    w_gs5 = (0.5 * w_in0[:, e:2 * e]).astype(BF16)
    w_ulru = w_in0[:, 2 * e:3 * e].astype(BF16)
    w_glru = (0.5 * w_in0[:, 3 * e:]).astype(BF16)
    w_glu = (0.5 * s5_w_glu[0]).astype(BF16)
    wo = w_out[0].astype(BF16)

    s5_params = _s5_param_layout(s5_a_re[0], s5_a_im[0], s5_log_step[0], s5_b_re[0], s5_b_im[0],
                                 s5_c_re[0], s5_c_im[0], s5_d[0])
    x3 = x.reshape(ns, S5_SEG, d)
    xt, hg5 = _s5_inproj(x3, gs, sh, wt_us5, w_gs5, bsz)
    ctx_pad = jnp.zeros((bsz, nseg, S5_T, d), F32).at[:, :nctx].set(ctx.reshape(bsz, nctx, S5_T, d))
    gs_c = jnp.broadcast_to(gs[bsz:bsz + 1], (8, d))
    sh_c = jnp.broadcast_to(sh[bsz:bsz + 1], (8, d))
    xct, _ = _s5_inproj(ctx_pad.reshape(ns, S5_T, d), gs_c, sh_c, wt_us5, w_gs5, bsz)
    y5 = _s5_core(xt, xct, s5_params, nseg, nctx)
    nj = S5_SEG // S5_T
    m5 = _s5_post(y5.reshape(nj, S5_T, e, ns), hg5, w_glu, 0.5 * s5_b_glu[0].reshape(1, e))

    wg_all = _gate_tiles(0.5 * lru_w_a[0], 0.5 * lru_w_x[0])
    gb_all = 0.5 * jnp.stack([lru_b_a[0, 0], lru_b_x[0, 0], lru_b_a[0, 1], lru_b_x[0, 1]])
    cw = lru_conv_w[0]
    cb = lru_conv_b[0].reshape(1, e)
    lam = lru_lam[0]
    ctx_p = jnp.zeros((8, nctx_tok, d), F32).at[:bsz].set(ctx).transpose(1, 0, 2).reshape(nctx_tok * 8, d)
    ff, fr = _lru_ctx(ctx_p, gs[bsz:bsz + 1], sh[bsz:bsz + 1], w_ulru, cw, cb, wg_all, gb_all, lam)
    x4 = x.reshape(bsz, nrow, GRID_W, d)
    xc, hloc, pcum, pr, hr = _lru_pass_a(x4, gs, sh, w_ulru, cw, cb, wg_all, gb_all, lam)
    hs_f, hs_r = _lru_stitch(pcum, hloc, pr, hr, jnp.repeat(ff[:bsz], GRID_W, axis=0),
                             jnp.repeat(fr[:bsz], GRID_W, axis=0), GRID_W)
    out4 = _lru_rev_out(xc, hloc, pcum, x4, m5.reshape(bsz, nrow, GRID_W, e), gs, sh, gt,
                        wg_all[:, :, 2 * GATE_TILE:], gb_all[2:4], lam[1:2], hs_f, hs_r, w_glru, wo,
                        final_g.reshape(1, d))
    return out4.reshape(bsz, seq, d)
```

```python
import functools
import math

import jax
import jax.numpy as jnp
from jax import lax
from jax.experimental import pallas as pl
from jax.experimental.pallas import tpu as pltpu

F32 = jnp.float32
BF16 = jnp.bfloat16

EPS = 1e-6
TINY = 1e-30
GRID_W = 64
LRU_C = 8.0
S5_H = 16
S5_N = 64
S5_T = 16
S5_SEG = 256
S5_TB = 8
S5_PB = 16
LRU_HEAD = 64
GATE_TILE = 256
LRU_RC = 16
VMEM_LIMIT = 56 * 1024 * 1024


def _cparams(sem):
    return pltpu.CompilerParams(dimension_semantics=sem, vmem_limit_bytes=VMEM_LIMIT)


def _sigmoid(z):
    return 0.5 * (jnp.tanh(0.5 * z) + 1.0)


def _gelu_tanh(y):
    return 0.5 * y * (1.0 + jnp.tanh(math.sqrt(2.0 / math.pi) * (y + 0.044715 * (y * y * y))))


def _norm_mod(x, gs, sh):
    ms = jnp.mean(x * x, axis=-1, keepdims=True)
    return x * lax.rsqrt(ms + EPS) * gs + sh


def _mod_kernel(c_ref, w_ref, b_ref, o_ref):
    c = c_ref[...]
    s = c * _sigmoid(c)
    o_ref[...] = jnp.dot(s, w_ref[...], preferred_element_type=F32) + b_ref[...]


def _modulation(c8, w_mod, b_mod):
    d = c8.shape[1]
    n = w_mod.shape[1]
    nb = n // d
    return pl.pallas_call(
        _mod_kernel,
        grid=(nb,),
        in_specs=[pl.BlockSpec((8, d), lambda i: (0, 0)),
                  pl.BlockSpec((d, d), lambda i: (0, i)),
                  pl.BlockSpec((1, d), lambda i: (0, i))],
        out_specs=pl.BlockSpec((8, d), lambda i: (0, i)),
        out_shape=jax.ShapeDtypeStruct((8, n), F32),
        compiler_params=_cparams(("arbitrary",)),
    )(c8, w_mod, b_mod.reshape(1, n))


def _perm_matrix(n_a, n_b):
    n = n_a * n_b
    r = lax.broadcasted_iota(jnp.int32, (n, n), 0)
    c = lax.broadcasted_iota(jnp.int32, (n, n), 1)
    return ((r // n_b == c % n_a) & (r % n_b == c // n_a)).astype(BF16)


def _s5_inproj_kernel(x_ref, gs_ref, sh_ref, wt_ref, wg_ref, o_ref, hg_ref, xs_ref, *, nbatch):
    ns, tb, d = x_ref.shape
    groups = o_ref.shape[0]
    rows_b = (ns // nbatch) * tb
    blk = S5_PB * tb
    perm = _perm_matrix(tb, S5_PB)
    x = x_ref[...].reshape(ns * tb, d)
    xn = jnp.concatenate(
        [_norm_mod(x[b * rows_b:(b + 1) * rows_b], gs_ref[b:b + 1], sh_ref[b:b + 1]).astype(BF16)
         for b in range(nbatch)], axis=0)
    for k in range(ns // S5_PB):
        q = jnp.dot(perm, xn[k * blk:(k + 1) * blk], preferred_element_type=F32).astype(BF16)
        for t in range(tb):
            xs_ref[t * ns + k * S5_PB:t * ns + (k + 1) * S5_PB, :] = q[t * S5_PB:(t + 1) * S5_PB]
    wt = wt_ref[...]
    for t in range(0, tb, 2):
        ut = lax.dot_general(wt, xs_ref[t * ns:(t + 2) * ns, :], (((1,), (1,)), ((), ())),
                             preferred_element_type=F32)
        o_ref[:, t * S5_H:(t + 1) * S5_H, :] = ut[:, :ns].reshape(groups, S5_H, ns).astype(BF16)
        o_ref[:, (t + 1) * S5_H:(t + 2) * S5_H, :] = ut[:, ns:].reshape(groups, S5_H, ns).astype(BF16)
    hg = jnp.dot(xs_ref[...], wg_ref[...], preferred_element_type=F32)
    hg_ref[...] = hg.astype(BF16).reshape(tb, ns, hg.shape[1])


def _s5_inproj(x3, gs8, sh8, wt, wg, nbatch):
    ns, seg, d = x3.shape
    e = wt.shape[0]
    groups = e // S5_H
    nj = seg // S5_T
    per = S5_T // S5_TB
    return pl.pallas_call(
        functools.partial(_s5_inproj_kernel, nbatch=nbatch),
        grid=(seg // S5_TB,),
        in_specs=[pl.BlockSpec((ns, S5_TB, d), lambda i: (0, i, 0)),
                  pl.BlockSpec((8, d), lambda i: (0, 0)),
                  pl.BlockSpec((8, d), lambda i: (0, 0)),
                  pl.BlockSpec((e, d), lambda i: (0, 0)),
                  pl.BlockSpec((d, e), lambda i: (0, 0))],
        out_specs=[pl.BlockSpec((groups, S5_TB * S5_H, ns), lambda i: (0, i % per, i // per)),
                   pl.BlockSpec((S5_TB, ns, e), lambda i: (i, 0, 0))],
        out_shape=[jax.ShapeDtypeStruct((groups, S5_T * S5_H, nj * ns), BF16),
                   jax.ShapeDtypeStruct((seg, ns, e), BF16)],
        scratch_shapes=[pltpu.VMEM((S5_TB * ns, d), BF16)],
        compiler_params=_cparams(("arbitrary",)),
        name="s5_inproj",
    )(x3, gs8, sh8, wt, wg)


def _cmul_add(dr, di, hr, hi, sr, si):
    return dr * hr - di * hi + sr, dr * hi + di * hr + si


def _lane_scan(vr, vi, dr, di, slot, count, reverse):
    n = vr.shape[1]
    k = 1
    while k < count:
        if reverse:
            sr = pltpu.roll(vr, n - k, 1)
            si = pltpu.roll(vi, n - k, 1)
            keep = slot < count - k
        else:
            sr = pltpu.roll(vr, k, 1)
            si = pltpu.roll(vi, k, 1)
            keep = slot >= k
        sr = jnp.where(keep, sr, 0.0)
        si = jnp.where(keep, si, 0.0)
        vr, vi = _cmul_add(dr, di, sr, si, vr, vi)
        dr, di = dr * dr - di * di, 2.0 * dr * di
        k *= 2
    return vr, vi


def _csq(r, i):
    return r * r - i * i, 2.0 * r * i


def _cpow_bits(e, pows):
    pr = jnp.ones(e.shape, F32)
    pi = jnp.zeros(e.shape, F32)
    for bit, (qr, qi) in pows:
        on = (e & bit) != 0
        pr, pi = jnp.where(on, pr * qr - pi * qi, pr), jnp.where(on, pr * qi + pi * qr, pi)
    return pr, pi


def _s5_discretize(ar, ai, ls):
    step = jnp.exp(ls)
    mag = jnp.exp(ar * step)
    pr = mag * jnp.cos(ai * step)
    pi = mag * jnp.sin(ai * step)
    den = ar * ar + ai * ai
    nr = pr - 1.0
    return pr, pi, (nr * ar + pi * ai) / den, (pi * ar - nr * ai) / den


def _s5_operators(pcol_ref, prow_ref, bt_ref, ct1_ref, ct2_ref, dsk_ref):
    n, t_len, th = S5_N, S5_T, S5_T * S5_H
    hp = lax.Precision.HIGHEST
    lane_t = lax.broadcasted_iota(jnp.int32, (1, th), 1) // S5_H
    bits = [1 << b for b in range(t_len.bit_length() - 1)]

    blocks, bbs, decay = [], [], []
    for d in range(2):
        pr, pi, cr, ci = _s5_discretize(pcol_ref[3 * d], pcol_ref[3 * d + 1], pcol_ref[3 * d + 2])
        pows, q = [], (pr, pi)
        for bit in bits:
            pows.append((bit, q))
            q = _csq(*q)
        decay.append(q)
        btr = bt_ref[2 * d * n:(2 * d + 1) * n, :]
        bti = bt_ref[(2 * d + 1) * n:(2 * d + 2) * n, :]
        bbr = cr * btr - ci * bti
        bbi = cr * bti + ci * btr
        e = (t_len - 1 - lane_t) if d == 0 else lane_t
        wr, wi = _cpow_bits(jnp.broadcast_to(e, (n, th)), pows)
        blocks += [wr * bbr - wi * bbi, wr * bbi + wi * bbr]
        bbs.append(jnp.concatenate([bbr, bbi], axis=0))
    bpow = jnp.concatenate(blocks, axis=0).astype(BF16)

    qr, qi, _, _ = _s5_discretize(prow_ref[0:1], prow_ref[1:2], prow_ref[2:3])
    pows, q = [], (qr, qi)
    for bit in bits:
        pows.append((bit, q))
        q = _csq(*q)
    q_t = q
    row_t = lax.broadcasted_iota(jnp.int32, (th, 1), 0) // S5_H
    is_f = lax.broadcasted_iota(jnp.int32, (1, 4 * n), 1) < 2 * n
    e_k = jnp.where(is_f, row_t, (t_len - row_t) % t_len)
    kr, ki = _cpow_bits(e_k, pows)
    nr = jnp.where(is_f, kr * qr - ki * qi, jnp.where(row_t == 0, q_t[0], kr))
    ni = jnp.where(is_f, kr * qi + ki * qr, jnp.where(row_t == 0, q_t[1], ki))
    ct1 = jnp.concatenate([ct1_ref[...]] * t_len, axis=0)
    ct2 = jnp.concatenate([ct2_ref[...]] * t_len, axis=0)
    cpow = (ct1 * nr + ct2 * ni).astype(BF16)
    qk = ct1 * kr + ct2 * ki
    mf = jnp.dot(qk[:, :2 * n], bbs[0], precision=hp, preferred_element_type=F32)
    mr = jnp.dot(qk[:, 2 * n:], bbs[1], precision=hp, preferred_element_type=F32)
    m = (jnp.where(row_t <= t_len - 1 - lane_t, mf, 0.0)
         + jnp.where((row_t == 0) | (row_t >= t_len - lane_t), mr, 0.0))
    for bit in bits:
        m = jnp.where((lane_t & bit) != 0, pltpu.roll(m, S5_H * bit, 0), m)
    row = lax.broadcasted_iota(jnp.int32, (th, th), 0)
    col = lax.broadcasted_iota(jnp.int32, (th, th), 1)
    toep = (m + jnp.where(row == col, dsk_ref[...], 0.0)).astype(BF16)
    return toep, bpow, cpow, decay


def _s5_core_kernel(xt_ref, xc_ref, pcol_ref, prow_ref, bt_ref, ct1_ref, ct2_ref, dsk_ref, o_ref, s_ref, *,
                    nseg, nctx):
    n = S5_N
    ns = xc_ref.shape[-1]
    nj = xt_ref.shape[-1] // ns
    toep, bpow, cpow, decay = _s5_operators(pcol_ref, prow_ref, bt_ref, ct1_ref, ct2_ref, dsk_ref)
    dfr = jnp.broadcast_to(decay[0][0], (n, ns))
    dfi = jnp.broadcast_to(decay[0][1], (n, ns))
    drr = jnp.broadcast_to(decay[1][0], (n, ns))
    dri = jnp.broadcast_to(decay[1][1], (n, ns))
    slot = lax.broadcasted_iota(jnp.int32, (n, ns), 1) % nseg

    sc = jnp.dot(bpow, xc_ref[...], preferred_element_type=F32)
    valid = slot < nctx
    cfr, cfi = _lane_scan(jnp.where(valid, sc[0:n], 0.0), jnp.where(valid, sc[n:2 * n], 0.0),
                          dfr, dfi, slot, nseg, False)
    crr, cri = _lane_scan(jnp.where(valid, sc[2 * n:3 * n], 0.0), jnp.where(valid, sc[3 * n:4 * n], 0.0),
                          drr, dri, slot, nseg, True)

    step = 2 if nj % 2 == 0 else 1
    for j in range(0, nj, step):
        cols = slice(j * ns, (j + step) * ns)
        s_ref[:, cols] = jnp.dot(bpow, xt_ref[:, cols], preferred_element_type=F32)

    zero = jnp.zeros((n, ns), F32)
    efr, efi, err, eri = zero, zero, zero, zero
    for j in range(nj):
        cols = slice(j * ns, (j + 1) * ns)
        efr, efi = _cmul_add(dfr, dfi, efr, efi, s_ref[0:n, cols], s_ref[n:2 * n, cols])
    for j in range(nj - 1, -1, -1):
        cols = slice(j * ns, (j + 1) * ns)
        err, eri = _cmul_add(drr, dri, err, eri, s_ref[2 * n:3 * n, cols], s_ref[3 * n:4 * n, cols])

    sfr, sfi, srr, sri = dfr, dfi, drr, dri
    k = 1
    while k < nj:
        sfr, sfi = sfr * sfr - sfi * sfi, 2.0 * sfr * sfi
        srr, sri = srr * srr - sri * sri, 2.0 * srr * sri
        k *= 2

    first = slot == 0
    last = slot == nseg - 1
    vfr = jnp.where(first, pltpu.roll(cfr, ns - (nctx - 1), 1) if nctx > 1 else cfr, pltpu.roll(efr, 1, 1))
    vfi = jnp.where(first, pltpu.roll(cfi, ns - (nctx - 1), 1) if nctx > 1 else cfi, pltpu.roll(efi, 1, 1))
    vrr = jnp.where(last, pltpu.roll(crr, nseg - 1, 1) if nseg > 1 else crr, pltpu.roll(err, ns - 1, 1))
    vri = jnp.where(last, pltpu.roll(cri, nseg - 1, 1) if nseg > 1 else cri, pltpu.roll(eri, ns - 1, 1))
    hfr, hfi = _lane_scan(vfr, vfi, sfr, sfi, slot, nseg, False)
    hrr, hri = _lane_scan(vrr, vri, srr, sri, slot, nseg, True)

    for j in range(nj):
        cols = slice(j * ns, (j + 1) * ns)
        nr, ni = _cmul_add(dfr, dfi, hfr, hfi, s_ref[0:n, cols], s_ref[n:2 * n, cols])
        s_ref[0:n, cols] = hfr
        s_ref[n:2 * n, cols] = hfi
        hfr, hfi = nr, ni
    for j in range(nj - 1, -1, -1):
        cols = slice(j * ns, (j + 1) * ns)
        nr, ni = _cmul_add(drr, dri, hrr, hri, s_ref[2 * n:3 * n, cols], s_ref[3 * n:4 * n, cols])
        s_ref[2 * n:3 * n, cols] = hrr
        s_ref[3 * n:4 * n, cols] = hri
        hrr, hri = nr, ni

    for j in range(0, nj, step):
        cols = slice(j * ns, (j + step) * ns)
        y = (jnp.dot(toep, xt_ref[:, cols], preferred_element_type=F32)
             + jnp.dot(cpow, s_ref[:, cols].astype(BF16), preferred_element_type=F32))
        for q in range(step):
            o_ref[j + q] = y[:, q * ns:(q + 1) * ns].reshape(S5_T, S5_H, ns)


def _s5_core(xt, xct, params, nseg, nctx):
    groups, rows, lanes = xt.shape
    ns = xct.shape[-1]
    nj = lanes // ns
    kern = functools.partial(_s5_core_kernel, nseg=nseg, nctx=nctx)
    per_group = lambda a: pl.BlockSpec((None,) + a.shape[1:], lambda g, nd=a.ndim: (g,) + (0,) * (nd - 1))
    return pl.pallas_call(
        kern,
        grid=(groups,),
        in_specs=[per_group(xt), per_group(xct)] + [per_group(a) for a in params],
        out_specs=pl.BlockSpec((nj, S5_T, None, S5_H, ns), lambda g: (0, 0, g, 0, 0)),
        out_shape=jax.ShapeDtypeStruct((nj, S5_T, groups, S5_H, ns), F32),
        scratch_shapes=[pltpu.VMEM((rows, lanes), F32)],
        compiler_params=_cparams(("arbitrary",)),
        name="s5_core",
    )(xt, xct, *params)


def _s5_param_layout(a_re, a_im, log_step, b_re, b_im, c_re, c_im, d_skip):
    ndir, groups, n = a_re.shape
    h = b_re.shape[-1]
    ls = jnp.broadcast_to(log_step[..., None], a_re.shape)
    pcol = jnp.stack([a_re[0], a_im[0], ls[0], a_re[1], a_im[1], ls[1]], axis=1)[..., None]
    lay = lambda v: jnp.concatenate([v[0], v[0], v[1], v[1]], axis=-1)
    prow = jnp.stack([lay(a_re), lay(a_im), lay(ls)], axis=1)
    tile_t = lambda v: jnp.tile(v, (1, 1, S5_T))
    bt = jnp.concatenate([tile_t(b_re[0]), tile_t(b_im[0]), tile_t(b_re[1]), tile_t(b_im[1])], axis=1)
    ct1 = jnp.concatenate([c_re[0], -c_im[0], c_re[1], -c_im[1]], axis=-1)
    ct2 = jnp.concatenate([-c_im[0], -c_re[0], -c_im[1], -c_re[1]], axis=-1)
    dsk = jnp.tile(d_skip.reshape(groups, 1, h), (1, 1, S5_T))
    return tuple(v.astype(F32) for v in (pcol, prow, bt, ct1, ct2, dsk))


def _s5_post_kernel(y_ref, hg_ref, wglu_ref, bglu_ref, o_ref, m_ref):
    q = pl.program_id(1)
    tb, e, ns = y_ref.shape
    ys = jnp.concatenate([y_ref[t].T for t in range(tb)], axis=0)
    yg = _gelu_tanh(ys)
    zh = jnp.dot(yg.astype(BF16), wglu_ref[...], preferred_element_type=F32) + bglu_ref[...]
    hg = hg_ref[...].reshape(tb * ns, e).astype(F32)
    m = ((yg * (0.5 * jnp.tanh(zh) + 0.5)) * (hg * jnp.tanh(hg) + hg)).astype(BF16)
    nq = S5_T // tb
    for half in range(nq):
        @pl.when(q == half)
        def _(half=half):
            m_ref[half * tb * ns:(half + 1) * tb * ns, :] = m

    @pl.when(q == nq - 1)
    def _():
        perm = _perm_matrix(S5_PB, S5_T)
        for k in range(ns // S5_PB):
            g = jnp.concatenate([m_ref[t * ns + k * S5_PB:t * ns + (k + 1) * S5_PB, :] for t in range(S5_T)], axis=0)
            r = jnp.dot(perm, g, preferred_element_type=F32).astype(BF16)
            o_ref[k * S5_PB:(k + 1) * S5_PB] = r.reshape(S5_PB, S5_T, e)


def _s5_post(y4, hg, wglu, bglu):
    nj, t_all, e, ns = y4.shape
    nt = t_all // S5_TB
    return pl.pallas_call(
        _s5_post_kernel,
        grid=(nj, nt),
        in_specs=[pl.BlockSpec((None, S5_TB, e, ns), lambda j, q: (j, q, 0, 0)),
                  pl.BlockSpec((S5_TB, ns, e), lambda j, q: (j * nt + q, 0, 0)),
                  pl.BlockSpec((e, e), lambda j, q: (0, 0)),
                  pl.BlockSpec((1, e), lambda j, q: (0, 0))],
        out_specs=pl.BlockSpec((ns, S5_T, e), lambda j, q: (0, j, 0)),
        out_shape=jax.ShapeDtypeStruct((ns, nj * S5_T, e), BF16),
        scratch_shapes=[pltpu.VMEM((S5_T * ns, e), BF16)],
        compiler_params=_cparams(("arbitrary", "arbitrary")),
        name="s5_post",
    )(y4, hg, wglu, bglu)


def _softplus(z):
    return jnp.maximum(z, 0.0) + jnp.log1p(jnp.exp(-jnp.abs(z)))


def _gate_matmul(xc, wg_ref):
    xb = xc.astype(BF16)
    nq = wg_ref.shape[0]
    return [jnp.dot(xb[:, q * GATE_TILE:(q + 1) * GATE_TILE], wg_ref[q], preferred_element_type=F32)
            for q in range(nq)]


def _gate_cols(zs, k):
    return jnp.concatenate([z[:, k * GATE_TILE:(k + 1) * GATE_TILE] for z in zs], axis=1)


def _lru_decay_scale(lam_row):
    return (-0.5 * LRU_C * math.log2(math.e)) * _softplus(-lam_row)


def _lru_gate_math(hx, za, zx, ba, bx, c2):
    ta = jnp.tanh(za + ba)
    tx = jnp.tanh(zx + bx)
    la = ta * c2 + c2
    a = jnp.exp2(la)
    om = 1.0 - a * a
    return la, a, (om * lax.rsqrt(jnp.maximum(om, TINY))) * (tx * hx + hx)


def _lru_coeffs(hx, zs, k, gbh, c2):
    return _lru_gate_math(hx, _gate_cols(zs, 2 * k), _gate_cols(zs, 2 * k + 1),
                          gbh[2 * k:2 * k + 1], gbh[2 * k + 1:2 * k + 2], c2)[1:]


def _norm_mod_rows(x_ref, gs_ref, sh_ref, dst_ref):
    nb, w, _ = x_ref.shape
    for b in range(nb):
        dst_ref[b * w:(b + 1) * w, :] = _norm_mod(x_ref[b], gs_ref[b:b + 1], sh_ref[b:b + 1]).astype(BF16)


def _lru_ctx_kernel(x_ref, gs_ref, sh_ref, wu_ref, cw_ref, cb_ref, wg_ref, gb_ref, lam_ref,
                    ff_ref, fr_ref, af_s, bf_s, ar_s, br_s):
    rows = x_ref.shape[0]
    npos = rows // 8
    xn = _norm_mod(x_ref[...], gs_ref[...], sh_ref[...]).astype(BF16)
    u = jnp.dot(xn, wu_ref[...], preferred_element_type=F32)
    pos = lax.broadcasted_iota(jnp.int32, u.shape, 0) // 8
    um1 = jnp.where(pos >= 1, pltpu.roll(u, 8, 0), 0.0)
    up1 = jnp.where(pos < npos - 1, pltpu.roll(u, rows - 8, 0), 0.0)
    up2 = jnp.where(pos < npos - 2, pltpu.roll(u, rows - 16, 0), 0.0)
    cw = cw_ref[...]
    xc = cw[0:1] * um1 + cw[1:2] * u + cw[2:3] * up1 + cw[3:4] * up2 + cb_ref[...]
    zs = _gate_matmul(xc, wg_ref)
    gbh = gb_ref[...]
    c2 = _lru_decay_scale(lam_ref[...])
    hx = 0.5 * xc
    a, b = _lru_coeffs(hx, zs, 0, gbh, c2[0:1])
    af_s[...] = a
    bf_s[...] = b
    a, b = _lru_coeffs(hx, zs, 1, gbh, c2[1:2])
    ar_s[...] = a
    br_s[...] = b

    def body(p, carry):
        hf, hr, pr = carry
        r0 = pl.multiple_of(p * 8, 8)
        hf = af_s[pl.ds(r0, 8), :] * hf + bf_s[pl.ds(r0, 8), :]
        hr = hr + pr * br_s[pl.ds(r0, 8), :]
        pr = pr * ar_s[pl.ds(r0, 8), :]
        return hf, hr, pr

    zero = jnp.zeros((8, u.shape[1]), F32)
    hf, hr, _ = lax.fori_loop(0, npos, body, (zero, zero, zero + 1.0))
    ff_ref[...] = hf
    fr_ref[...] = hr


def _lru_ctx(xp, gs, sh, wu, cw, cb, wg, gb, lam):
    rows, d = xp.shape
    e = wu.shape[1]
    nq = e // GATE_TILE
    full = lambda a: pl.BlockSpec(a.shape, lambda q, nd=a.ndim: (0,) * nd)
    cols = lambda a: pl.BlockSpec((a.shape[0], GATE_TILE), lambda q: (0, q))
    return pl.pallas_call(
        _lru_ctx_kernel,
        grid=(nq,),
        in_specs=[full(xp), full(gs), full(sh), cols(wu), cols(cw), cols(cb),
                  pl.BlockSpec((1,) + wg.shape[1:], lambda q: (q, 0, 0)), cols(gb), cols(lam)],
        out_specs=[pl.BlockSpec((8, GATE_TILE), lambda q: (0, q)), pl.BlockSpec((8, GATE_TILE), lambda q: (0, q))],
        out_shape=[jax.ShapeDtypeStruct((8, e), F32), jax.ShapeDtypeStruct((8, e), F32)],
        scratch_shapes=[pltpu.VMEM((rows, GATE_TILE), F32) for _ in range(4)],
        compiler_params=_cparams(("arbitrary",)),
    )(xp, gs, sh, wu, cw, cb, wg, gb, lam)


def _lru_pass_a_kernel(x_ref, gs_ref, sh_ref, wu_ref, wgl_ref, cw_ref, cb_ref, wg_ref, gb_ref, lam_ref,
                       lar_ref, br_ref, hl_ref, pc_ref, hg_ref, hs_ref, ps_ref, pr_ref, hr_ref,
                       ring_ref, gring_ref, xb_ref, xc_ref, xcb_ref, z_ref, *, nrow):
    i = pl.program_id(0)
    nb, w, d = x_ref.shape
    rows = nb * w
    nq = wg_ref.shape[0]
    gt = GATE_TILE

    @pl.when(i == 0)
    def _():
        ring_ref[...] = jnp.zeros_like(ring_ref)
        gring_ref[...] = jnp.zeros_like(gring_ref)

    @pl.when(i <= 3)
    def _():
        hs_ref[...] = jnp.zeros_like(hs_ref)
        ps_ref[...] = jnp.ones_like(ps_ref)
        hr_ref[...] = jnp.zeros_like(hr_ref)
        pr_ref[...] = jnp.ones_like(pr_ref)

    _norm_mod_rows(x_ref, gs_ref, sh_ref, xb_ref)
    off = jnp.where(i == 0, -1, jnp.where(i > nrow, 1, 0))
    r = lax.broadcasted_iota(jnp.int32, (rows, rows), 0)
    c = lax.broadcasted_iota(jnp.int32, (rows, rows), 1)
    mix = ((c == r + off) & (r // w == c // w)).astype(BF16)
    xs = jnp.dot(mix, xb_ref[...], preferred_element_type=F32).astype(BF16)
    ring_ref[i & 3] = jnp.dot(xs, wu_ref[...], preferred_element_type=F32)
    gring_ref[i & 3] = jnp.dot(xb_ref[...], wgl_ref[...], preferred_element_type=F32).astype(BF16)

    cw = cw_ref[...]
    cb = cb_ref[...]
    s0, s1, s2, s3 = (i - 3) & 3, (i - 2) & 3, (i - 1) & 3, i & 3
    for r0 in range(0, rows, LRU_RC):
        rs = slice(r0, r0 + LRU_RC)
        xc = (cw[0:1] * ring_ref[s0, rs, :] + cw[1:2] * ring_ref[s1, rs, :]
              + cw[2:3] * ring_ref[s2, rs, :] + cw[3:4] * ring_ref[s3, rs, :] + cb)
        xc_ref[rs, :] = xc
        xcb_ref[rs, :] = xc.astype(BF16)
    for q in range(nq):
        z_ref[q] = jnp.dot(xcb_ref[:, q * gt:(q + 1) * gt], wg_ref[q], preferred_element_type=F32)
    hg_ref[...] = gring_ref[(i - 2) & 3]
    gbh = gb_ref[...]
    c2 = _lru_decay_scale(lam_ref[...])
    for q in range(nq):
        cs = slice(q * gt, (q + 1) * gt)
        for r0 in range(0, rows, LRU_RC):
            rs = slice(r0, r0 + LRU_RC)
            hx = 0.5 * xc_ref[rs, cs]
            _, af, bf = _lru_gate_math(hx, z_ref[q, rs, 0:gt], z_ref[q, rs, gt:2 * gt],
                                       gbh[0:1, cs], gbh[1:2, cs], c2[0:1, cs])
            lar, ar, br = _lru_gate_math(hx, z_ref[q, rs, 2 * gt:3 * gt], z_ref[q, rs, 3 * gt:4 * gt],
                                         gbh[2:3, cs], gbh[3:4, cs], c2[1:2, cs])
            h = af * hs_ref[rs, cs] + bf
            p = ps_ref[rs, cs] * af
            hs_ref[rs, cs] = h
            ps_ref[rs, cs] = p
            hl_ref[rs, cs] = h.astype(BF16)
            pc_ref[rs, cs] = p.astype(BF16)
            lar_ref[rs, cs] = lar.astype(BF16)
            br_ref[rs, cs] = br.astype(BF16)
            prod = pr_ref[rs, cs]
            hr_ref[rs, cs] = hr_ref[rs, cs] + prod * br
            pr_ref[rs, cs] = prod * ar


def _lru_pass_a(x4, gs8, sh8, wu, wgl, cw, cb, wg, gb, lam):
    nb, nrow, w, d = x4.shape
    e = wu.shape[1]
    rows = nb * w
    kern = functools.partial(_lru_pass_a_kernel, nrow=nrow)
    const = lambda a: pl.BlockSpec(a.shape, lambda i, nd=a.ndim: (0,) * nd)
    acc = pl.BlockSpec((rows, e), lambda i: (0, 0))
    acc_shape = jax.ShapeDtypeStruct((rows, e), F32)
    per_row = pl.BlockSpec((None, rows, e), lambda i: (jnp.maximum(i - 3, 0), 0, 0))
    per_row_shape = jax.ShapeDtypeStruct((nrow, rows, e), BF16)
    return pl.pallas_call(
        kern,
        grid=(nrow + 3,),
        in_specs=[pl.BlockSpec((nb, None, w, d), lambda i: (0, (i + nrow - 1) % nrow, 0, 0)),
                  const(gs8), const(sh8), const(wu), const(wgl), const(cw), const(cb), const(wg), const(gb),
                  const(lam)],
        out_specs=[per_row] * 5 + [acc] * 4,
        out_shape=[per_row_shape] * 5 + [acc_shape] * 4,
        scratch_shapes=[pltpu.VMEM((4, rows, e), F32), pltpu.VMEM((4, rows, e), BF16),
                        pltpu.VMEM((rows, d), BF16), pltpu.VMEM((rows, e), F32), pltpu.VMEM((rows, e), BF16),
                        pltpu.VMEM((wg.shape[0], rows, wg.shape[2]), F32)],
        compiler_params=_cparams(("arbitrary",)),
        name="lru_pass_a",
    )(x4, gs8, sh8, wu, wgl, cw, cb, wg, gb, lam)


def _lru_stitch_kernel(pf_ref, hf_ref, pr_ref, hr_ref, ff_ref, fr_ref, sf_ref, sr_ref, *, w):
    rows = pf_ref.shape[0]
    col = lax.broadcasted_iota(jnp.int32, pf_ref.shape, 0) % w
    p, h = pf_ref[...], hf_ref[...]
    k = 1
    while k < w:
        keep = col >= k
        ps = jnp.where(keep, pltpu.roll(p, k, 0), 1.0)
        hs = jnp.where(keep, pltpu.roll(h, k, 0), 0.0)
        p, h = p * ps, p * hs + h
        k *= 2
    fin = ff_ref[...]
    sf_ref[...] = jnp.where(col == 0, fin, pltpu.roll(p, 1, 0) * fin + pltpu.roll(h, 1, 0))
    p, h = pr_ref[...], hr_ref[...]
    k = 1
    while k < w:
        keep = col < w - k
        ps = jnp.where(keep, pltpu.roll(p, rows - k, 0), 1.0)
        hs = jnp.where(keep, pltpu.roll(h, rows - k, 0), 0.0)
        p, h = p * ps, p * hs + h
        k *= 2
    fin = fr_ref[...]
    sr_ref[...] = jnp.where(col == w - 1, fin, pltpu.roll(p, rows - 1, 0) * fin + pltpu.roll(h, rows - 1, 0))


def _lru_stitch(pf, hf, pr, hr, ff_rows, fr_rows, w):
    rows, e = pf.shape
    big = pl.BlockSpec((rows, e), lambda i: (0, 0))
    shp = jax.ShapeDtypeStruct((rows, e), F32)
    return pl.pallas_call(
        functools.partial(_lru_stitch_kernel, w=w),
        grid=(1,),
        in_specs=[big] * 6,
        out_specs=[big, big],
        out_shape=[shp, shp],
        compiler_params=_cparams(("arbitrary",)),
        name="lru_stitch",
    )(pf, hf, pr, hr, ff_rows, fr_rows)


def _lru_rev_out_kernel(lar_ref, br_ref, hl_ref, pc_ref, hg_ref, x_ref, m5_ref, gt_ref, hsf_ref, hsr_ref,
                        wo_ref, fg_ref, o_ref, h_ref, mb_ref, mix_ref):
    i = pl.program_id(0)
    nb, w, d = x_ref.shape
    rows = nb * w
    e = hl_ref.shape[-1]

    @pl.when(i == 0)
    def _():
        h_ref[...] = hsr_ref[...]

    for c0 in range(0, e, GATE_TILE):
        cs = slice(c0, c0 + GATE_TILE)
        for r0 in range(0, rows, LRU_RC):
            rs = slice(r0, r0 + LRU_RC)
            h = jnp.exp2(lar_ref[rs, cs].astype(F32)) * h_ref[rs, cs] + br_ref[rs, cs].astype(F32)
            h_ref[rs, cs] = h
            y = hl_ref[rs, cs].astype(F32) + pc_ref[rs, cs].astype(F32) * hsf_ref[rs, cs] + h
            hg = hg_ref[rs, cs].astype(F32)
            mb_ref[rs, cs] = (y * (hg * jnp.tanh(hg) + hg)).astype(BF16)
    mix_ref[...] = (jnp.dot(m5_ref[...].reshape(rows, e), wo_ref[0:e, :], preferred_element_type=F32)
                    + jnp.dot(mb_ref[...], wo_ref[e:2 * e, :], preferred_element_type=F32))
    fg = fg_ref[...]
    for b in range(nb):
        for r0 in range(0, w, LRU_RC):
            res = x_ref[b, r0:r0 + LRU_RC, :] + gt_ref[b:b + 1] * mix_ref[b * w + r0:b * w + r0 + LRU_RC, :]
            ms = jnp.mean(res * res, axis=-1, keepdims=True)
            o_ref[b, r0:r0 + LRU_RC, :] = res * lax.rsqrt(ms + EPS) * fg


def _lru_rev_out(lar, br, hloc, pcum, hg, x4, m5, gt8, hs_f, hs_r, wo, fg):
    nb, nrow, w, d = x4.shape
    _, rows, e = lar.shape
    const = lambda a: pl.BlockSpec(a.shape, lambda i, nd=a.ndim: (0,) * nd)
    rev3 = pl.BlockSpec((None, rows, e), lambda i: (nrow - 1 - i, 0, 0))
    rev4 = pl.BlockSpec((nb, None, w, d), lambda i: (0, nrow - 1 - i, 0, 0))
    rev4e = pl.BlockSpec((nb, None, w, e), lambda i: (0, nrow - 1 - i, 0, 0))
    return pl.pallas_call(
        _lru_rev_out_kernel,
        grid=(nrow,),
        in_specs=[rev3] * 5 + [rev4, rev4e, const(gt8), const(hs_f), const(hs_r), const(wo), const(fg)],
        out_specs=rev4,
        out_shape=jax.ShapeDtypeStruct((nb, nrow, w, d), F32),
        scratch_shapes=[pltpu.VMEM((rows, e), F32), pltpu.VMEM((rows, e), BF16), pltpu.VMEM((rows, d), F32)],
        compiler_params=_cparams(("arbitrary",)),
        name="lru_rev_out",
    )(lar, br, hloc, pcum, hg, x4, m5, gt8, hs_f, hs_r, wo, fg)


def _gate_tiles(w_a, w_x):
    ndir, heads, hd, _ = w_a.shape
    per = GATE_TILE // hd
    nq = heads // per

    def tiles(w):
        wq = w.reshape(nq, per, hd, hd)
        eye = jnp.eye(per, dtype=w.dtype)
        blk = wq[:, :, :, None, :] * eye[None, :, None, :, None]
        return blk.reshape(nq, GATE_TILE, GATE_TILE)

    return jnp.concatenate([tiles(w_a[0]), tiles(w_x[0]), tiles(w_a[1]), tiles(w_x[1])], axis=2).astype(BF16)


def kernel(x, c, ctx, c_ctx, w_mod, b_mod, norm_g, w_in, s5_a_re, s5_a_im, s5_log_step, s5_b_re, s5_b_im,
           s5_c_re, s5_c_im, s5_d, s5_w_glu, s5_b_glu, lru_conv_w, lru_conv_b, lru_w_a, lru_b_a, lru_w_x,
           lru_b_x, lru_lam, w_out, final_g):
    bsz, seq, d = x.shape
    nctx_tok = ctx.shape[1]
    e = s5_w_glu.shape[-1]
    assert w_mod.shape[0] == 1, "single-layer block"
    assert bsz <= 8 and seq % S5_SEG == 0 and seq % GRID_W == 0 and nctx_tok % S5_T == 0
    nseg = seq // S5_SEG
    ns = bsz * nseg
    nctx = nctx_tok // S5_T
    assert nctx <= nseg
    nrow = seq // GRID_W

    c8 = jnp.zeros((8, d), F32).at[:bsz].set(c).at[bsz].set(c_ctx)
    mod = _modulation(c8, w_mod[0], b_mod[0])
    sh, sc, gt = mod[:, :d], mod[:, d:2 * d], mod[:, 2 * d:]
    gs = norm_g[0][None, :] * (1.0 + sc)

    w_in0 = w_in[0]
    wt_us5 = w_in0[:, :e].T.astype(BF16)
    w_gs5 = (0.5 * w_in0[:, e:2 * e]).astype(BF16)
    w_ulru = w_in0[:, 2 * e:3 * e].astype(BF16)
    w_glru = (0.5 * w_in0[:, 3 * e:]).astype(BF16)
    w_glu = (0.5 * s5_w_glu[0]).astype(BF16)
    wo = w_out[0].astype(BF16)

    s5_params = _s5_param_layout(s5_a_re[0], s5_a_im[0], s5_log_step[0], s5_b_re[0], s5_b_im[0],
                                 s5_c_re[0], s5_c_im[0], s5_d[0])
    x3 = x.reshape(ns, S5_SEG, d)
    xt, hg5 = _s5_inproj(x3, gs, sh, wt_us5, w_gs5, bsz)
    ctx_pad = jnp.zeros((bsz, nseg, S5_T, d), F32).at[:, :nctx].set(ctx.reshape(bsz, nctx, S5_T, d))
    gs_c = jnp.broadcast_to(gs[bsz:bsz + 1], (8, d))
    sh_c = jnp.broadcast_to(sh[bsz:bsz + 1], (8, d))
    xct, _ = _s5_inproj(ctx_pad.reshape(ns, S5_T, d), gs_c, sh_c, wt_us5, w_gs5, bsz)
    y5 = _s5_core(xt, xct, s5_params, nseg, nctx)
    nj = S5_SEG // S5_T
    m5 = _s5_post(y5.reshape(nj, S5_T, e, ns), hg5, w_glu, 0.5 * s5_b_glu[0].reshape(1, e))

    wg_all = _gate_tiles(0.5 * lru_w_a[0], 0.5 * lru_w_x[0])
    gb_all = 0.5 * jnp.stack([lru_b_a[0, 0], lru_b_x[0, 0], lru_b_a[0, 1], lru_b_x[0, 1]])
    cw = lru_conv_w[0]
    cb = lru_conv_b[0].reshape(1, e)
    lam = lru_lam[0]
    ctx_p = jnp.zeros((8, nctx_tok, d), F32).at[:bsz].set(ctx).transpose(1, 0, 2).reshape(nctx_tok * 8, d)
    ff, fr = _lru_ctx(ctx_p, gs[bsz:bsz + 1], sh[bsz:bsz + 1], w_ulru, cw, cb, wg_all, gb_all, lam)
    x4 = x.reshape(bsz, nrow, GRID_W, d)
    lar, br, hloc, pcum, hgl, hf, pf, pr, hr = _lru_pass_a(x4, gs, sh, w_ulru, w_glru, cw, cb, wg_all, gb_all, lam)
    hs_f, hs_r = _lru_stitch(pf, hf, pr, hr, jnp.repeat(ff[:bsz], GRID_W, axis=0),
                             jnp.repeat(fr[:bsz], GRID_W, axis=0), GRID_W)
    out4 = _lru_rev_out(lar, br, hloc, pcum, hgl, x4, m5.reshape(bsz, nrow, GRID_W, e), gt, hs_f, hs_r, wo,
                        final_g.reshape(1, d))
    return out4.reshape(bsz, seq, d)
```

```python
import functools
import math

import jax
import jax.numpy as jnp
from jax import lax
from jax.experimental import pallas as pl
from jax.experimental.pallas import tpu as pltpu

F32 = jnp.float32
BF16 = jnp.bfloat16

EPS = 1e-6
TINY = 1e-30
GRID_W = 64
LRU_C = 8.0
S5_H = 16
S5_N = 64
S5_T = 16
S5_SEG = 256
S5_TB = 8
S5_PB = 16
LRU_HEAD = 64
GATE_TILE = 256
LRU_LAG = 4
LRU_PK_LA, LRU_PK_BR, LRU_PK_HL, LRU_PK_PC, LRU_PK_HG, LRU_PK_N = 0, 1, 2, 3, 4, 5
LRU_RC = 16
VMEM_LIMIT = 56 * 1024 * 1024


def _cparams(sem):
    return pltpu.CompilerParams(dimension_semantics=sem, vmem_limit_bytes=VMEM_LIMIT)


def _sigmoid(z):
    return 0.5 * (jnp.tanh(0.5 * z) + 1.0)


def _gelu_tanh(y):
    return 0.5 * y * (1.0 + jnp.tanh(math.sqrt(2.0 / math.pi) * (y + 0.044715 * (y * y * y))))


def _norm_mod(x, gs, sh):
    ms = jnp.mean(x * x, axis=-1, keepdims=True)
    return x * lax.rsqrt(ms + EPS) * gs + sh


def _mod_kernel(c_ref, w_ref, b_ref, o_ref):
    c = c_ref[...]
    s = c * _sigmoid(c)
    o_ref[...] = jnp.dot(s, w_ref[...], preferred_element_type=F32) + b_ref[...]


def _modulation(c8, w_mod, b_mod):
    d = c8.shape[1]
    n = w_mod.shape[1]
    nb = n // d
    return pl.pallas_call(
        _mod_kernel,
        grid=(nb,),
        in_specs=[pl.BlockSpec((8, d), lambda i: (0, 0)),
                  pl.BlockSpec((d, d), lambda i: (0, i)),
                  pl.BlockSpec((1, d), lambda i: (0, i))],
        out_specs=pl.BlockSpec((8, d), lambda i: (0, i)),
        out_shape=jax.ShapeDtypeStruct((8, n), F32),
        compiler_params=_cparams(("arbitrary",)),
    )(c8, w_mod, b_mod.reshape(1, n))


def _perm_matrix(n_a, n_b):
    n = n_a * n_b
    r = lax.broadcasted_iota(jnp.int32, (n, n), 0)
    c = lax.broadcasted_iota(jnp.int32, (n, n), 1)
    return ((r // n_b == c % n_a) & (r % n_b == c // n_a)).astype(BF16)


def _s5_inproj_kernel(x_ref, gs_ref, sh_ref, wt_ref, wg_ref, o_ref, hg_ref, xs_ref, *, nbatch):
    ns, tb, d = x_ref.shape
    groups = o_ref.shape[0]
    rows_b = (ns // nbatch) * tb
    blk = S5_PB * tb
    perm = _perm_matrix(tb, S5_PB)
    x = x_ref[...].reshape(ns * tb, d)
    xn = jnp.concatenate(
        [_norm_mod(x[b * rows_b:(b + 1) * rows_b], gs_ref[b:b + 1], sh_ref[b:b + 1]).astype(BF16)
         for b in range(nbatch)], axis=0)
    for k in range(ns // S5_PB):
        q = jnp.dot(perm, xn[k * blk:(k + 1) * blk], preferred_element_type=F32).astype(BF16)
        for t in range(tb):
            xs_ref[t * ns + k * S5_PB:t * ns + (k + 1) * S5_PB, :] = q[t * S5_PB:(t + 1) * S5_PB]
    wt = wt_ref[...]
    for t in range(0, tb, 2):
        ut = lax.dot_general(wt, xs_ref[t * ns:(t + 2) * ns, :], (((1,), (1,)), ((), ())),
                             preferred_element_type=F32)
        o_ref[:, t * S5_H:(t + 1) * S5_H, :] = ut[:, :ns].reshape(groups, S5_H, ns).astype(BF16)
        o_ref[:, (t + 1) * S5_H:(t + 2) * S5_H, :] = ut[:, ns:].reshape(groups, S5_H, ns).astype(BF16)
    hg = jnp.dot(xs_ref[...], wg_ref[...], preferred_element_type=F32)
    hg_ref[...] = hg.astype(BF16).reshape(tb, ns, hg.shape[1])


def _s5_inproj(x3, gs8, sh8, wt, wg, nbatch):
    ns, seg, d = x3.shape
    e = wt.shape[0]
    groups = e // S5_H
    nj = seg // S5_T
    per = S5_T // S5_TB
    return pl.pallas_call(
        functools.partial(_s5_inproj_kernel, nbatch=nbatch),
        grid=(seg // S5_TB,),
        in_specs=[pl.BlockSpec((ns, S5_TB, d), lambda i: (0, i, 0)),
                  pl.BlockSpec((8, d), lambda i: (0, 0)),
                  pl.BlockSpec((8, d), lambda i: (0, 0)),
                  pl.BlockSpec((e, d), lambda i: (0, 0)),
                  pl.BlockSpec((d, e), lambda i: (0, 0))],
        out_specs=[pl.BlockSpec((groups, S5_TB * S5_H, ns), lambda i: (0, i % per, i // per)),
                   pl.BlockSpec((S5_TB, ns, e), lambda i: (i, 0, 0))],
        out_shape=[jax.ShapeDtypeStruct((groups, S5_T * S5_H, nj * ns), BF16),
                   jax.ShapeDtypeStruct((seg, ns, e), BF16)],
        scratch_shapes=[pltpu.VMEM((S5_TB * ns, d), BF16)],
        compiler_params=_cparams(("arbitrary",)),
        name="s5_inproj",
    )(x3, gs8, sh8, wt, wg)


def _cmul_add(dr, di, hr, hi, sr, si):
    return dr * hr - di * hi + sr, dr * hi + di * hr + si


def _lane_scan(vr, vi, dr, di, slot, count, reverse):
    n = vr.shape[1]
    k = 1
    while k < count:
        if reverse:
            sr = pltpu.roll(vr, n - k, 1)
            si = pltpu.roll(vi, n - k, 1)
            keep = slot < count - k
        else:
            sr = pltpu.roll(vr, k, 1)
            si = pltpu.roll(vi, k, 1)
            keep = slot >= k
        sr = jnp.where(keep, sr, 0.0)
        si = jnp.where(keep, si, 0.0)
        vr, vi = _cmul_add(dr, di, sr, si, vr, vi)
        dr, di = dr * dr - di * di, 2.0 * dr * di
        k *= 2
    return vr, vi


def _csq(r, i):
    return r * r - i * i, 2.0 * r * i


def _cpow_bits(e, pows):
    pr = jnp.ones(e.shape, F32)
    pi = jnp.zeros(e.shape, F32)
    for bit, (qr, qi) in pows:
        on = (e & bit) != 0
        pr, pi = jnp.where(on, pr * qr - pi * qi, pr), jnp.where(on, pr * qi + pi * qr, pi)
    return pr, pi


def _s5_discretize(ar, ai, ls):
    step = jnp.exp(ls)
    mag = jnp.exp(ar * step)
    pr = mag * jnp.cos(ai * step)
    pi = mag * jnp.sin(ai * step)
    den = ar * ar + ai * ai
    nr = pr - 1.0
    return pr, pi, (nr * ar + pi * ai) / den, (pi * ar - nr * ai) / den


def _s5_operators(pcol_ref, prow_ref, bt_ref, ct1_ref, ct2_ref, dsk_ref):
    n, t_len, th = S5_N, S5_T, S5_T * S5_H
    hp = lax.Precision.HIGHEST
    lane_t = lax.broadcasted_iota(jnp.int32, (1, th), 1) // S5_H
    bits = [1 << b for b in range(t_len.bit_length() - 1)]

    blocks, bbs, decay = [], [], []
    for d in range(2):
        pr, pi, cr, ci = _s5_discretize(pcol_ref[3 * d], pcol_ref[3 * d + 1], pcol_ref[3 * d + 2])
        pows, q = [], (pr, pi)
        for bit in bits:
            pows.append((bit, q))
            q = _csq(*q)
        decay.append(q)
        btr = bt_ref[2 * d * n:(2 * d + 1) * n, :]
        bti = bt_ref[(2 * d + 1) * n:(2 * d + 2) * n, :]
        bbr = cr * btr - ci * bti
        bbi = cr * bti + ci * btr
        e = (t_len - 1 - lane_t) if d == 0 else lane_t
        wr, wi = _cpow_bits(jnp.broadcast_to(e, (n, th)), pows)
        blocks += [wr * bbr - wi * bbi, wr * bbi + wi * bbr]
        bbs.append(jnp.concatenate([bbr, bbi], axis=0))
    bpow = jnp.concatenate(blocks, axis=0).astype(BF16)

    qr, qi, _, _ = _s5_discretize(prow_ref[0:1], prow_ref[1:2], prow_ref[2:3])
    pows, q = [], (qr, qi)
    for bit in bits:
        pows.append((bit, q))
        q = _csq(*q)
    q_t = q
    row_t = lax.broadcasted_iota(jnp.int32, (th, 1), 0) // S5_H
    is_f = lax.broadcasted_iota(jnp.int32, (1, 4 * n), 1) < 2 * n
    e_k = jnp.where(is_f, row_t, (t_len - row_t) % t_len)
    kr, ki = _cpow_bits(e_k, pows)
    nr = jnp.where(is_f, kr * qr - ki * qi, jnp.where(row_t == 0, q_t[0], kr))
    ni = jnp.where(is_f, kr * qi + ki * qr, jnp.where(row_t == 0, q_t[1], ki))
    ct1 = jnp.concatenate([ct1_ref[...]] * t_len, axis=0)
    ct2 = jnp.concatenate([ct2_ref[...]] * t_len, axis=0)
    cpow = (ct1 * nr + ct2 * ni).astype(BF16)
    qk = ct1 * kr + ct2 * ki
    mf = jnp.dot(qk[:, :2 * n], bbs[0], precision=hp, preferred_element_type=F32)
    mr = jnp.dot(qk[:, 2 * n:], bbs[1], precision=hp, preferred_element_type=F32)
    m = (jnp.where(row_t <= t_len - 1 - lane_t, mf, 0.0)
         + jnp.where((row_t == 0) | (row_t >= t_len - lane_t), mr, 0.0))
    for bit in bits:
        m = jnp.where((lane_t & bit) != 0, pltpu.roll(m, S5_H * bit, 0), m)
    row = lax.broadcasted_iota(jnp.int32, (th, th), 0)
    col = lax.broadcasted_iota(jnp.int32, (th, th), 1)
    toep = (m + jnp.where(row == col, dsk_ref[...], 0.0)).astype(BF16)
    return toep, bpow, cpow, decay


def _s5_core_kernel(xt_ref, xc_ref, pcol_ref, pp_ref, bt_ref, o_ref, s_ref, *, nseg, nctx):
    n = S5_N
    h = S5_H
    ns = xc_ref.shape[-1]
    nj = xt_ref.shape[-1] // ns
    toep, bpow, cpow, decay = _s5_operators(pcol_ref, pp_ref.at[2 * h:2 * h + 3], bt_ref, pp_ref.at[0:h],
                                            pp_ref.at[h:2 * h], pp_ref.at[2 * h + 3:2 * h + 4])
    dfr = jnp.broadcast_to(decay[0][0], (n, ns))
    dfi = jnp.broadcast_to(decay[0][1], (n, ns))
    drr = jnp.broadcast_to(decay[1][0], (n, ns))
    dri = jnp.broadcast_to(decay[1][1], (n, ns))
    slot = lax.broadcasted_iota(jnp.int32, (n, ns), 1) % nseg

    sc = jnp.dot(bpow, xc_ref[...], preferred_element_type=F32)
    valid = slot < nctx
    cfr, cfi = _lane_scan(jnp.where(valid, sc[0:n], 0.0), jnp.where(valid, sc[n:2 * n], 0.0),
                          dfr, dfi, slot, nseg, False)
    crr, cri = _lane_scan(jnp.where(valid, sc[2 * n:3 * n], 0.0), jnp.where(valid, sc[3 * n:4 * n], 0.0),
                          drr, dri, slot, nseg, True)

    step = 2 if nj % 2 == 0 else 1
    for j in range(0, nj, step):
        cols = slice(j * ns, (j + step) * ns)
        s_ref[:, cols] = jnp.dot(bpow, xt_ref[:, cols], preferred_element_type=F32)

    zero = jnp.zeros((n, ns), F32)
    efr, efi, err, eri = zero, zero, zero, zero
    for j in range(nj):
        cols = slice(j * ns, (j + 1) * ns)
        efr, efi = _cmul_add(dfr, dfi, efr, efi, s_ref[0:n, cols], s_ref[n:2 * n, cols])
    for j in range(nj - 1, -1, -1):
        cols = slice(j * ns, (j + 1) * ns)
        err, eri = _cmul_add(drr, dri, err, eri, s_ref[2 * n:3 * n, cols], s_ref[3 * n:4 * n, cols])

    sfr, sfi, srr, sri = dfr, dfi, drr, dri
    k = 1
    while k < nj:
        sfr, sfi = sfr * sfr - sfi * sfi, 2.0 * sfr * sfi
        srr, sri = srr * srr - sri * sri, 2.0 * srr * sri
        k *= 2

    first = slot == 0
    last = slot == nseg - 1
    vfr = jnp.where(first, pltpu.roll(cfr, ns - (nctx - 1), 1) if nctx > 1 else cfr, pltpu.roll(efr, 1, 1))
    vfi = jnp.where(first, pltpu.roll(cfi, ns - (nctx - 1), 1) if nctx > 1 else cfi, pltpu.roll(efi, 1, 1))
    vrr = jnp.where(last, pltpu.roll(crr, nseg - 1, 1) if nseg > 1 else crr, pltpu.roll(err, ns - 1, 1))
    vri = jnp.where(last, pltpu.roll(cri, nseg - 1, 1) if nseg > 1 else cri, pltpu.roll(eri, ns - 1, 1))
    hfr, hfi = _lane_scan(vfr, vfi, sfr, sfi, slot, nseg, False)
    hrr, hri = _lane_scan(vrr, vri, srr, sri, slot, nseg, True)

    for j in range(nj):
        cols = slice(j * ns, (j + 1) * ns)
        nr, ni = _cmul_add(dfr, dfi, hfr, hfi, s_ref[0:n, cols], s_ref[n:2 * n, cols])
        s_ref[0:n, cols] = hfr
        s_ref[n:2 * n, cols] = hfi
        hfr, hfi = nr, ni
    for j in range(nj - 1, -1, -1):
        cols = slice(j * ns, (j + 1) * ns)
        nr, ni = _cmul_add(drr, dri, hrr, hri, s_ref[2 * n:3 * n, cols], s_ref[3 * n:4 * n, cols])
        s_ref[2 * n:3 * n, cols] = hrr
        s_ref[3 * n:4 * n, cols] = hri
        hrr, hri = nr, ni

    for j in range(0, nj, step):
        cols = slice(j * ns, (j + step) * ns)
        y = (jnp.dot(toep, xt_ref[:, cols], preferred_element_type=F32)
             + jnp.dot(cpow, s_ref[:, cols].astype(BF16), preferred_element_type=F32))
        for q in range(step):
            o_ref[j + q] = y[:, q * ns:(q + 1) * ns].reshape(S5_T, S5_H, ns)


def _s5_core(xt, xct, params, nseg, nctx):
    groups, rows, lanes = xt.shape
    ns = xct.shape[-1]
    nj = lanes // ns
    kern = functools.partial(_s5_core_kernel, nseg=nseg, nctx=nctx)
    per_group = lambda a: pl.BlockSpec((None,) + a.shape[1:], lambda g, nd=a.ndim: (g,) + (0,) * (nd - 1))
    return pl.pallas_call(
        kern,
        grid=(groups,),
        in_specs=[per_group(xt), per_group(xct)] + [per_group(a) for a in params],
        out_specs=pl.BlockSpec((nj, S5_T, None, S5_H, ns), lambda g: (0, 0, g, 0, 0)),
        out_shape=jax.ShapeDtypeStruct((nj, S5_T, groups, S5_H, ns), F32),
        scratch_shapes=[pltpu.VMEM((rows, lanes), F32)],
        compiler_params=_cparams(("arbitrary",)),
        name="s5_core",
    )(xt, xct, *params)


def _s5_param_layout(a_re, a_im, log_step, b_re, b_im, c_re, c_im, d_skip):
    ndir, groups, n = a_re.shape
    h = b_re.shape[-1]
    ls = jnp.broadcast_to(log_step[..., None], a_re.shape)
    pcol = jnp.stack([a_re[0], a_im[0], ls[0], a_re[1], a_im[1], ls[1]], axis=1)[..., None]
    lay = lambda v: jnp.concatenate([v[0], v[0], v[1], v[1]], axis=-1)
    prow = jnp.stack([lay(a_re), lay(a_im), lay(ls)], axis=1)
    tile_t = lambda v: jnp.tile(v, (1, 1, S5_T))
    bt = jnp.concatenate([tile_t(b_re[0]), tile_t(b_im[0]), tile_t(b_re[1]), tile_t(b_im[1])], axis=1)
    ct1 = jnp.concatenate([c_re[0], -c_im[0], c_re[1], -c_im[1]], axis=-1)
    ct2 = jnp.concatenate([-c_im[0], -c_re[0], -c_im[1], -c_re[1]], axis=-1)
    dsk = jnp.tile(d_skip.reshape(groups, 1, h), (1, 1, S5_T))
    pad = jnp.zeros((groups, (-(2 * h + 4)) % 8, S5_T * h), F32)
    pp = jnp.concatenate([ct1, ct2, prow, dsk, pad], axis=1)
    return tuple(v.astype(F32) for v in (pcol, pp, bt))


def _s5_post_kernel(y_ref, hg_ref, wglu_ref, bglu_ref, o_ref, m_ref):
    q = pl.program_id(1)
    tb, e, ns = y_ref.shape
    ys = jnp.concatenate([y_ref[t].T for t in range(tb)], axis=0)
    yg = _gelu_tanh(ys)
    zh = jnp.dot(yg.astype(BF16), wglu_ref[...], preferred_element_type=F32) + bglu_ref[...]
    hg = hg_ref[...].reshape(tb * ns, e).astype(F32)
    m = ((yg * (0.5 * jnp.tanh(zh) + 0.5)) * (hg * jnp.tanh(hg) + hg)).astype(BF16)
    nq = S5_T // tb
    for half in range(nq):
        @pl.when(q == half)
        def _(half=half):
            m_ref[half * tb * ns:(half + 1) * tb * ns, :] = m

    @pl.when(q == nq - 1)
    def _():
        perm = _perm_matrix(S5_PB, S5_T)
        for k in range(ns // S5_PB):
            g = jnp.concatenate([m_ref[t * ns + k * S5_PB:t * ns + (k + 1) * S5_PB, :] for t in range(S5_T)], axis=0)
            r = jnp.dot(perm, g, preferred_element_type=F32).astype(BF16)
            o_ref[k * S5_PB:(k + 1) * S5_PB] = r.reshape(S5_PB, S5_T, e)


def _s5_post(y4, hg, wglu, bglu):
    nj, t_all, e, ns = y4.shape
    nt = t_all // S5_TB
    return pl.pallas_call(
        _s5_post_kernel,
        grid=(nj, nt),
        in_specs=[pl.BlockSpec((None, S5_TB, e, ns), lambda j, q: (j, q, 0, 0)),
                  pl.BlockSpec((S5_TB, ns, e), lambda j, q: (j * nt + q, 0, 0)),
                  pl.BlockSpec((e, e), lambda j, q: (0, 0)),
                  pl.BlockSpec((1, e), lambda j, q: (0, 0))],
        out_specs=pl.BlockSpec((ns, S5_T, e), lambda j, q: (0, j, 0)),
        out_shape=jax.ShapeDtypeStruct((ns, nj * S5_T, e), BF16),
        scratch_shapes=[pltpu.VMEM((S5_T * ns, e), BF16)],
        compiler_params=_cparams(("arbitrary", "arbitrary")),
        name="s5_post",
    )(y4, hg, wglu, bglu)


def _softplus(z):
    return jnp.maximum(z, 0.0) + jnp.log1p(jnp.exp(-jnp.abs(z)))


def _gate_matmul(xc, wg_ref):
    xb = xc.astype(BF16)
    nq = wg_ref.shape[0]
    return [jnp.dot(xb[:, q * GATE_TILE:(q + 1) * GATE_TILE], wg_ref[q], preferred_element_type=F32)
            for q in range(nq)]


def _gate_cols(zs, k):
    return jnp.concatenate([z[:, k * GATE_TILE:(k + 1) * GATE_TILE] for z in zs], axis=1)


def _lru_decay_scale(lam_row):
    return (-0.5 * LRU_C * math.log2(math.e)) * _softplus(-lam_row)


def _lru_gate_math(hx, za, zx, ba, bx, c2):
    ta = jnp.tanh(za + ba)
    tx = jnp.tanh(zx + bx)
    la = ta * c2 + c2
    a = jnp.exp2(la)
    om = 1.0 - a * a
    return la, a, (om * lax.rsqrt(jnp.maximum(om, TINY))) * (tx * hx + hx)


def _lru_coeffs(hx, zs, k, gbh, c2):
    return _lru_gate_math(hx, _gate_cols(zs, 2 * k), _gate_cols(zs, 2 * k + 1),
                          gbh[2 * k:2 * k + 1], gbh[2 * k + 1:2 * k + 2], c2)[1:]


def _norm_mod_rows(x_ref, gs_ref, sh_ref, dst_ref):
    nb, w, _ = x_ref.shape
    for b in range(nb):
        dst_ref[b * w:(b + 1) * w, :] = _norm_mod(x_ref[b], gs_ref[b:b + 1], sh_ref[b:b + 1]).astype(BF16)


def _lru_ctx_kernel(x_ref, gs_ref, sh_ref, wu_ref, cw_ref, cb_ref, wg_ref, gb_ref, lam_ref,
                    ff_ref, fr_ref, af_s, bf_s, ar_s, br_s):
    rows = x_ref.shape[0]
    npos = rows // 8
    xn = _norm_mod(x_ref[...], gs_ref[...], sh_ref[...]).astype(BF16)
    u = jnp.dot(xn, wu_ref[...], preferred_element_type=F32)
    pos = lax.broadcasted_iota(jnp.int32, u.shape, 0) // 8
    um1 = jnp.where(pos >= 1, pltpu.roll(u, 8, 0), 0.0)
    up1 = jnp.where(pos < npos - 1, pltpu.roll(u, rows - 8, 0), 0.0)
    up2 = jnp.where(pos < npos - 2, pltpu.roll(u, rows - 16, 0), 0.0)
    cw = cw_ref[...]
    xc = cw[0:1] * um1 + cw[1:2] * u + cw[2:3] * up1 + cw[3:4] * up2 + cb_ref[...]
    zs = _gate_matmul(xc, wg_ref)
    gbh = gb_ref[...]
    c2 = _lru_decay_scale(lam_ref[...])
    hx = 0.5 * xc
    a, b = _lru_coeffs(hx, zs, 0, gbh, c2[0:1])
    af_s[...] = a
    bf_s[...] = b
    a, b = _lru_coeffs(hx, zs, 1, gbh, c2[1:2])
    ar_s[...] = a
    br_s[...] = b

    def body(p, carry):
        hf, hr, pr = carry
        r0 = pl.multiple_of(p * 8, 8)
        hf = af_s[pl.ds(r0, 8), :] * hf + bf_s[pl.ds(r0, 8), :]
        hr = hr + pr * br_s[pl.ds(r0, 8), :]
        pr = pr * ar_s[pl.ds(r0, 8), :]
        return hf, hr, pr

    zero = jnp.zeros((8, u.shape[1]), F32)
    hf, hr, _ = lax.fori_loop(0, npos, body, (zero, zero, zero + 1.0))
    ff_ref[...] = hf
    fr_ref[...] = hr


def _lru_ctx(xp, gs, sh, wu, cw, cb, wg, gb, lam):
    rows, d = xp.shape
    e = wu.shape[1]
    nq = e // GATE_TILE
    full = lambda a: pl.BlockSpec(a.shape, lambda q, nd=a.ndim: (0,) * nd)
    cols = lambda a: pl.BlockSpec((a.shape[0], GATE_TILE), lambda q: (0, q))
    return pl.pallas_call(
        _lru_ctx_kernel,
        grid=(nq,),
        in_specs=[full(xp), full(gs), full(sh), cols(wu), cols(cw), cols(cb),
                  pl.BlockSpec((1,) + wg.shape[1:], lambda q: (q, 0, 0)), cols(gb), cols(lam)],
        out_specs=[pl.BlockSpec((8, GATE_TILE), lambda q: (0, q)), pl.BlockSpec((8, GATE_TILE), lambda q: (0, q))],
        out_shape=[jax.ShapeDtypeStruct((8, e), F32), jax.ShapeDtypeStruct((8, e), F32)],
        scratch_shapes=[pltpu.VMEM((rows, GATE_TILE), F32) for _ in range(4)],
        compiler_params=_cparams(("arbitrary",)),
    )(xp, gs, sh, wu, cw, cb, wg, gb, lam)


def _lru_pass_a_kernel(x_ref, gs_ref, sh_ref, wu_ref, wgl_ref, cw_ref, cb_ref, wg_ref, gb_ref, lam_ref,
                       pk_ref, hs_ref, ps_ref, pr_ref, hr_ref,
                       ring_ref, gring_ref, xb_ref, xs_ref, xc_ref, xcb_ref, z_ref, *, nrow):
    i = pl.program_id(0)
    nb, w, d = x_ref.shape
    rows = nb * w
    e = xc_ref.shape[-1]
    nq = wg_ref.shape[0]
    gt = GATE_TILE

    @pl.when(i == 0)
    def _():
        ring_ref[...] = jnp.zeros_like(ring_ref)
        gring_ref[...] = jnp.zeros_like(gring_ref)

    @pl.when(i <= LRU_LAG)
    def _():
        hs_ref[...] = jnp.zeros_like(hs_ref)
        ps_ref[...] = jnp.ones_like(ps_ref)
        hr_ref[...] = jnp.zeros_like(hr_ref)
        pr_ref[...] = jnp.ones_like(pr_ref)

    cw = cw_ref[...]
    cb = cb_ref[...]
    s0, s1, s2, s3 = (i - 4) & 7, (i - 3) & 7, (i - 2) & 7, (i - 1) & 7
    for r0 in range(0, rows, LRU_RC):
        rs = slice(r0, r0 + LRU_RC)
        xc = (cw[0:1] * ring_ref[s0, rs, :] + cw[1:2] * ring_ref[s1, rs, :]
              + cw[2:3] * ring_ref[s2, rs, :] + cw[3:4] * ring_ref[s3, rs, :] + cb)
        xc_ref[rs, :] = xc
        xcb_ref[rs, :] = xc.astype(BF16)
    pk_ref[:, LRU_PK_HG * e:(LRU_PK_HG + 1) * e] = gring_ref[(i - 3) & 3]

    _norm_mod_rows(x_ref, gs_ref, sh_ref, xb_ref)
    off = jnp.where(i == 0, -1, jnp.where(i > nrow, 1, 0))
    r = lax.broadcasted_iota(jnp.int32, (rows, rows), 0)
    c = lax.broadcasted_iota(jnp.int32, (rows, rows), 1)
    mix = ((c == r + off) & (r // w == c // w)).astype(BF16)
    xs_ref[...] = jnp.dot(mix, xb_ref[...], preferred_element_type=F32).astype(BF16)
    ring_ref[i & 7] = jnp.dot(xs_ref[...], wu_ref[...], preferred_element_type=F32)
    gring_ref[i & 3] = jnp.dot(xb_ref[...], wgl_ref[...], preferred_element_type=F32).astype(BF16)

    for q in range(nq):
        z_ref[q] = jnp.dot(xcb_ref[:, q * gt:(q + 1) * gt], wg_ref[q], preferred_element_type=F32)
    gbh = gb_ref[...]
    c2 = _lru_decay_scale(lam_ref[...])
    for q in range(nq):
        cs = slice(q * gt, (q + 1) * gt)
        for r0 in range(0, rows, LRU_RC):
            rs = slice(r0, r0 + LRU_RC)
            hx = 0.5 * xc_ref[rs, cs]
            _, af, bf = _lru_gate_math(hx, z_ref[q, rs, 0:gt], z_ref[q, rs, gt:2 * gt],
                                       gbh[0:1, cs], gbh[1:2, cs], c2[0:1, cs])
            lar, ar, br = _lru_gate_math(hx, z_ref[q, rs, 2 * gt:3 * gt], z_ref[q, rs, 3 * gt:4 * gt],
                                         gbh[2:3, cs], gbh[3:4, cs], c2[1:2, cs])
            h = af * hs_ref[rs, cs] + bf
            p = ps_ref[rs, cs] * af
            hs_ref[rs, cs] = h
            ps_ref[rs, cs] = p
            pk_ref[rs, LRU_PK_HL * e + q * gt:LRU_PK_HL * e + (q + 1) * gt] = h.astype(BF16)
            pk_ref[rs, LRU_PK_PC * e + q * gt:LRU_PK_PC * e + (q + 1) * gt] = p.astype(BF16)
            pk_ref[rs, LRU_PK_LA * e + q * gt:LRU_PK_LA * e + (q + 1) * gt] = lar.astype(BF16)
            pk_ref[rs, LRU_PK_BR * e + q * gt:LRU_PK_BR * e + (q + 1) * gt] = br.astype(BF16)
            prod = pr_ref[rs, cs]
            hr_ref[rs, cs] = hr_ref[rs, cs] + prod * br
            pr_ref[rs, cs] = prod * ar


def _lru_pass_a(x4, gs8, sh8, wu, wgl, cw, cb, wg, gb, lam):
    nb, nrow, w, d = x4.shape
    e = wu.shape[1]
    rows = nb * w
    kern = functools.partial(_lru_pass_a_kernel, nrow=nrow)
    const = lambda a: pl.BlockSpec(a.shape, lambda i, nd=a.ndim: (0,) * nd)
    acc = pl.BlockSpec((rows, e), lambda i: (0, 0))
    acc_shape = jax.ShapeDtypeStruct((rows, e), F32)
    per_row = pl.BlockSpec((None, rows, LRU_PK_N * e), lambda i: (jnp.maximum(i - LRU_LAG, 0), 0, 0))
    per_row_shape = jax.ShapeDtypeStruct((nrow, rows, LRU_PK_N * e), BF16)
    return pl.pallas_call(
        kern,
        grid=(nrow + LRU_LAG,),
        in_specs=[pl.BlockSpec((nb, None, w, d), lambda i: (0, (i + nrow - 1) % nrow, 0, 0)),
                  const(gs8), const(sh8), const(wu), const(wgl), const(cw), const(cb), const(wg), const(gb),
                  const(lam)],
        out_specs=[per_row] + [acc] * 4,
        out_shape=[per_row_shape] + [acc_shape] * 4,
        scratch_shapes=[pltpu.VMEM((8, rows, e), F32), pltpu.VMEM((4, rows, e), BF16),
                        pltpu.VMEM((rows, d), BF16), pltpu.VMEM((rows, d), BF16), pltpu.VMEM((rows, e), F32),
                        pltpu.VMEM((rows, e), BF16), pltpu.VMEM((wg.shape[0], rows, wg.shape[2]), F32)],
        compiler_params=_cparams(("arbitrary",)),
        name="lru_pass_a",
    )(x4, gs8, sh8, wu, wgl, cw, cb, wg, gb, lam)


def _lru_stitch_kernel(pf_ref, hf_ref, pr_ref, hr_ref, ff_ref, fr_ref, sf_ref, sr_ref, *, w):
    rows = pf_ref.shape[0]
    col = lax.broadcasted_iota(jnp.int32, pf_ref.shape, 0) % w
    p, h = pf_ref[...], hf_ref[...]
    k = 1
    while k < w:
        keep = col >= k
        ps = jnp.where(keep, pltpu.roll(p, k, 0), 1.0)
        hs = jnp.where(keep, pltpu.roll(h, k, 0), 0.0)
        p, h = p * ps, p * hs + h
        k *= 2
    fin = ff_ref[...]
    sf_ref[...] = jnp.where(col == 0, fin, pltpu.roll(p, 1, 0) * fin + pltpu.roll(h, 1, 0))
    p, h = pr_ref[...], hr_ref[...]
    k = 1
    while k < w:
        keep = col < w - k
        ps = jnp.where(keep, pltpu.roll(p, rows - k, 0), 1.0)
        hs = jnp.where(keep, pltpu.roll(h, rows - k, 0), 0.0)
        p, h = p * ps, p * hs + h
        k *= 2
    fin = fr_ref[...]
    sr_ref[...] = jnp.where(col == w - 1, fin, pltpu.roll(p, rows - 1, 0) * fin + pltpu.roll(h, rows - 1, 0))


def _lru_stitch(pf, hf, pr, hr, ff_rows, fr_rows, w):
    rows, e = pf.shape
    big = pl.BlockSpec((rows, e), lambda i: (0, 0))
    shp = jax.ShapeDtypeStruct((rows, e), F32)
    return pl.pallas_call(
        functools.partial(_lru_stitch_kernel, w=w),
        grid=(1,),
        in_specs=[big] * 6,
        out_specs=[big, big],
        out_shape=[shp, shp],
        compiler_params=_cparams(("arbitrary",)),
        name="lru_stitch",
    )(pf, hf, pr, hr, ff_rows, fr_rows)


def _lru_rev_out_kernel(pk_ref, x_ref, m5_ref, gt_ref, hsf_ref, hsr_ref, wo_ref, fg_ref, o_ref, h_ref, mb_ref,
                        mix_ref):
    i = pl.program_id(0)
    nb, w, d = x_ref.shape
    rows = nb * w
    e = h_ref.shape[-1]

    @pl.when(i == 0)
    def _():
        h_ref[...] = hsr_ref[...]

    def packed(k, rs, c0):
        return pk_ref[rs, k * e + c0:k * e + c0 + GATE_TILE].astype(F32)

    for c0 in range(0, e, GATE_TILE):
        cs = slice(c0, c0 + GATE_TILE)
        for r0 in range(0, rows, LRU_RC):
            rs = slice(r0, r0 + LRU_RC)
            h = jnp.exp2(packed(LRU_PK_LA, rs, c0)) * h_ref[rs, cs] + packed(LRU_PK_BR, rs, c0)
            h_ref[rs, cs] = h
            y = packed(LRU_PK_HL, rs, c0) + packed(LRU_PK_PC, rs, c0) * hsf_ref[rs, cs] + h
            hg = packed(LRU_PK_HG, rs, c0)
            mb_ref[rs, cs] = (y * (hg * jnp.tanh(hg) + hg)).astype(BF16)
    mix_ref[...] = (jnp.dot(m5_ref[...].reshape(rows, e), wo_ref[0:e, :], preferred_element_type=F32)
                    + jnp.dot(mb_ref[...], wo_ref[e:2 * e, :], preferred_element_type=F32))
    fg = fg_ref[...]
    for b in range(nb):
        for r0 in range(0, w, LRU_RC):
            res = x_ref[b, r0:r0 + LRU_RC, :] + gt_ref[b:b + 1] * mix_ref[b * w + r0:b * w + r0 + LRU_RC, :]
            ms = jnp.mean(res * res, axis=-1, keepdims=True)
            o_ref[b, r0:r0 + LRU_RC, :] = res * lax.rsqrt(ms + EPS) * fg


def _lru_rev_out(packed, x4, m5, gt8, hs_f, hs_r, wo, fg):
    nb, nrow, w, d = x4.shape
    rows, e = hs_f.shape
    const = lambda a: pl.BlockSpec(a.shape, lambda i, nd=a.ndim: (0,) * nd)
    rev3 = pl.BlockSpec((None, rows, packed.shape[-1]), lambda i: (nrow - 1 - i, 0, 0))
    rev4 = pl.BlockSpec((nb, None, w, d), lambda i: (0, nrow - 1 - i, 0, 0))
    rev4e = pl.BlockSpec((nb, None, w, e), lambda i: (0, nrow - 1 - i, 0, 0))
    return pl.pallas_call(
        _lru_rev_out_kernel,
        grid=(nrow,),
        in_specs=[rev3, rev4, rev4e, const(gt8), const(hs_f), const(hs_r), const(wo), const(fg)],
        out_specs=rev4,
        out_shape=jax.ShapeDtypeStruct((nb, nrow, w, d), F32),
        scratch_shapes=[pltpu.VMEM((rows, e), F32), pltpu.VMEM((rows, e), BF16), pltpu.VMEM((rows, d), F32)],
        compiler_params=_cparams(("arbitrary",)),
        name="lru_rev_out",
    )(packed, x4, m5, gt8, hs_f, hs_r, wo, fg)


def _gate_tiles(w_a, w_x):
    ndir, heads, hd, _ = w_a.shape
    per = GATE_TILE // hd
    nq = heads // per

    def tiles(w):
        wq = w.reshape(nq, per, hd, hd)
        eye = jnp.eye(per, dtype=w.dtype)
        blk = wq[:, :, :, None, :] * eye[None, :, None, :, None]
        return blk.reshape(nq, GATE_TILE, GATE_TILE)

    return jnp.concatenate([tiles(w_a[0]), tiles(w_x[0]), tiles(w_a[1]), tiles(w_x[1])], axis=2).astype(BF16)


def kernel(x, c, ctx, c_ctx, w_mod, b_mod, norm_g, w_in, s5_a_re, s5_a_im, s5_log_step, s5_b_re, s5_b_im,
           s5_c_re, s5_c_im, s5_d, s5_w_glu, s5_b_glu, lru_conv_w, lru_conv_b, lru_w_a, lru_b_a, lru_w_x,
           lru_b_x, lru_lam, w_out, final_g):
    bsz, seq, d = x.shape
    nctx_tok = ctx.shape[1]
    e = s5_w_glu.shape[-1]
    assert w_mod.shape[0] == 1, "single-layer block"
    assert bsz <= 8 and seq % S5_SEG == 0 and seq % GRID_W == 0 and nctx_tok % S5_T == 0
    nseg = seq // S5_SEG
    ns = bsz * nseg
    nctx = nctx_tok // S5_T
    assert nctx <= nseg
    nrow = seq // GRID_W

    c8 = jnp.zeros((8, d), F32).at[:bsz].set(c).at[bsz].set(c_ctx)
    mod = _modulation(c8, w_mod[0], b_mod[0])
    sh, sc, gt = mod[:, :d], mod[:, d:2 * d], mod[:, 2 * d:]
    gs = norm_g[0][None, :] * (1.0 + sc)

    w_in0 = w_in[0]
    wt_us5 = w_in0[:, :e].T.astype(BF16)
    w_gs5 = (0.5 * w_in0[:, e:2 * e]).astype(BF16)
    w_ulru = w_in0[:, 2 * e:3 * e].astype(BF16)
    w_glru = (0.5 * w_in0[:, 3 * e:]).astype(BF16)
    w_glu = (0.5 * s5_w_glu[0]).astype(BF16)
    wo = w_out[0].astype(BF16)

    s5_params = _s5_param_layout(s5_a_re[0], s5_a_im[0], s5_log_step[0], s5_b_re[0], s5_b_im[0],
                                 s5_c_re[0], s5_c_im[0], s5_d[0])
    x3 = x.reshape(ns, S5_SEG, d)
    xt, hg5 = _s5_inproj(x3, gs, sh, wt_us5, w_gs5, bsz)
    ctx_pad = jnp.zeros((bsz, nseg, S5_T, d), F32).at[:, :nctx].set(ctx.reshape(bsz, nctx, S5_T, d))
    gs_c = jnp.broadcast_to(gs[bsz:bsz + 1], (8, d))
    sh_c = jnp.broadcast_to(sh[bsz:bsz + 1], (8, d))
    xct, _ = _s5_inproj(ctx_pad.reshape(ns, S5_T, d), gs_c, sh_c, wt_us5, w_gs5, bsz)
    y5 = _s5_core(xt, xct, s5_params, nseg, nctx)
    nj = S5_SEG // S5_T
    m5 = _s5_post(y5.reshape(nj, S5_T, e, ns), hg5, w_glu, 0.5 * s5_b_glu[0].reshape(1, e))

    wg_all = _gate_tiles(0.5 * lru_w_a[0], 0.5 * lru_w_x[0])
    gb_all = 0.5 * jnp.stack([lru_b_a[0, 0], lru_b_x[0, 0], lru_b_a[0, 1], lru_b_x[0, 1]])
    cw = lru_conv_w[0]
    cb = lru_conv_b[0].reshape(1, e)
    lam = lru_lam[0]
    ctx_p = jnp.zeros((8, nctx_tok, d), F32).at[:bsz].set(ctx).transpose(1, 0, 2).reshape(nctx_tok * 8, d)
    ff, fr = _lru_ctx(ctx_p, gs[bsz:bsz + 1], sh[bsz:bsz + 1], w_ulru, cw, cb, wg_all, gb_all, lam)
    x4 = x.reshape(bsz, nrow, GRID_W, d)
    packed, hf, pf, pr, hr = _lru_pass_a(x4, gs, sh, w_ulru, w_glru, cw, cb, wg_all, gb_all, lam)
    hs_f, hs_r = _lru_stitch(pf, hf, pr, hr, jnp.repeat(ff[:bsz], GRID_W, axis=0),
                             jnp.repeat(fr[:bsz], GRID_W, axis=0), GRID_W)
    out4 = _lru_rev_out(packed, x4, m5.reshape(bsz, nrow, GRID_W, e), gt, hs_f, hs_r, wo, final_g.reshape(1, d))
    return out4.reshape(bsz, seq, d)
```

```python
import functools
import math

import jax
import jax.numpy as jnp
from jax import lax
from jax.experimental import pallas as pl
from jax.experimental.pallas import tpu as pltpu

F32 = jnp.float32
BF16 = jnp.bfloat16

EPS = 1e-6
TINY = 1e-30
GRID_W = 64
LRU_C = 8.0
S5_H = 16
S5_N = 64
S5_T = 16
S5_SEG = 256
S5_TB = 8
S5_PB = 16
LRU_HEAD = 64
GATE_TILE = 256
LRU_LAG = 3
LRU_RC = 16
VMEM_LIMIT = 56 * 1024 * 1024


def _cparams(sem):
    return pltpu.CompilerParams(dimension_semantics=sem, vmem_limit_bytes=VMEM_LIMIT)


def _sigmoid(z):
    return 0.5 * (jnp.tanh(0.5 * z) + 1.0)


def _gelu_tanh(y):
    return 0.5 * y * (1.0 + jnp.tanh(math.sqrt(2.0 / math.pi) * (y + 0.044715 * (y * y * y))))


def _norm_mod(x, gs, sh):
    ms = jnp.mean(x * x, axis=-1, keepdims=True)
    return x * lax.rsqrt(ms + EPS) * gs + sh


def _mod_kernel(c_ref, w_ref, b_ref, o_ref):
    c = c_ref[...]
    s = c * _sigmoid(c)
    o_ref[...] = jnp.dot(s, w_ref[...], preferred_element_type=F32) + b_ref[...]


def _modulation(c8, w_mod, b_mod):
    d = c8.shape[1]
    n = w_mod.shape[1]
    nb = n // d
    return pl.pallas_call(
        _mod_kernel,
        grid=(nb,),
        in_specs=[pl.BlockSpec((8, d), lambda i: (0, 0)),
                  pl.BlockSpec((d, d), lambda i: (0, i)),
                  pl.BlockSpec((1, d), lambda i: (0, i))],
        out_specs=pl.BlockSpec((8, d), lambda i: (0, i)),
        out_shape=jax.ShapeDtypeStruct((8, n), F32),
        compiler_params=_cparams(("arbitrary",)),
    )(c8, w_mod, b_mod.reshape(1, n))


def _perm_matrix(n_a, n_b):
    n = n_a * n_b
    r = lax.broadcasted_iota(jnp.int32, (n, n), 0)
    c = lax.broadcasted_iota(jnp.int32, (n, n), 1)
    return ((r // n_b == c % n_a) & (r % n_b == c // n_a)).astype(BF16)


def _s5_inproj_kernel(x_ref, gs_ref, sh_ref, wt_ref, wg_ref, o_ref, hg_ref, xs_ref, *, nbatch):
    ns, tb, d = x_ref.shape
    groups = o_ref.shape[0]
    rows_b = (ns // nbatch) * tb
    blk = S5_PB * tb
    perm = _perm_matrix(tb, S5_PB)
    x = x_ref[...].reshape(ns * tb, d)
    xn = jnp.concatenate(
        [_norm_mod(x[b * rows_b:(b + 1) * rows_b], gs_ref[b:b + 1], sh_ref[b:b + 1]).astype(BF16)
         for b in range(nbatch)], axis=0)
    for k in range(ns // S5_PB):
        q = jnp.dot(perm, xn[k * blk:(k + 1) * blk], preferred_element_type=F32).astype(BF16)
        for t in range(tb):
            xs_ref[t * ns + k * S5_PB:t * ns + (k + 1) * S5_PB, :] = q[t * S5_PB:(t + 1) * S5_PB]
    wt = wt_ref[...]
    for t in range(0, tb, 2):
        ut = lax.dot_general(wt, xs_ref[t * ns:(t + 2) * ns, :], (((1,), (1,)), ((), ())),
                             preferred_element_type=F32)
        o_ref[:, t * S5_H:(t + 1) * S5_H, :] = ut[:, :ns].reshape(groups, S5_H, ns).astype(BF16)
        o_ref[:, (t + 1) * S5_H:(t + 2) * S5_H, :] = ut[:, ns:].reshape(groups, S5_H, ns).astype(BF16)
    hg = jnp.dot(xs_ref[...], wg_ref[...], preferred_element_type=F32)
    hg_ref[...] = hg.astype(BF16).reshape(tb, ns, hg.shape[1])


def _s5_inproj(x3, gs8, sh8, wt, wg, nbatch):
    ns, seg, d = x3.shape
    e = wt.shape[0]
    groups = e // S5_H
    nj = seg // S5_T
    per = S5_T // S5_TB
    return pl.pallas_call(
        functools.partial(_s5_inproj_kernel, nbatch=nbatch),
        grid=(seg // S5_TB,),
        in_specs=[pl.BlockSpec((ns, S5_TB, d), lambda i: (0, i, 0)),
                  pl.BlockSpec((8, d), lambda i: (0, 0)),
                  pl.BlockSpec((8, d), lambda i: (0, 0)),
                  pl.BlockSpec((e, d), lambda i: (0, 0)),
                  pl.BlockSpec((d, e), lambda i: (0, 0))],
        out_specs=[pl.BlockSpec((groups, S5_TB * S5_H, ns), lambda i: (0, i % per, i // per)),
                   pl.BlockSpec((S5_TB, ns, e), lambda i: (i, 0, 0))],
        out_shape=[jax.ShapeDtypeStruct((groups, S5_T * S5_H, nj * ns), BF16),
                   jax.ShapeDtypeStruct((seg, ns, e), BF16)],
        scratch_shapes=[pltpu.VMEM((S5_TB * ns, d), BF16)],
        compiler_params=_cparams(("arbitrary",)),
        name="s5_inproj",
    )(x3, gs8, sh8, wt, wg)


def _cmul_add(dr, di, hr, hi, sr, si):
    return dr * hr - di * hi + sr, dr * hi + di * hr + si


def _lane_scan(vr, vi, dr, di, slot, count, reverse):
    n = vr.shape[1]
    k = 1
    while k < count:
        if reverse:
            sr = pltpu.roll(vr, n - k, 1)
            si = pltpu.roll(vi, n - k, 1)
            keep = slot < count - k
        else:
            sr = pltpu.roll(vr, k, 1)
            si = pltpu.roll(vi, k, 1)
            keep = slot >= k
        sr = jnp.where(keep, sr, 0.0)
        si = jnp.where(keep, si, 0.0)
        vr, vi = _cmul_add(dr, di, sr, si, vr, vi)
        dr, di = dr * dr - di * di, 2.0 * dr * di
        k *= 2
    return vr, vi


def _csq(r, i):
    return r * r - i * i, 2.0 * r * i


def _cpow_bits(e, pows):
    pr = jnp.ones(e.shape, F32)
    pi = jnp.zeros(e.shape, F32)
    for bit, (qr, qi) in pows:
        on = (e & bit) != 0
        pr, pi = jnp.where(on, pr * qr - pi * qi, pr), jnp.where(on, pr * qi + pi * qr, pi)
    return pr, pi


def _s5_discretize(ar, ai, ls):
    step = jnp.exp(ls)
    mag = jnp.exp(ar * step)
    pr = mag * jnp.cos(ai * step)
    pi = mag * jnp.sin(ai * step)
    den = ar * ar + ai * ai
    nr = pr - 1.0
    return pr, pi, (nr * ar + pi * ai) / den, (pi * ar - nr * ai) / den


def _s5_operators(pcol_ref, prow_ref, bt_ref, ct1_ref, ct2_ref, dsk_ref):
    n, t_len, th = S5_N, S5_T, S5_T * S5_H
    hp = lax.Precision.HIGHEST
    lane_t = lax.broadcasted_iota(jnp.int32, (1, th), 1) // S5_H
    bits = [1 << b for b in range(t_len.bit_length() - 1)]

    blocks, bbs, decay = [], [], []
    for d in range(2):
        pr, pi, cr, ci = _s5_discretize(pcol_ref[3 * d], pcol_ref[3 * d + 1], pcol_ref[3 * d + 2])
        pows, q = [], (pr, pi)
        for bit in bits:
            pows.append((bit, q))
            q = _csq(*q)
        decay.append(q)
        btr = bt_ref[2 * d * n:(2 * d + 1) * n, :]
        bti = bt_ref[(2 * d + 1) * n:(2 * d + 2) * n, :]
        bbr = cr * btr - ci * bti
        bbi = cr * bti + ci * btr
        e = (t_len - 1 - lane_t) if d == 0 else lane_t
        wr, wi = _cpow_bits(jnp.broadcast_to(e, (n, th)), pows)
        blocks += [wr * bbr - wi * bbi, wr * bbi + wi * bbr]
        bbs.append(jnp.concatenate([bbr, bbi], axis=0))
    bpow = jnp.concatenate(blocks, axis=0).astype(BF16)

    qr, qi, _, _ = _s5_discretize(prow_ref[0:1], prow_ref[1:2], prow_ref[2:3])
    pows, q = [], (qr, qi)
    for bit in bits:
        pows.append((bit, q))
        q = _csq(*q)
    q_t = q
    row_t = lax.broadcasted_iota(jnp.int32, (th, 1), 0) // S5_H
    is_f = lax.broadcasted_iota(jnp.int32, (1, 4 * n), 1) < 2 * n
    e_k = jnp.where(is_f, row_t, (t_len - row_t) % t_len)
    kr, ki = _cpow_bits(e_k, pows)
    nr = jnp.where(is_f, kr * qr - ki * qi, jnp.where(row_t == 0, q_t[0], kr))
    ni = jnp.where(is_f, kr * qi + ki * qr, jnp.where(row_t == 0, q_t[1], ki))
    ct1 = jnp.concatenate([ct1_ref[...]] * t_len, axis=0)
    ct2 = jnp.concatenate([ct2_ref[...]] * t_len, axis=0)
    cpow = (ct1 * nr + ct2 * ni).astype(BF16)
    qk = ct1 * kr + ct2 * ki
    mf = jnp.dot(qk[:, :2 * n], bbs[0], precision=hp, preferred_element_type=F32)
    mr = jnp.dot(qk[:, 2 * n:], bbs[1], precision=hp, preferred_element_type=F32)
    m = (jnp.where(row_t <= t_len - 1 - lane_t, mf, 0.0)
         + jnp.where((row_t == 0) | (row_t >= t_len - lane_t), mr, 0.0))
    for bit in bits:
        m = jnp.where((lane_t & bit) != 0, pltpu.roll(m, S5_H * bit, 0), m)
    row = lax.broadcasted_iota(jnp.int32, (th, th), 0)
    col = lax.broadcasted_iota(jnp.int32, (th, th), 1)
    toep = (m + jnp.where(row == col, dsk_ref[...], 0.0)).astype(BF16)
    return toep, bpow, cpow, decay


def _s5_core_kernel(xt_ref, xc_ref, pcol_ref, pp_ref, bt_ref, o_ref, s_ref, *, nseg, nctx):
    n = S5_N
    h = S5_H
    ns = xc_ref.shape[-1]
    nj = xt_ref.shape[-1] // ns
    toep, bpow, cpow, decay = _s5_operators(pcol_ref, pp_ref.at[2 * h:2 * h + 3], bt_ref, pp_ref.at[0:h],
                                            pp_ref.at[h:2 * h], pp_ref.at[2 * h + 3:2 * h + 4])
    dfr = jnp.broadcast_to(decay[0][0], (n, ns))
    dfi = jnp.broadcast_to(decay[0][1], (n, ns))
    drr = jnp.broadcast_to(decay[1][0], (n, ns))
    dri = jnp.broadcast_to(decay[1][1], (n, ns))
    slot = lax.broadcasted_iota(jnp.int32, (n, ns), 1) % nseg

    sc = jnp.dot(bpow, xc_ref[...], preferred_element_type=F32)
    valid = slot < nctx
    cfr, cfi = _lane_scan(jnp.where(valid, sc[0:n], 0.0), jnp.where(valid, sc[n:2 * n], 0.0),
                          dfr, dfi, slot, nseg, False)
    crr, cri = _lane_scan(jnp.where(valid, sc[2 * n:3 * n], 0.0), jnp.where(valid, sc[3 * n:4 * n], 0.0),
                          drr, dri, slot, nseg, True)

    step = 2 if nj % 2 == 0 else 1
    for j in range(0, nj, step):
        cols = slice(j * ns, (j + step) * ns)
        s_ref[:, cols] = jnp.dot(bpow, xt_ref[:, cols], preferred_element_type=F32)

    zero = jnp.zeros((n, ns), F32)
    efr, efi, err, eri = zero, zero, zero, zero
    for j in range(nj):
        cols = slice(j * ns, (j + 1) * ns)
        efr, efi = _cmul_add(dfr, dfi, efr, efi, s_ref[0:n, cols], s_ref[n:2 * n, cols])
    for j in range(nj - 1, -1, -1):
        cols = slice(j * ns, (j + 1) * ns)
        err, eri = _cmul_add(drr, dri, err, eri, s_ref[2 * n:3 * n, cols], s_ref[3 * n:4 * n, cols])

    sfr, sfi, srr, sri = dfr, dfi, drr, dri
    k = 1
    while k < nj:
        sfr, sfi = sfr * sfr - sfi * sfi, 2.0 * sfr * sfi
        srr, sri = srr * srr - sri * sri, 2.0 * srr * sri
        k *= 2

    first = slot == 0
    last = slot == nseg - 1
    vfr = jnp.where(first, pltpu.roll(cfr, ns - (nctx - 1), 1) if nctx > 1 else cfr, pltpu.roll(efr, 1, 1))
    vfi = jnp.where(first, pltpu.roll(cfi, ns - (nctx - 1), 1) if nctx > 1 else cfi, pltpu.roll(efi, 1, 1))
    vrr = jnp.where(last, pltpu.roll(crr, nseg - 1, 1) if nseg > 1 else crr, pltpu.roll(err, ns - 1, 1))
    vri = jnp.where(last, pltpu.roll(cri, nseg - 1, 1) if nseg > 1 else cri, pltpu.roll(eri, ns - 1, 1))
    hfr, hfi = _lane_scan(vfr, vfi, sfr, sfi, slot, nseg, False)
    hrr, hri = _lane_scan(vrr, vri, srr, sri, slot, nseg, True)

    for j in range(nj):
        cols = slice(j * ns, (j + 1) * ns)
        nr, ni = _cmul_add(dfr, dfi, hfr, hfi, s_ref[0:n, cols], s_ref[n:2 * n, cols])
        s_ref[0:n, cols] = hfr
        s_ref[n:2 * n, cols] = hfi
        hfr, hfi = nr, ni
    for j in range(nj - 1, -1, -1):
        cols = slice(j * ns, (j + 1) * ns)
        nr, ni = _cmul_add(drr, dri, hrr, hri, s_ref[2 * n:3 * n, cols], s_ref[3 * n:4 * n, cols])
        s_ref[2 * n:3 * n, cols] = hrr
        s_ref[3 * n:4 * n, cols] = hri
        hrr, hri = nr, ni

    for j in range(0, nj, step):
        cols = slice(j * ns, (j + step) * ns)
        y = (jnp.dot(toep, xt_ref[:, cols], preferred_element_type=F32)
             + jnp.dot(cpow, s_ref[:, cols].astype(BF16), preferred_element_type=F32))
        for q in range(step):
            o_ref[j + q] = y[:, q * ns:(q + 1) * ns].reshape(S5_T, S5_H, ns).astype(BF16)


def _s5_core(xt, xct, params, nseg, nctx):
    groups, rows, lanes = xt.shape
    ns = xct.shape[-1]
    nj = lanes // ns
    kern = functools.partial(_s5_core_kernel, nseg=nseg, nctx=nctx)
    per_group = lambda a: pl.BlockSpec((None,) + a.shape[1:], lambda g, nd=a.ndim: (g,) + (0,) * (nd - 1))
    return pl.pallas_call(
        kern,
        grid=(groups,),
        in_specs=[per_group(xt), per_group(xct)] + [per_group(a) for a in params],
        out_specs=pl.BlockSpec((nj, S5_T, None, S5_H, ns), lambda g: (0, 0, g, 0, 0)),
        out_shape=jax.ShapeDtypeStruct((nj, S5_T, groups, S5_H, ns), BF16),
        scratch_shapes=[pltpu.VMEM((rows, lanes), F32)],
        compiler_params=_cparams(("arbitrary",)),
        name="s5_core",
    )(xt, xct, *params)


def _s5_param_layout(a_re, a_im, log_step, b_re, b_im, c_re, c_im, d_skip):
    ndir, groups, n = a_re.shape
    h = b_re.shape[-1]
    ls = jnp.broadcast_to(log_step[..., None], a_re.shape)
    pcol = jnp.stack([a_re[0], a_im[0], ls[0], a_re[1], a_im[1], ls[1]], axis=1)[..., None]
    lay = lambda v: jnp.concatenate([v[0], v[0], v[1], v[1]], axis=-1)
    prow = jnp.stack([lay(a_re), lay(a_im), lay(ls)], axis=1)
    tile_t = lambda v: jnp.tile(v, (1, 1, S5_T))
    bt = jnp.concatenate([tile_t(b_re[0]), tile_t(b_im[0]), tile_t(b_re[1]), tile_t(b_im[1])], axis=1)
    ct1 = jnp.concatenate([c_re[0], -c_im[0], c_re[1], -c_im[1]], axis=-1)
    ct2 = jnp.concatenate([-c_im[0], -c_re[0], -c_im[1], -c_re[1]], axis=-1)
    dsk = jnp.tile(d_skip.reshape(groups, 1, h), (1, 1, S5_T))
    pad = jnp.zeros((groups, (-(2 * h + 4)) % 8, S5_T * h), F32)
    pp = jnp.concatenate([ct1, ct2, prow, dsk, pad], axis=1)
    return tuple(v.astype(F32) for v in (pcol, pp, bt))


def _s5_post_kernel(y_ref, hg_ref, wglu_ref, bglu_ref, o_ref, m_ref):
    q = pl.program_id(1)
    tb, e, ns = y_ref.shape
    ys = jnp.concatenate([y_ref[t].astype(F32).T for t in range(tb)], axis=0)
    yg = _gelu_tanh(ys)
    zh = jnp.dot(yg.astype(BF16), wglu_ref[...], preferred_element_type=F32) + bglu_ref[...]
    hg = hg_ref[...].reshape(tb * ns, e).astype(F32)
    m = ((yg * (0.5 * jnp.tanh(zh) + 0.5)) * (hg * jnp.tanh(hg) + hg)).astype(BF16)
    nq = S5_T // tb
    for half in range(nq):
        @pl.when(q == half)
        def _(half=half):
            m_ref[half * tb * ns:(half + 1) * tb * ns, :] = m

    @pl.when(q == nq - 1)
    def _():
        perm = _perm_matrix(S5_PB, S5_T)
        for k in range(ns // S5_PB):
            g = jnp.concatenate([m_ref[t * ns + k * S5_PB:t * ns + (k + 1) * S5_PB, :] for t in range(S5_T)], axis=0)
            r = jnp.dot(perm, g, preferred_element_type=F32).astype(BF16)
            o_ref[k * S5_PB:(k + 1) * S5_PB] = r.reshape(S5_PB, S5_T, e)


def _s5_post(y4, hg, wglu, bglu):
    nj, t_all, e, ns = y4.shape
    nt = t_all // S5_TB
    return pl.pallas_call(
        _s5_post_kernel,
        grid=(nj, nt),
        in_specs=[pl.BlockSpec((None, S5_TB, e, ns), lambda j, q: (j, q, 0, 0)),
                  pl.BlockSpec((S5_TB, ns, e), lambda j, q: (j * nt + q, 0, 0)),
                  pl.BlockSpec((e, e), lambda j, q: (0, 0)),
                  pl.BlockSpec((1, e), lambda j, q: (0, 0))],
        out_specs=pl.BlockSpec((ns, S5_T, e), lambda j, q: (0, j, 0)),
        out_shape=jax.ShapeDtypeStruct((ns, nj * S5_T, e), BF16),
        scratch_shapes=[pltpu.VMEM((S5_T * ns, e), BF16)],
        compiler_params=_cparams(("arbitrary", "arbitrary")),
        name="s5_post",
    )(y4, hg, wglu, bglu)


def _softplus(z):
    return jnp.maximum(z, 0.0) + jnp.log1p(jnp.exp(-jnp.abs(z)))


def _gate_matmul(xc, wg_ref):
    xb = xc.astype(BF16)
    nq = wg_ref.shape[0]
    return [jnp.dot(xb[:, q * GATE_TILE:(q + 1) * GATE_TILE], wg_ref[q], preferred_element_type=F32)
            for q in range(nq)]


def _gate_cols(zs, k):
    return jnp.concatenate([z[:, k * GATE_TILE:(k + 1) * GATE_TILE] for z in zs], axis=1)


def _lru_decay_scale(lam_row):
    return (-0.5 * LRU_C * math.log2(math.e)) * _softplus(-lam_row)


def _lru_gate_math(hx, za, zx, ba, bx, c2):
    ta = jnp.tanh(za + ba)
    tx = jnp.tanh(zx + bx)
    la = ta * c2 + c2
    a = jnp.exp2(la)
    om = 1.0 - a * a
    return la, a, (om * lax.rsqrt(jnp.maximum(om, TINY))) * (tx * hx + hx)


def _lru_coeffs(hx, zs, k, gbh, c2):
    return _lru_gate_math(hx, _gate_cols(zs, 2 * k), _gate_cols(zs, 2 * k + 1),
                          gbh[2 * k:2 * k + 1], gbh[2 * k + 1:2 * k + 2], c2)[1:]


def _norm_mod_rows(x_ref, gs_ref, sh_ref, dst_ref):
    nb, w, _ = x_ref.shape
    for b in range(nb):
        dst_ref[b * w:(b + 1) * w, :] = _norm_mod(x_ref[b], gs_ref[b:b + 1], sh_ref[b:b + 1]).astype(BF16)


def _lru_ctx_kernel(x_ref, gs_ref, sh_ref, wu_ref, cw_ref, cb_ref, wg_ref, gb_ref, lam_ref,
                    ff_ref, fr_ref, af_s, bf_s, ar_s, br_s):
    rows = x_ref.shape[0]
    npos = rows // 8
    xn = _norm_mod(x_ref[...], gs_ref[...], sh_ref[...]).astype(BF16)
    u = jnp.dot(xn, wu_ref[...], preferred_element_type=F32)
    pos = lax.broadcasted_iota(jnp.int32, u.shape, 0) // 8
    um1 = jnp.where(pos >= 1, pltpu.roll(u, 8, 0), 0.0)
    up1 = jnp.where(pos < npos - 1, pltpu.roll(u, rows - 8, 0), 0.0)
    up2 = jnp.where(pos < npos - 2, pltpu.roll(u, rows - 16, 0), 0.0)
    cw = cw_ref[...]
    xc = cw[0:1] * um1 + cw[1:2] * u + cw[2:3] * up1 + cw[3:4] * up2 + cb_ref[...]
    zs = _gate_matmul(xc, wg_ref)
    gbh = gb_ref[...]
    c2 = _lru_decay_scale(lam_ref[...])
    hx = 0.5 * xc
    a, b = _lru_coeffs(hx, zs, 0, gbh, c2[0:1])
    af_s[...] = a
    bf_s[...] = b
    a, b = _lru_coeffs(hx, zs, 1, gbh, c2[1:2])
    ar_s[...] = a
    br_s[...] = b

    def body(p, carry):
        hf, hr, pr = carry
        r0 = pl.multiple_of(p * 8, 8)
        hf = af_s[pl.ds(r0, 8), :] * hf + bf_s[pl.ds(r0, 8), :]
        hr = hr + pr * br_s[pl.ds(r0, 8), :]
        pr = pr * ar_s[pl.ds(r0, 8), :]
        return hf, hr, pr

    zero = jnp.zeros((8, u.shape[1]), F32)
    hf, hr, _ = lax.fori_loop(0, npos, body, (zero, zero, zero + 1.0))
    ff_ref[...] = hf
    fr_ref[...] = hr


def _lru_ctx(xp, gs, sh, wu, cw, cb, wg, gb, lam):
    rows, d = xp.shape
    e = wu.shape[1]
    nq = e // GATE_TILE
    full = lambda a: pl.BlockSpec(a.shape, lambda q, nd=a.ndim: (0,) * nd)
    cols = lambda a: pl.BlockSpec((a.shape[0], GATE_TILE), lambda q: (0, q))
    return pl.pallas_call(
        _lru_ctx_kernel,
        grid=(nq,),
        in_specs=[full(xp), full(gs), full(sh), cols(wu), cols(cw), cols(cb),
                  pl.BlockSpec((1,) + wg.shape[1:], lambda q: (q, 0, 0)), cols(gb), cols(lam)],
        out_specs=[pl.BlockSpec((8, GATE_TILE), lambda q: (0, q)), pl.BlockSpec((8, GATE_TILE), lambda q: (0, q))],
        out_shape=[jax.ShapeDtypeStruct((8, e), F32), jax.ShapeDtypeStruct((8, e), F32)],
        scratch_shapes=[pltpu.VMEM((rows, GATE_TILE), F32) for _ in range(4)],
        compiler_params=_cparams(("arbitrary",)),
    )(xp, gs, sh, wu, cw, cb, wg, gb, lam)


def _lru_pass_a_kernel(x_ref, gs_ref, sh_ref, wu_ref, cw_ref, cb_ref, wg_ref, gb_ref, lam_ref,
                       lar_ref, br_ref, hl_ref, pc_ref, hs_ref, ps_ref, pr_ref, hr_ref,
                       ring_ref, xb_ref, xc_ref, xcb_ref, z_ref, *, nrow):
    i = pl.program_id(0)
    nb, w, d = x_ref.shape
    rows = nb * w
    nq = wg_ref.shape[0]
    gt = GATE_TILE

    @pl.when(i == 0)
    def _():
        ring_ref[...] = jnp.zeros_like(ring_ref)

    @pl.when(i <= LRU_LAG)
    def _():
        hs_ref[...] = jnp.zeros_like(hs_ref)
        ps_ref[...] = jnp.ones_like(ps_ref)
        hr_ref[...] = jnp.zeros_like(hr_ref)
        pr_ref[...] = jnp.ones_like(pr_ref)

    _norm_mod_rows(x_ref, gs_ref, sh_ref, xb_ref)
    off = jnp.where(i == 0, -1, jnp.where(i > nrow, 1, 0))
    r = lax.broadcasted_iota(jnp.int32, (rows, rows), 0)
    c = lax.broadcasted_iota(jnp.int32, (rows, rows), 1)
    mix = ((c == r + off) & (r // w == c // w)).astype(BF16)
    xs = jnp.dot(mix, xb_ref[...], preferred_element_type=F32).astype(BF16)
    ring_ref[i & 3] = jnp.dot(xs, wu_ref[...], preferred_element_type=F32)

    cw = cw_ref[...]
    cb = cb_ref[...]
    s0, s1, s2, s3 = (i - 3) & 3, (i - 2) & 3, (i - 1) & 3, i & 3
    for r0 in range(0, rows, LRU_RC):
        rs = slice(r0, r0 + LRU_RC)
        xc = (cw[0:1] * ring_ref[s0, rs, :] + cw[1:2] * ring_ref[s1, rs, :]
              + cw[2:3] * ring_ref[s2, rs, :] + cw[3:4] * ring_ref[s3, rs, :] + cb)
        xc_ref[rs, :] = xc
        xcb_ref[rs, :] = xc.astype(BF16)
    for q in range(nq):
        z_ref[q] = jnp.dot(xcb_ref[:, q * gt:(q + 1) * gt], wg_ref[q], preferred_element_type=F32)
    gbh = gb_ref[...]
    c2 = _lru_decay_scale(lam_ref[...])
    for q in range(nq):
        cs = slice(q * gt, (q + 1) * gt)
        for r0 in range(0, rows, LRU_RC):
            rs = slice(r0, r0 + LRU_RC)
            hx = 0.5 * xc_ref[rs, cs]
            _, af, bf = _lru_gate_math(hx, z_ref[q, rs, 0:gt], z_ref[q, rs, gt:2 * gt],
                                       gbh[0:1, cs], gbh[1:2, cs], c2[0:1, cs])
            lar, ar, br = _lru_gate_math(hx, z_ref[q, rs, 2 * gt:3 * gt], z_ref[q, rs, 3 * gt:4 * gt],
                                         gbh[2:3, cs], gbh[3:4, cs], c2[1:2, cs])
            h = af * hs_ref[rs, cs] + bf
            p = ps_ref[rs, cs] * af
            hs_ref[rs, cs] = h
            ps_ref[rs, cs] = p
            hl_ref[rs, cs] = h.astype(BF16)
            pc_ref[rs, cs] = p.astype(BF16)
            lar_ref[rs, cs] = lar.astype(BF16)
            br_ref[rs, cs] = br.astype(BF16)
            prod = pr_ref[rs, cs]
            hr_ref[rs, cs] = hr_ref[rs, cs] + prod * br
            pr_ref[rs, cs] = prod * ar


def _lru_pass_a(x4, gs8, sh8, wu, cw, cb, wg, gb, lam):
    nb, nrow, w, d = x4.shape
    e = wu.shape[1]
    rows = nb * w
    kern = functools.partial(_lru_pass_a_kernel, nrow=nrow)
    const = lambda a: pl.BlockSpec(a.shape, lambda i, nd=a.ndim: (0,) * nd)
    acc = pl.BlockSpec((rows, e), lambda i: (0, 0))
    acc_shape = jax.ShapeDtypeStruct((rows, e), F32)
    per_row = pl.BlockSpec((None, rows, e), lambda i: (jnp.maximum(i - LRU_LAG, 0), 0, 0))
    per_row_shape = jax.ShapeDtypeStruct((nrow, rows, e), BF16)
    return pl.pallas_call(
        kern,
        grid=(nrow + LRU_LAG,),
        in_specs=[pl.BlockSpec((nb, None, w, d), lambda i: (0, (i + nrow - 1) % nrow, 0, 0)),
                  const(gs8), const(sh8), const(wu), const(cw), const(cb), const(wg), const(gb), const(lam)],
        out_specs=[per_row] * 4 + [acc] * 4,
        out_shape=[per_row_shape] * 4 + [acc_shape] * 4,
        scratch_shapes=[pltpu.VMEM((4, rows, e), F32), pltpu.VMEM((rows, d), BF16), pltpu.VMEM((rows, e), F32),
                        pltpu.VMEM((rows, e), BF16), pltpu.VMEM((wg.shape[0], rows, wg.shape[2]), F32)],
        compiler_params=_cparams(("arbitrary",)),
        name="lru_pass_a",
    )(x4, gs8, sh8, wu, cw, cb, wg, gb, lam)


def _lru_stitch_kernel(pf_ref, hf_ref, pr_ref, hr_ref, ff_ref, fr_ref, sf_ref, sr_ref, *, w):
    rows = pf_ref.shape[0]
    col = lax.broadcasted_iota(jnp.int32, pf_ref.shape, 0) % w
    p, h = pf_ref[...], hf_ref[...]
    k = 1
    while k < w:
        keep = col >= k
        ps = jnp.where(keep, pltpu.roll(p, k, 0), 1.0)
        hs = jnp.where(keep, pltpu.roll(h, k, 0), 0.0)
        p, h = p * ps, p * hs + h
        k *= 2
    fin = ff_ref[...]
    sf_ref[...] = jnp.where(col == 0, fin, pltpu.roll(p, 1, 0) * fin + pltpu.roll(h, 1, 0))
    p, h = pr_ref[...], hr_ref[...]
    k = 1
    while k < w:
        keep = col < w - k
        ps = jnp.where(keep, pltpu.roll(p, rows - k, 0), 1.0)
        hs = jnp.where(keep, pltpu.roll(h, rows - k, 0), 0.0)
        p, h = p * ps, p * hs + h
        k *= 2
    fin = fr_ref[...]
    sr_ref[...] = jnp.where(col == w - 1, fin, pltpu.roll(p, rows - 1, 0) * fin + pltpu.roll(h, rows - 1, 0))


def _lru_stitch(pf, hf, pr, hr, ff_rows, fr_rows, w):
    rows, e = pf.shape
    big = pl.BlockSpec((rows, e), lambda i: (0, 0))
    shp = jax.ShapeDtypeStruct((rows, e), F32)
    return pl.pallas_call(
        functools.partial(_lru_stitch_kernel, w=w),
        grid=(1,),
        in_specs=[big] * 6,
        out_specs=[big, big],
        out_shape=[shp, shp],
        compiler_params=_cparams(("arbitrary",)),
        name="lru_stitch",
    )(pf, hf, pr, hr, ff_rows, fr_rows)


def _lru_rev_out_kernel(lar_ref, br_ref, hl_ref, pc_ref, x_ref, m5_ref, gs_ref, sh_ref, gt_ref, hsf_ref, hsr_ref,
                        wgl_ref, wo_ref, fg_ref, o_ref, h_ref, xb_ref, g_ref, mb_ref, mix_ref):
    i = pl.program_id(0)
    nb, w, d = x_ref.shape
    rows = nb * w
    e = h_ref.shape[-1]

    @pl.when(i == 0)
    def _():
        h_ref[...] = hsr_ref[...]

    _norm_mod_rows(x_ref, gs_ref, sh_ref, xb_ref)
    g_ref[...] = jnp.dot(xb_ref[...], wgl_ref[...], preferred_element_type=F32)
    for c0 in range(0, e, GATE_TILE):
        cs = slice(c0, c0 + GATE_TILE)
        for r0 in range(0, rows, LRU_RC):
            rs = slice(r0, r0 + LRU_RC)
            h = jnp.exp2(lar_ref[rs, cs].astype(F32)) * h_ref[rs, cs] + br_ref[rs, cs].astype(F32)
            h_ref[rs, cs] = h
            y = hl_ref[rs, cs].astype(F32) + pc_ref[rs, cs].astype(F32) * hsf_ref[rs, cs] + h
            hg = g_ref[rs, cs]
            mb_ref[rs, cs] = (y * (hg * jnp.tanh(hg) + hg)).astype(BF16)
    mix_ref[...] = (jnp.dot(m5_ref[...].reshape(rows, e), wo_ref[0:e, :], preferred_element_type=F32)
                    + jnp.dot(mb_ref[...], wo_ref[e:2 * e, :], preferred_element_type=F32))
    fg = fg_ref[...]
    for b in range(nb):
        for r0 in range(0, w, LRU_RC):
            res = x_ref[b, r0:r0 + LRU_RC, :] + gt_ref[b:b + 1] * mix_ref[b * w + r0:b * w + r0 + LRU_RC, :]
            ms = jnp.mean(res * res, axis=-1, keepdims=True)
            o_ref[b, r0:r0 + LRU_RC, :] = res * lax.rsqrt(ms + EPS) * fg


def _lru_rev_out(lar, br, hloc, pcum, x4, m5, gs8, sh8, gt8, hs_f, hs_r, wgl, wo, fg):
    nb, nrow, w, d = x4.shape
    rows, e = hs_f.shape
    const = lambda a: pl.BlockSpec(a.shape, lambda i, nd=a.ndim: (0,) * nd)
    rev3 = pl.BlockSpec((None, rows, e), lambda i: (nrow - 1 - i, 0, 0))
    rev4 = pl.BlockSpec((nb, None, w, d), lambda i: (0, nrow - 1 - i, 0, 0))
    rev4e = pl.BlockSpec((nb, None, w, e), lambda i: (0, nrow - 1 - i, 0, 0))
    return pl.pallas_call(
        _lru_rev_out_kernel,
        grid=(nrow,),
        in_specs=[rev3] * 4 + [rev4, rev4e, const(gs8), const(sh8), const(gt8), const(hs_f), const(hs_r),
                               const(wgl), const(wo), const(fg)],
        out_specs=rev4,
        out_shape=jax.ShapeDtypeStruct((nb, nrow, w, d), F32),
        scratch_shapes=[pltpu.VMEM((rows, e), F32), pltpu.VMEM((rows, d), BF16), pltpu.VMEM((rows, e), F32),
                        pltpu.VMEM((rows, e), BF16), pltpu.VMEM((rows, d), F32)],
        compiler_params=_cparams(("arbitrary",)),
        name="lru_rev_out",
    )(lar, br, hloc, pcum, x4, m5, gs8, sh8, gt8, hs_f, hs_r, wgl, wo, fg)


def _gate_tiles(w_a, w_x):
    ndir, heads, hd, _ = w_a.shape
    per = GATE_TILE // hd
    nq = heads // per

    def tiles(w):
        wq = w.reshape(nq, per, hd, hd)
        eye = jnp.eye(per, dtype=w.dtype)
        blk = wq[:, :, :, None, :] * eye[None, :, None, :, None]
        return blk.reshape(nq, GATE_TILE, GATE_TILE)

    return jnp.concatenate([tiles(w_a[0]), tiles(w_x[0]), tiles(w_a[1]), tiles(w_x[1])], axis=2).astype(BF16)


def kernel(x, c, ctx, c_ctx, w_mod, b_mod, norm_g, w_in, s5_a_re, s5_a_im, s5_log_step, s5_b_re, s5_b_im,
           s5_c_re, s5_c_im, s5_d, s5_w_glu, s5_b_glu, lru_conv_w, lru_conv_b, lru_w_a, lru_b_a, lru_w_x,
           lru_b_x, lru_lam, w_out, final_g):
    bsz, seq, d = x.shape
    nctx_tok = ctx.shape[1]
    e = s5_w_glu.shape[-1]
    assert w_mod.shape[0] == 1, "single-layer block"
    assert bsz <= 8 and seq % S5_SEG == 0 and seq % GRID_W == 0 and nctx_tok % S5_T == 0
    nseg = seq // S5_SEG
    ns = bsz * nseg
    nctx = nctx_tok // S5_T
    assert nctx <= nseg
    nrow = seq // GRID_W

    c8 = jnp.zeros((8, d), F32).at[:bsz].set(c).at[bsz].set(c_ctx)
    mod = _modulation(c8, w_mod[0], b_mod[0])
    sh, sc, gt = mod[:, :d], mod[:, d:2 * d], mod[:, 2 * d:]
    gs = norm_g[0][None, :] * (1.0 + sc)

    w_in0 = w_in[0]
    wt_us5 = w_in0[:, :e].T.astype(BF16)
    w_gs5 = (0.5 * w_in0[:, e:2 * e]).astype(BF16)
    w_ulru = w_in0[:, 2 * e:3 * e].astype(BF16)
    w_glru = (0.5 * w_in0[:, 3 * e:]).astype(BF16)
    w_glu = (0.5 * s5_w_glu[0]).astype(BF16)
    wo = w_out[0].astype(BF16)

    s5_params = _s5_param_layout(s5_a_re[0], s5_a_im[0], s5_log_step[0], s5_b_re[0], s5_b_im[0],
                                 s5_c_re[0], s5_c_im[0], s5_d[0])
    x3 = x.reshape(ns, S5_SEG, d)
    xt, hg5 = _s5_inproj(x3, gs, sh, wt_us5, w_gs5, bsz)
    ctx_pad = jnp.zeros((bsz, nseg, S5_T, d), F32).at[:, :nctx].set(ctx.reshape(bsz, nctx, S5_T, d))
    gs_c = jnp.broadcast_to(gs[bsz:bsz + 1], (8, d))
    sh_c = jnp.broadcast_to(sh[bsz:bsz + 1], (8, d))
    xct, _ = _s5_inproj(ctx_pad.reshape(ns, S5_T, d), gs_c, sh_c, wt_us5, w_gs5, bsz)
    y5 = _s5_core(xt, xct, s5_params, nseg, nctx)
    nj = S5_SEG // S5_T
    m5 = _s5_post(y5.reshape(nj, S5_T, e, ns), hg5, w_glu, 0.5 * s5_b_glu[0].reshape(1, e))

    wg_all = _gate_tiles(0.5 * lru_w_a[0], 0.5 * lru_w_x[0])
    gb_all = 0.5 * jnp.stack([lru_b_a[0, 0], lru_b_x[0, 0], lru_b_a[0, 1], lru_b_x[0, 1]])
    cw = lru_conv_w[0]
    cb = lru_conv_b[0].reshape(1, e)
    lam = lru_lam[0]
    ctx_p = jnp.zeros((8, nctx_tok, d), F32).at[:bsz].set(ctx).transpose(1, 0, 2).reshape(nctx_tok * 8, d)
    ff, fr = _lru_ctx(ctx_p, gs[bsz:bsz + 1], sh[bsz:bsz + 1], w_ulru, cw, cb, wg_all, gb_all, lam)
    x4 = x.reshape(bsz, nrow, GRID_W, d)
    lar, br, hloc, pcum, hf, pf, pr, hr = _lru_pass_a(x4, gs, sh, w_ulru, cw, cb, wg_all, gb_all, lam)
    hs_f, hs_r = _lru_stitch(pf, hf, pr, hr, jnp.repeat(ff[:bsz], GRID_W, axis=0),
                             jnp.repeat(fr[:bsz], GRID_W, axis=0), GRID_W)
    out4 = _lru_rev_out(lar, br, hloc, pcum, x4, m5.reshape(bsz, nrow, GRID_W, e), gs, sh, gt, hs_f, hs_r,
                        w_glru, wo, final_g.reshape(1, d))
    return out4.reshape(bsz, seq, d)
```

```python
import functools
import math

import jax
import jax.numpy as jnp
from jax import lax
from jax.experimental import pallas as pl
from jax.experimental.pallas import tpu as pltpu

F32 = jnp.float32
BF16 = jnp.bfloat16

EPS = 1e-6
TINY = 1e-30
GRID_W = 64
LRU_C = 8.0
S5_H = 16
S5_N = 64
S5_T = 16
S5_SEG = 256
S5_TB = 8
S5_PB = 16
LRU_HEAD = 64
GATE_TILE = 256
LRU_LAG = 3
LRU_RB = 2
LRU_RC = 16
VMEM_LIMIT = 56 * 1024 * 1024


def _cparams(sem):
    return pltpu.CompilerParams(dimension_semantics=sem, vmem_limit_bytes=VMEM_LIMIT)


def _sigmoid(z):
    return 0.5 * (jnp.tanh(0.5 * z) + 1.0)


def _gelu_tanh(y):
    return 0.5 * y * (1.0 + jnp.tanh(math.sqrt(2.0 / math.pi) * (y + 0.044715 * (y * y * y))))


def _norm_mod(x, gs, sh):
    ms = jnp.mean(x * x, axis=-1, keepdims=True)
    return x * lax.rsqrt(ms + EPS) * gs + sh


def _mod_kernel(c_ref, w_ref, b_ref, o_ref):
    c = c_ref[...]
    s = c * _sigmoid(c)
    o_ref[...] = jnp.dot(s, w_ref[...], preferred_element_type=F32) + b_ref[...]


def _modulation(c8, w_mod, b_mod):
    d = c8.shape[1]
    n = w_mod.shape[1]
    nb = n // d
    return pl.pallas_call(
        _mod_kernel,
        grid=(nb,),
        in_specs=[pl.BlockSpec((8, d), lambda i: (0, 0)),
                  pl.BlockSpec((d, d), lambda i: (0, i)),
                  pl.BlockSpec((1, d), lambda i: (0, i))],
        out_specs=pl.BlockSpec((8, d), lambda i: (0, i)),
        out_shape=jax.ShapeDtypeStruct((8, n), F32),
        compiler_params=_cparams(("arbitrary",)),
    )(c8, w_mod, b_mod.reshape(1, n))


def _perm_matrix(n_a, n_b):
    n = n_a * n_b
    r = lax.broadcasted_iota(jnp.int32, (n, n), 0)
    c = lax.broadcasted_iota(jnp.int32, (n, n), 1)
    return ((r // n_b == c % n_a) & (r % n_b == c // n_a)).astype(BF16)


def _s5_inproj_kernel(x_ref, gs_ref, sh_ref, wt_ref, wg_ref, o_ref, hg_ref, xs_ref, *, nbatch):
    ns, tb, d = x_ref.shape
    groups = o_ref.shape[0]
    rows_b = (ns // nbatch) * tb
    blk = S5_PB * tb
    perm = _perm_matrix(tb, S5_PB)
    x = x_ref[...].reshape(ns * tb, d)
    xn = jnp.concatenate(
        [_norm_mod(x[b * rows_b:(b + 1) * rows_b], gs_ref[b:b + 1], sh_ref[b:b + 1]).astype(BF16)
         for b in range(nbatch)], axis=0)
    for k in range(ns // S5_PB):
        q = jnp.dot(perm, xn[k * blk:(k + 1) * blk], preferred_element_type=F32).astype(BF16)
        for t in range(tb):
            xs_ref[t * ns + k * S5_PB:t * ns + (k + 1) * S5_PB, :] = q[t * S5_PB:(t + 1) * S5_PB]
    wt = wt_ref[...]
    for t in range(0, tb, 2):
        ut = lax.dot_general(wt, xs_ref[t * ns:(t + 2) * ns, :], (((1,), (1,)), ((), ())),
                             preferred_element_type=F32)
        o_ref[:, t * S5_H:(t + 1) * S5_H, :] = ut[:, :ns].reshape(groups, S5_H, ns).astype(BF16)
        o_ref[:, (t + 1) * S5_H:(t + 2) * S5_H, :] = ut[:, ns:].reshape(groups, S5_H, ns).astype(BF16)
    hg = jnp.dot(xs_ref[...], wg_ref[...], preferred_element_type=F32)
    hg_ref[...] = hg.astype(BF16).reshape(tb, ns, hg.shape[1])


def _s5_inproj(x3, gs8, sh8, wt, wg, nbatch):
    ns, seg, d = x3.shape
    e = wt.shape[0]
    groups = e // S5_H
    nj = seg // S5_T
    per = S5_T // S5_TB
    return pl.pallas_call(
        functools.partial(_s5_inproj_kernel, nbatch=nbatch),
        grid=(seg // S5_TB,),
        in_specs=[pl.BlockSpec((ns, S5_TB, d), lambda i: (0, i, 0)),
                  pl.BlockSpec((8, d), lambda i: (0, 0)),
                  pl.BlockSpec((8, d), lambda i: (0, 0)),
                  pl.BlockSpec((e, d), lambda i: (0, 0)),
                  pl.BlockSpec((d, e), lambda i: (0, 0))],
        out_specs=[pl.BlockSpec((groups, S5_TB * S5_H, ns), lambda i: (0, i % per, i // per)),
                   pl.BlockSpec((S5_TB, ns, e), lambda i: (i, 0, 0))],
        out_shape=[jax.ShapeDtypeStruct((groups, S5_T * S5_H, nj * ns), BF16),
                   jax.ShapeDtypeStruct((seg, ns, e), BF16)],
        scratch_shapes=[pltpu.VMEM((S5_TB * ns, d), BF16)],
        compiler_params=_cparams(("arbitrary",)),
        name="s5_inproj",
    )(x3, gs8, sh8, wt, wg)


def _cmul_add(dr, di, hr, hi, sr, si):
    return dr * hr - di * hi + sr, dr * hi + di * hr + si


def _lane_scan(vr, vi, dr, di, slot, count, reverse):
    n = vr.shape[1]
    k = 1
    while k < count:
        if reverse:
            sr = pltpu.roll(vr, n - k, 1)
            si = pltpu.roll(vi, n - k, 1)
            keep = slot < count - k
        else:
            sr = pltpu.roll(vr, k, 1)
            si = pltpu.roll(vi, k, 1)
            keep = slot >= k
        sr = jnp.where(keep, sr, 0.0)
        si = jnp.where(keep, si, 0.0)
        vr, vi = _cmul_add(dr, di, sr, si, vr, vi)
        dr, di = dr * dr - di * di, 2.0 * dr * di
        k *= 2
    return vr, vi


def _csq(r, i):
    return r * r - i * i, 2.0 * r * i


def _cpow_bits(e, pows):
    pr = jnp.ones(e.shape, F32)
    pi = jnp.zeros(e.shape, F32)
    for bit, (qr, qi) in pows:
        on = (e & bit) != 0
        pr, pi = jnp.where(on, pr * qr - pi * qi, pr), jnp.where(on, pr * qi + pi * qr, pi)
    return pr, pi


def _s5_discretize(ar, ai, ls):
    step = jnp.exp(ls)
    mag = jnp.exp(ar * step)
    pr = mag * jnp.cos(ai * step)
    pi = mag * jnp.sin(ai * step)
    den = ar * ar + ai * ai
    nr = pr - 1.0
    return pr, pi, (nr * ar + pi * ai) / den, (pi * ar - nr * ai) / den


def _s5_operators(pcol_ref, prow_ref, bt_ref, ct1_ref, ct2_ref, dsk_ref):
    n, t_len, th = S5_N, S5_T, S5_T * S5_H
    hp = lax.Precision.HIGHEST
    lane_t = lax.broadcasted_iota(jnp.int32, (1, th), 1) // S5_H
    bits = [1 << b for b in range(t_len.bit_length() - 1)]

    blocks, bbs, decay = [], [], []
    for d in range(2):
        pr, pi, cr, ci = _s5_discretize(pcol_ref[3 * d], pcol_ref[3 * d + 1], pcol_ref[3 * d + 2])
        pows, q = [], (pr, pi)
        for bit in bits:
            pows.append((bit, q))
            q = _csq(*q)
        decay.append(q)
        btr = bt_ref[2 * d * n:(2 * d + 1) * n, :]
        bti = bt_ref[(2 * d + 1) * n:(2 * d + 2) * n, :]
        bbr = cr * btr - ci * bti
        bbi = cr * bti + ci * btr
        e = (t_len - 1 - lane_t) if d == 0 else lane_t
        wr, wi = _cpow_bits(jnp.broadcast_to(e, (n, th)), pows)
        blocks += [wr * bbr - wi * bbi, wr * bbi + wi * bbr]
        bbs.append(jnp.concatenate([bbr, bbi], axis=0))
    bpow = jnp.concatenate(blocks, axis=0).astype(BF16)

    qr, qi, _, _ = _s5_discretize(prow_ref[0:1], prow_ref[1:2], prow_ref[2:3])
    pows, q = [], (qr, qi)
    for bit in bits:
        pows.append((bit, q))
        q = _csq(*q)
    q_t = q
    row_t = lax.broadcasted_iota(jnp.int32, (th, 1), 0) // S5_H
    is_f = lax.broadcasted_iota(jnp.int32, (1, 4 * n), 1) < 2 * n
    e_k = jnp.where(is_f, row_t, (t_len - row_t) % t_len)
    kr, ki = _cpow_bits(e_k, pows)
    nr = jnp.where(is_f, kr * qr - ki * qi, jnp.where(row_t == 0, q_t[0], kr))
    ni = jnp.where(is_f, kr * qi + ki * qr, jnp.where(row_t == 0, q_t[1], ki))
    ct1 = jnp.concatenate([ct1_ref[...]] * t_len, axis=0)
    ct2 = jnp.concatenate([ct2_ref[...]] * t_len, axis=0)
    cpow = (ct1 * nr + ct2 * ni).astype(BF16)
    qk = ct1 * kr + ct2 * ki
    mf = jnp.dot(qk[:, :2 * n], bbs[0], precision=hp, preferred_element_type=F32)
    mr = jnp.dot(qk[:, 2 * n:], bbs[1], precision=hp, preferred_element_type=F32)
    m = (jnp.where(row_t <= t_len - 1 - lane_t, mf, 0.0)
         + jnp.where((row_t == 0) | (row_t >= t_len - lane_t), mr, 0.0))
    for bit in bits:
        m = jnp.where((lane_t & bit) != 0, pltpu.roll(m, S5_H * bit, 0), m)
    row = lax.broadcasted_iota(jnp.int32, (th, th), 0)
    col = lax.broadcasted_iota(jnp.int32, (th, th), 1)
    toep = (m + jnp.where(row == col, dsk_ref[...], 0.0)).astype(BF16)
    return toep, bpow, cpow, decay


def _s5_core_kernel(xt_ref, xc_ref, pcol_ref, pp_ref, bt_ref, o_ref, s_ref, *, nseg, nctx):
    n = S5_N
    h = S5_H
    ns = xc_ref.shape[-1]
    nj = xt_ref.shape[-1] // ns
    toep, bpow, cpow, decay = _s5_operators(pcol_ref, pp_ref.at[2 * h:2 * h + 3], bt_ref, pp_ref.at[0:h],
                                            pp_ref.at[h:2 * h], pp_ref.at[2 * h + 3:2 * h + 4])
    dfr = jnp.broadcast_to(decay[0][0], (n, ns))
    dfi = jnp.broadcast_to(decay[0][1], (n, ns))
    drr = jnp.broadcast_to(decay[1][0], (n, ns))
    dri = jnp.broadcast_to(decay[1][1], (n, ns))
    slot = lax.broadcasted_iota(jnp.int32, (n, ns), 1) % nseg

    sc = jnp.dot(bpow, xc_ref[...], preferred_element_type=F32)
    valid = slot < nctx
    cfr, cfi = _lane_scan(jnp.where(valid, sc[0:n], 0.0), jnp.where(valid, sc[n:2 * n], 0.0),
                          dfr, dfi, slot, nseg, False)
    crr, cri = _lane_scan(jnp.where(valid, sc[2 * n:3 * n], 0.0), jnp.where(valid, sc[3 * n:4 * n], 0.0),
                          drr, dri, slot, nseg, True)

    step = 2 if nj % 2 == 0 else 1
    for j in range(0, nj, step):
        cols = slice(j * ns, (j + step) * ns)
        s_ref[:, cols] = jnp.dot(bpow, xt_ref[:, cols], preferred_element_type=F32)

    zero = jnp.zeros((n, ns), F32)
    efr, efi, err, eri = zero, zero, zero, zero
    for j in range(nj):
        cols = slice(j * ns, (j + 1) * ns)
        efr, efi = _cmul_add(dfr, dfi, efr, efi, s_ref[0:n, cols], s_ref[n:2 * n, cols])
    for j in range(nj - 1, -1, -1):
        cols = slice(j * ns, (j + 1) * ns)
        err, eri = _cmul_add(drr, dri, err, eri, s_ref[2 * n:3 * n, cols], s_ref[3 * n:4 * n, cols])

    sfr, sfi, srr, sri = dfr, dfi, drr, dri
    k = 1
    while k < nj:
        sfr, sfi = sfr * sfr - sfi * sfi, 2.0 * sfr * sfi
        srr, sri = srr * srr - sri * sri, 2.0 * srr * sri
        k *= 2

    first = slot == 0
    last = slot == nseg - 1
    vfr = jnp.where(first, pltpu.roll(cfr, ns - (nctx - 1), 1) if nctx > 1 else cfr, pltpu.roll(efr, 1, 1))
    vfi = jnp.where(first, pltpu.roll(cfi, ns - (nctx - 1), 1) if nctx > 1 else cfi, pltpu.roll(efi, 1, 1))
    vrr = jnp.where(last, pltpu.roll(crr, nseg - 1, 1) if nseg > 1 else crr, pltpu.roll(err, ns - 1, 1))
    vri = jnp.where(last, pltpu.roll(cri, nseg - 1, 1) if nseg > 1 else cri, pltpu.roll(eri, ns - 1, 1))
    hfr, hfi = _lane_scan(vfr, vfi, sfr, sfi, slot, nseg, False)
    hrr, hri = _lane_scan(vrr, vri, srr, sri, slot, nseg, True)

    for j in range(nj):
        cols = slice(j * ns, (j + 1) * ns)
        nr, ni = _cmul_add(dfr, dfi, hfr, hfi, s_ref[0:n, cols], s_ref[n:2 * n, cols])
        s_ref[0:n, cols] = hfr
        s_ref[n:2 * n, cols] = hfi
        hfr, hfi = nr, ni
    for j in range(nj - 1, -1, -1):
        cols = slice(j * ns, (j + 1) * ns)
        nr, ni = _cmul_add(drr, dri, hrr, hri, s_ref[2 * n:3 * n, cols], s_ref[3 * n:4 * n, cols])
        s_ref[2 * n:3 * n, cols] = hrr
        s_ref[3 * n:4 * n, cols] = hri
        hrr, hri = nr, ni

    for j in range(0, nj, step):
        cols = slice(j * ns, (j + step) * ns)
        y = (jnp.dot(toep, xt_ref[:, cols], preferred_element_type=F32)
             + jnp.dot(cpow, s_ref[:, cols].astype(BF16), preferred_element_type=F32))
        for q in range(step):
            o_ref[j + q] = y[:, q * ns:(q + 1) * ns].reshape(S5_T, S5_H, ns).astype(BF16)


def _s5_core(xt, xct, params, nseg, nctx):
    groups, rows, lanes = xt.shape
    ns = xct.shape[-1]
    nj = lanes // ns
    kern = functools.partial(_s5_core_kernel, nseg=nseg, nctx=nctx)
    per_group = lambda a: pl.BlockSpec((None,) + a.shape[1:], lambda g, nd=a.ndim: (g,) + (0,) * (nd - 1))
    return pl.pallas_call(
        kern,
        grid=(groups,),
        in_specs=[per_group(xt), per_group(xct)] + [per_group(a) for a in params],
        out_specs=pl.BlockSpec((nj, S5_T, None, S5_H, ns), lambda g: (0, 0, g, 0, 0)),
        out_shape=jax.ShapeDtypeStruct((nj, S5_T, groups, S5_H, ns), BF16),
        scratch_shapes=[pltpu.VMEM((rows, lanes), F32)],
        compiler_params=_cparams(("arbitrary",)),
        name="s5_core",
    )(xt, xct, *params)


def _s5_param_layout(a_re, a_im, log_step, b_re, b_im, c_re, c_im, d_skip):
    ndir, groups, n = a_re.shape
    h = b_re.shape[-1]
    ls = jnp.broadcast_to(log_step[..., None], a_re.shape)
    pcol = jnp.stack([a_re[0], a_im[0], ls[0], a_re[1], a_im[1], ls[1]], axis=1)[..., None]
    lay = lambda v: jnp.concatenate([v[0], v[0], v[1], v[1]], axis=-1)
    prow = jnp.stack([lay(a_re), lay(a_im), lay(ls)], axis=1)
    tile_t = lambda v: jnp.tile(v, (1, 1, S5_T))
    bt = jnp.concatenate([tile_t(b_re[0]), tile_t(b_im[0]), tile_t(b_re[1]), tile_t(b_im[1])], axis=1)
    ct1 = jnp.concatenate([c_re[0], -c_im[0], c_re[1], -c_im[1]], axis=-1)
    ct2 = jnp.concatenate([-c_im[0], -c_re[0], -c_im[1], -c_re[1]], axis=-1)
    dsk = jnp.tile(d_skip.reshape(groups, 1, h), (1, 1, S5_T))
    pad = jnp.zeros((groups, (-(2 * h + 4)) % 8, S5_T * h), F32)
    pp = jnp.concatenate([ct1, ct2, prow, dsk, pad], axis=1)
    return tuple(v.astype(F32) for v in (pcol, pp, bt))


def _s5_post_kernel(y_ref, hg_ref, wglu_ref, bglu_ref, o_ref, m_ref):
    q = pl.program_id(1)
    tb, e, ns = y_ref.shape
    ys = jnp.concatenate([y_ref[t].astype(F32).T for t in range(tb)], axis=0)
    yg = _gelu_tanh(ys)
    zh = jnp.dot(yg.astype(BF16), wglu_ref[...], preferred_element_type=F32) + bglu_ref[...]
    hg = hg_ref[...].reshape(tb * ns, e).astype(F32)
    m = ((yg * (0.5 * jnp.tanh(zh) + 0.5)) * (hg * jnp.tanh(hg) + hg)).astype(BF16)
    nq = S5_T // tb
    for half in range(nq):
        @pl.when(q == half)
        def _(half=half):
            m_ref[half * tb * ns:(half + 1) * tb * ns, :] = m

    @pl.when(q == nq - 1)
    def _():
        perm = _perm_matrix(S5_PB, S5_T)
        for k in range(ns // S5_PB):
            g = jnp.concatenate([m_ref[t * ns + k * S5_PB:t * ns + (k + 1) * S5_PB, :] for t in range(S5_T)], axis=0)
            r = jnp.dot(perm, g, preferred_element_type=F32).astype(BF16)
            o_ref[k * S5_PB:(k + 1) * S5_PB] = r.reshape(S5_PB, S5_T, e)


def _s5_post(y4, hg, wglu, bglu):
    nj, t_all, e, ns = y4.shape
    nt = t_all // S5_TB
    return pl.pallas_call(
        _s5_post_kernel,
        grid=(nj, nt),
        in_specs=[pl.BlockSpec((None, S5_TB, e, ns), lambda j, q: (j, q, 0, 0)),
                  pl.BlockSpec((S5_TB, ns, e), lambda j, q: (j * nt + q, 0, 0)),
                  pl.BlockSpec((e, e), lambda j, q: (0, 0)),
                  pl.BlockSpec((1, e), lambda j, q: (0, 0))],
        out_specs=pl.BlockSpec((ns, S5_T, e), lambda j, q: (0, j, 0)),
        out_shape=jax.ShapeDtypeStruct((ns, nj * S5_T, e), BF16),
        scratch_shapes=[pltpu.VMEM((S5_T * ns, e), BF16)],
        compiler_params=_cparams(("arbitrary", "arbitrary")),
        name="s5_post",
    )(y4, hg, wglu, bglu)


def _softplus(z):
    return jnp.maximum(z, 0.0) + jnp.log1p(jnp.exp(-jnp.abs(z)))


def _gate_matmul(xc, wg_ref):
    xb = xc.astype(BF16)
    nq = wg_ref.shape[0]
    return [jnp.dot(xb[:, q * GATE_TILE:(q + 1) * GATE_TILE], wg_ref[q], preferred_element_type=F32)
            for q in range(nq)]


def _gate_cols(zs, k):
    return jnp.concatenate([z[:, k * GATE_TILE:(k + 1) * GATE_TILE] for z in zs], axis=1)


def _lru_decay_scale(lam_row):
    return (-0.5 * LRU_C * math.log2(math.e)) * _softplus(-lam_row)


def _lru_gate_math(hx, za, zx, ba, bx, c2):
    ta = jnp.tanh(za + ba)
    tx = jnp.tanh(zx + bx)
    la = ta * c2 + c2
    a = jnp.exp2(la)
    om = 1.0 - a * a
    return la, a, (om * lax.rsqrt(jnp.maximum(om, TINY))) * (tx * hx + hx)


def _lru_coeffs(hx, zs, k, gbh, c2):
    return _lru_gate_math(hx, _gate_cols(zs, 2 * k), _gate_cols(zs, 2 * k + 1),
                          gbh[2 * k:2 * k + 1], gbh[2 * k + 1:2 * k + 2], c2)[1:]


def _norm_mod_rows(x_ref, gs_ref, sh_ref, dst_ref):
    nb, w, _ = x_ref.shape
    for b in range(nb):
        dst_ref[b * w:(b + 1) * w, :] = _norm_mod(x_ref[b], gs_ref[b:b + 1], sh_ref[b:b + 1]).astype(BF16)


def _lru_ctx_kernel(x_ref, gs_ref, sh_ref, wu_ref, cw_ref, cb_ref, wg_ref, gb_ref, lam_ref,
                    ff_ref, fr_ref, af_s, bf_s, ar_s, br_s):
    rows = x_ref.shape[0]
    npos = rows // 8
    xn = _norm_mod(x_ref[...], gs_ref[...], sh_ref[...]).astype(BF16)
    u = jnp.dot(xn, wu_ref[...], preferred_element_type=F32)
    pos = lax.broadcasted_iota(jnp.int32, u.shape, 0) // 8
    um1 = jnp.where(pos >= 1, pltpu.roll(u, 8, 0), 0.0)
    up1 = jnp.where(pos < npos - 1, pltpu.roll(u, rows - 8, 0), 0.0)
    up2 = jnp.where(pos < npos - 2, pltpu.roll(u, rows - 16, 0), 0.0)
    cw = cw_ref[...]
    xc = cw[0:1] * um1 + cw[1:2] * u + cw[2:3] * up1 + cw[3:4] * up2 + cb_ref[...]
    zs = _gate_matmul(xc, wg_ref)
    gbh = gb_ref[...]
    c2 = _lru_decay_scale(lam_ref[...])
    hx = 0.5 * xc
    a, b = _lru_coeffs(hx, zs, 0, gbh, c2[0:1])
    af_s[...] = a
    bf_s[...] = b
    a, b = _lru_coeffs(hx, zs, 1, gbh, c2[1:2])
    ar_s[...] = a
    br_s[...] = b

    def body(p, carry):
        hf, hr, pr = carry
        r0 = pl.multiple_of(p * 8, 8)
        hf = af_s[pl.ds(r0, 8), :] * hf + bf_s[pl.ds(r0, 8), :]
        hr = hr + pr * br_s[pl.ds(r0, 8), :]
        pr = pr * ar_s[pl.ds(r0, 8), :]
        return hf, hr, pr

    zero = jnp.zeros((8, u.shape[1]), F32)
    hf, hr, _ = lax.fori_loop(0, npos, body, (zero, zero, zero + 1.0))
    ff_ref[...] = hf
    fr_ref[...] = hr


def _lru_ctx(xp, gs, sh, wu, cw, cb, wg, gb, lam):
    rows, d = xp.shape
    e = wu.shape[1]
    nq = e // GATE_TILE
    full = lambda a: pl.BlockSpec(a.shape, lambda q, nd=a.ndim: (0,) * nd)
    cols = lambda a: pl.BlockSpec((a.shape[0], GATE_TILE), lambda q: (0, q))
    return pl.pallas_call(
        _lru_ctx_kernel,
        grid=(nq,),
        in_specs=[full(xp), full(gs), full(sh), cols(wu), cols(cw), cols(cb),
                  pl.BlockSpec((1,) + wg.shape[1:], lambda q: (q, 0, 0)), cols(gb), cols(lam)],
        out_specs=[pl.BlockSpec((8, GATE_TILE), lambda q: (0, q)), pl.BlockSpec((8, GATE_TILE), lambda q: (0, q))],
        out_shape=[jax.ShapeDtypeStruct((8, e), F32), jax.ShapeDtypeStruct((8, e), F32)],
        scratch_shapes=[pltpu.VMEM((rows, GATE_TILE), F32) for _ in range(4)],
        compiler_params=_cparams(("arbitrary",)),
    )(xp, gs, sh, wu, cw, cb, wg, gb, lam)


def _lru_pass_a_kernel(x_ref, gs_ref, sh_ref, wu_ref, cw_ref, cb_ref, wg_ref, gb_ref, lam_ref,
                       lar_ref, br_ref, hl_ref, pc_ref, hs_ref, ps_ref, pr_ref, hr_ref,
                       ring_ref, xb_ref, xc_ref, xcb_ref, z_ref, *, nrow):
    i = pl.program_id(0)
    nb, w, d = x_ref.shape
    rows = nb * w
    nq = wg_ref.shape[0]
    gt = GATE_TILE

    @pl.when(i == 0)
    def _():
        ring_ref[...] = jnp.zeros_like(ring_ref)

    @pl.when(i <= LRU_LAG)
    def _():
        hs_ref[...] = jnp.zeros_like(hs_ref)
        ps_ref[...] = jnp.ones_like(ps_ref)
        hr_ref[...] = jnp.zeros_like(hr_ref)
        pr_ref[...] = jnp.ones_like(pr_ref)

    _norm_mod_rows(x_ref, gs_ref, sh_ref, xb_ref)
    off = jnp.where(i == 0, -1, jnp.where(i > nrow, 1, 0))
    r = lax.broadcasted_iota(jnp.int32, (rows, rows), 0)
    c = lax.broadcasted_iota(jnp.int32, (rows, rows), 1)
    mix = ((c == r + off) & (r // w == c // w)).astype(BF16)
    xs = jnp.dot(mix, xb_ref[...], preferred_element_type=F32).astype(BF16)
    ring_ref[i & 3] = jnp.dot(xs, wu_ref[...], preferred_element_type=F32)

    cw = cw_ref[...]
    cb = cb_ref[...]
    s0, s1, s2, s3 = (i - 3) & 3, (i - 2) & 3, (i - 1) & 3, i & 3
    for r0 in range(0, rows, LRU_RC):
        rs = slice(r0, r0 + LRU_RC)
        xc = (cw[0:1] * ring_ref[s0, rs, :] + cw[1:2] * ring_ref[s1, rs, :]
              + cw[2:3] * ring_ref[s2, rs, :] + cw[3:4] * ring_ref[s3, rs, :] + cb)
        xc_ref[rs, :] = xc
        xcb_ref[rs, :] = xc.astype(BF16)
    for q in range(nq):
        z_ref[q] = jnp.dot(xcb_ref[:, q * gt:(q + 1) * gt], wg_ref[q], preferred_element_type=F32)
    gbh = gb_ref[...]
    c2 = _lru_decay_scale(lam_ref[...])
    for q in range(nq):
        cs = slice(q * gt, (q + 1) * gt)
        for r0 in range(0, rows, LRU_RC):
            rs = slice(r0, r0 + LRU_RC)
            hx = 0.5 * xc_ref[rs, cs]
            _, af, bf = _lru_gate_math(hx, z_ref[q, rs, 0:gt], z_ref[q, rs, gt:2 * gt],
                                       gbh[0:1, cs], gbh[1:2, cs], c2[0:1, cs])
            lar, ar, br = _lru_gate_math(hx, z_ref[q, rs, 2 * gt:3 * gt], z_ref[q, rs, 3 * gt:4 * gt],
                                         gbh[2:3, cs], gbh[3:4, cs], c2[1:2, cs])
            h = af * hs_ref[rs, cs] + bf
            p = ps_ref[rs, cs] * af
            hs_ref[rs, cs] = h
            ps_ref[rs, cs] = p
            hl_ref[rs, cs] = h.astype(BF16)
            pc_ref[rs, cs] = p.astype(BF16)
            lar_ref[rs, cs] = lar.astype(BF16)
            br_ref[rs, cs] = br.astype(BF16)
            prod = pr_ref[rs, cs]
            hr_ref[rs, cs] = hr_ref[rs, cs] + prod * br
            pr_ref[rs, cs] = prod * ar


def _lru_pass_a(x4, gs8, sh8, wu, cw, cb, wg, gb, lam):
    nb, nrow, w, d = x4.shape
    e = wu.shape[1]
    rows = nb * w
    kern = functools.partial(_lru_pass_a_kernel, nrow=nrow)
    const = lambda a: pl.BlockSpec(a.shape, lambda i, nd=a.ndim: (0,) * nd)
    acc = pl.BlockSpec((rows, e), lambda i: (0, 0))
    acc_shape = jax.ShapeDtypeStruct((rows, e), F32)
    per_row = pl.BlockSpec((None, rows, e), lambda i: (jnp.maximum(i - LRU_LAG, 0), 0, 0))
    per_row_shape = jax.ShapeDtypeStruct((nrow, rows, e), BF16)
    return pl.pallas_call(
        kern,
        grid=(nrow + LRU_LAG,),
        in_specs=[pl.BlockSpec((nb, None, w, d), lambda i: (0, (i + nrow - 1) % nrow, 0, 0)),
                  const(gs8), const(sh8), const(wu), const(cw), const(cb), const(wg), const(gb), const(lam)],
        out_specs=[per_row] * 4 + [acc] * 4,
        out_shape=[per_row_shape] * 4 + [acc_shape] * 4,
        scratch_shapes=[pltpu.VMEM((4, rows, e), F32), pltpu.VMEM((rows, d), BF16), pltpu.VMEM((rows, e), F32),
                        pltpu.VMEM((rows, e), BF16), pltpu.VMEM((wg.shape[0], rows, wg.shape[2]), F32)],
        compiler_params=_cparams(("arbitrary",)),
        name="lru_pass_a",
    )(x4, gs8, sh8, wu, cw, cb, wg, gb, lam)


def _lru_stitch_kernel(pf_ref, hf_ref, pr_ref, hr_ref, ff_ref, fr_ref, sf_ref, sr_ref, *, w):
    rows = pf_ref.shape[0]
    col = lax.broadcasted_iota(jnp.int32, pf_ref.shape, 0) % w
    p, h = pf_ref[...], hf_ref[...]
    k = 1
    while k < w:
        keep = col >= k
        ps = jnp.where(keep, pltpu.roll(p, k, 0), 1.0)
        hs = jnp.where(keep, pltpu.roll(h, k, 0), 0.0)
        p, h = p * ps, p * hs + h
        k *= 2
    fin = ff_ref[...]
    sf_ref[...] = jnp.where(col == 0, fin, pltpu.roll(p, 1, 0) * fin + pltpu.roll(h, 1, 0))
    p, h = pr_ref[...], hr_ref[...]
    k = 1
    while k < w:
        keep = col < w - k
        ps = jnp.where(keep, pltpu.roll(p, rows - k, 0), 1.0)
        hs = jnp.where(keep, pltpu.roll(h, rows - k, 0), 0.0)
        p, h = p * ps, p * hs + h
        k *= 2
    fin = fr_ref[...]
    sr_ref[...] = jnp.where(col == w - 1, fin, pltpu.roll(p, rows - 1, 0) * fin + pltpu.roll(h, rows - 1, 0))


def _lru_stitch(pf, hf, pr, hr, ff_rows, fr_rows, w):
    rows, e = pf.shape
    big = pl.BlockSpec((rows, e), lambda i: (0, 0))
    shp = jax.ShapeDtypeStruct((rows, e), F32)
    return pl.pallas_call(
        functools.partial(_lru_stitch_kernel, w=w),
        grid=(1,),
        in_specs=[big] * 6,
        out_specs=[big, big],
        out_shape=[shp, shp],
        compiler_params=_cparams(("arbitrary",)),
        name="lru_stitch",
    )(pf, hf, pr, hr, ff_rows, fr_rows)


def _lru_rev_out_kernel(lar_ref, br_ref, hl_ref, pc_ref, x_ref, m5_ref, gs_ref, sh_ref, gt_ref, hsf_ref, hsr_ref,
                        wgl_ref, wo_ref, fg_ref, o_ref, h_ref, xb_ref, g_ref, mb_ref, mix_ref):
    i = pl.program_id(0)
    nb, nrb, w, d = x_ref.shape
    rows = nb * w
    e = h_ref.shape[-1]

    @pl.when(i == 0)
    def _():
        h_ref[...] = hsr_ref[...]

    for rr in range(nrb):
        for b in range(nb):
            xb_ref[rr * rows + b * w:rr * rows + (b + 1) * w, :] = _norm_mod(
                x_ref[b, rr], gs_ref[b:b + 1], sh_ref[b:b + 1]).astype(BF16)
    g_ref[...] = jnp.dot(xb_ref[...], wgl_ref[...], preferred_element_type=F32)
    for rr in range(nrb - 1, -1, -1):
        for c0 in range(0, e, GATE_TILE):
            cs = slice(c0, c0 + GATE_TILE)
            for r0 in range(0, rows, LRU_RC):
                rs = slice(r0, r0 + LRU_RC)
                gs_rows = slice(rr * rows + r0, rr * rows + r0 + LRU_RC)
                h = jnp.exp2(lar_ref[rr, rs, cs].astype(F32)) * h_ref[rs, cs] + br_ref[rr, rs, cs].astype(F32)
                h_ref[rs, cs] = h
                y = hl_ref[rr, rs, cs].astype(F32) + pc_ref[rr, rs, cs].astype(F32) * hsf_ref[rs, cs] + h
                hg = g_ref[gs_rows, cs]
                mb_ref[gs_rows, cs] = (y * (hg * jnp.tanh(hg) + hg)).astype(BF16)
    m5 = jnp.concatenate([m5_ref[:, rr].reshape(rows, e) for rr in range(nrb)], axis=0)
    mix_ref[...] = (jnp.dot(m5, wo_ref[0:e, :], preferred_element_type=F32)
                    + jnp.dot(mb_ref[...], wo_ref[e:2 * e, :], preferred_element_type=F32))
    fg = fg_ref[...]
    for rr in range(nrb):
        for b in range(nb):
            for r0 in range(0, w, LRU_RC):
                m0 = rr * rows + b * w + r0
                res = x_ref[b, rr, r0:r0 + LRU_RC, :] + gt_ref[b:b + 1] * mix_ref[m0:m0 + LRU_RC, :]
                ms = jnp.mean(res * res, axis=-1, keepdims=True)
                o_ref[b, rr, r0:r0 + LRU_RC, :] = res * lax.rsqrt(ms + EPS) * fg


def _lru_rev_out(lar, br, hloc, pcum, x4, m5, gs8, sh8, gt8, hs_f, hs_r, wgl, wo, fg):
    nb, nrow, w, d = x4.shape
    rows, e = hs_f.shape
    const = lambda a: pl.BlockSpec(a.shape, lambda i, nd=a.ndim: (0,) * nd)
    rb = LRU_RB
    nstep = nrow // rb
    rev3 = pl.BlockSpec((rb, rows, e), lambda i: (nstep - 1 - i, 0, 0))
    rev4 = pl.BlockSpec((nb, rb, w, d), lambda i: (0, nstep - 1 - i, 0, 0))
    rev4e = pl.BlockSpec((nb, rb, w, e), lambda i: (0, nstep - 1 - i, 0, 0))
    return pl.pallas_call(
        _lru_rev_out_kernel,
        grid=(nstep,),
        in_specs=[rev3] * 4 + [rev4, rev4e, const(gs8), const(sh8), const(gt8), const(hs_f), const(hs_r),
                               const(wgl), const(wo), const(fg)],
        out_specs=rev4,
        out_shape=jax.ShapeDtypeStruct((nb, nrow, w, d), F32),
        scratch_shapes=[pltpu.VMEM((rows, e), F32), pltpu.VMEM((rb * rows, d), BF16),
                        pltpu.VMEM((rb * rows, e), F32), pltpu.VMEM((rb * rows, e), BF16),
                        pltpu.VMEM((rb * rows, d), F32)],
        compiler_params=_cparams(("arbitrary",)),
        name="lru_rev_out",
    )(lar, br, hloc, pcum, x4, m5, gs8, sh8, gt8, hs_f, hs_r, wgl, wo, fg)


def _gate_tiles(w_a, w_x):
    ndir, heads, hd, _ = w_a.shape
    per = GATE_TILE // hd
    nq = heads // per

    def tiles(w):
        wq = w.reshape(nq, per, hd, hd)
        eye = jnp.eye(per, dtype=w.dtype)
        blk = wq[:, :, :, None, :] * eye[None, :, None, :, None]
        return blk.reshape(nq, GATE_TILE, GATE_TILE)

    return jnp.concatenate([tiles(w_a[0]), tiles(w_x[0]), tiles(w_a[1]), tiles(w_x[1])], axis=2).astype(BF16)


def kernel(x, c, ctx, c_ctx, w_mod, b_mod, norm_g, w_in, s5_a_re, s5_a_im, s5_log_step, s5_b_re, s5_b_im,
           s5_c_re, s5_c_im, s5_d, s5_w_glu, s5_b_glu, lru_conv_w, lru_conv_b, lru_w_a, lru_b_a, lru_w_x,
           lru_b_x, lru_lam, w_out, final_g):
    bsz, seq, d = x.shape
    nctx_tok = ctx.shape[1]
    e = s5_w_glu.shape[-1]
    assert w_mod.shape[0] == 1, "single-layer block"
    assert bsz <= 8 and seq % S5_SEG == 0 and seq % GRID_W == 0 and nctx_tok % S5_T == 0
    nseg = seq // S5_SEG
    ns = bsz * nseg
    nctx = nctx_tok // S5_T
    assert nctx <= nseg and ns % S5_PB == 0
    nrow = seq // GRID_W
    assert nrow % LRU_RB == 0 and nrow > LRU_LAG

    c8 = jnp.zeros((8, d), F32).at[:bsz].set(c).at[bsz].set(c_ctx)
    mod = _modulation(c8, w_mod[0], b_mod[0])
    sh, sc, gt = mod[:, :d], mod[:, d:2 * d], mod[:, 2 * d:]
    gs = norm_g[0][None, :] * (1.0 + sc)

    w_in0 = w_in[0]
    wt_us5 = w_in0[:, :e].T.astype(BF16)
    w_gs5 = (0.5 * w_in0[:, e:2 * e]).astype(BF16)
    w_ulru = w_in0[:, 2 * e:3 * e].astype(BF16)
    w_glru = (0.5 * w_in0[:, 3 * e:]).astype(BF16)
    w_glu = (0.5 * s5_w_glu[0]).astype(BF16)
    wo = w_out[0].astype(BF16)

    s5_params = _s5_param_layout(s5_a_re[0], s5_a_im[0], s5_log_step[0], s5_b_re[0], s5_b_im[0],
                                 s5_c_re[0], s5_c_im[0], s5_d[0])
    x3 = x.reshape(ns, S5_SEG, d)
    xt, hg5 = _s5_inproj(x3, gs, sh, wt_us5, w_gs5, bsz)
    ctx_pad = jnp.zeros((bsz, nseg, S5_T, d), F32).at[:, :nctx].set(ctx.reshape(bsz, nctx, S5_T, d))
    gs_c = jnp.broadcast_to(gs[bsz:bsz + 1], (8, d))
    sh_c = jnp.broadcast_to(sh[bsz:bsz + 1], (8, d))
    xct, _ = _s5_inproj(ctx_pad.reshape(ns, S5_T, d), gs_c, sh_c, wt_us5, w_gs5, bsz)
    y5 = _s5_core(xt, xct, s5_params, nseg, nctx)
    nj = S5_SEG // S5_T
    m5 = _s5_post(y5.reshape(nj, S5_T, e, ns), hg5, w_glu, 0.5 * s5_b_glu[0].reshape(1, e))

    wg_all = _gate_tiles(0.5 * lru_w_a[0], 0.5 * lru_w_x[0])
    gb_all = 0.5 * jnp.stack([lru_b_a[0, 0], lru_b_x[0, 0], lru_b_a[0, 1], lru_b_x[0, 1]])
    cw = lru_conv_w[0]
    cb = lru_conv_b[0].reshape(1, e)
    lam = lru_lam[0]
    ctx_p = jnp.zeros((8, nctx_tok, d), F32).at[:bsz].set(ctx).transpose(1, 0, 2).reshape(nctx_tok * 8, d)
    ff, fr = _lru_ctx(ctx_p, gs[bsz:bsz + 1], sh[bsz:bsz + 1], w_ulru, cw, cb, wg_all, gb_all, lam)
    x4 = x.reshape(bsz, nrow, GRID_W, d)
    lar, br, hloc, pcum, hf, pf, pr, hr = _lru_pass_a(x4, gs, sh, w_ulru, cw, cb, wg_all, gb_all, lam)
    hs_f, hs_r = _lru_stitch(pf, hf, pr, hr, jnp.repeat(ff[:bsz], GRID_W, axis=0),
                             jnp.repeat(fr[:bsz], GRID_W, axis=0), GRID_W)
    out4 = _lru_rev_out(lar, br, hloc, pcum, x4, m5.reshape(bsz, nrow, GRID_W, e), gs, sh, gt, hs_f, hs_r,
                        w_glru, wo, final_g.reshape(1, d))
    return out4.reshape(bsz, seq, d)
```

```python
import functools
import math

import jax
import jax.numpy as jnp
from jax import lax
from jax.experimental import pallas as pl
from jax.experimental.pallas import tpu as pltpu

F32 = jnp.float32
BF16 = jnp.bfloat16

EPS = 1e-6
TINY = 1e-30
GRID_W = 64
LRU_C = 8.0
S5_H = 16
S5_N = 64
S5_T = 16
S5_SEG = 256
S5_TB = 8
S5_PB = 16
LRU_HEAD = 64
GATE_TILE = 256
LRU_RB = 2
LRU_WARM = 2
LRU_RC = 32
VMEM_LIMIT = 56 * 1024 * 1024


def _cparams(sem):
    return pltpu.CompilerParams(dimension_semantics=sem, vmem_limit_bytes=VMEM_LIMIT)


def _sigmoid(z):
    return 0.5 * (jnp.tanh(0.5 * z) + 1.0)


def _gelu_tanh(y):
    return 0.5 * y * (1.0 + jnp.tanh(math.sqrt(2.0 / math.pi) * (y + 0.044715 * (y * y * y))))


def _norm_mod(x, gs, sh):
    ms = jnp.mean(x * x, axis=-1, keepdims=True)
    return x * lax.rsqrt(ms + EPS) * gs + sh


def _mod_kernel(c_ref, w_ref, b_ref, o_ref):
    c = c_ref[...]
    s = c * _sigmoid(c)
    o_ref[...] = jnp.dot(s, w_ref[...], preferred_element_type=F32) + b_ref[...]


def _modulation(c8, w_mod, b_mod):
    d = c8.shape[1]
    n = w_mod.shape[1]
    nb = n // d
    return pl.pallas_call(
        _mod_kernel,
        grid=(nb,),
        in_specs=[pl.BlockSpec((8, d), lambda i: (0, 0)),
                  pl.BlockSpec((d, d), lambda i: (0, i)),
                  pl.BlockSpec((1, d), lambda i: (0, i))],
        out_specs=pl.BlockSpec((8, d), lambda i: (0, i)),
        out_shape=jax.ShapeDtypeStruct((8, n), F32),
        compiler_params=_cparams(("arbitrary",)),
    )(c8, w_mod, b_mod.reshape(1, n))


def _perm_matrix(n_a, n_b):
    n = n_a * n_b
    r = lax.broadcasted_iota(jnp.int32, (n, n), 0)
    c = lax.broadcasted_iota(jnp.int32, (n, n), 1)
    return ((r // n_b == c % n_a) & (r % n_b == c // n_a)).astype(BF16)


def _s5_inproj_kernel(x_ref, gs_ref, sh_ref, wt_ref, wg_ref, o_ref, hg_ref, xs_ref, *, nbatch):
    ns, tb, d = x_ref.shape
    groups = o_ref.shape[0]
    rows_b = (ns // nbatch) * tb
    blk = S5_PB * tb
    perm = _perm_matrix(tb, S5_PB)
    x = x_ref[...].reshape(ns * tb, d)
    xn = jnp.concatenate(
        [_norm_mod(x[b * rows_b:(b + 1) * rows_b], gs_ref[b:b + 1], sh_ref[b:b + 1]).astype(BF16)
         for b in range(nbatch)], axis=0)
    for k in range(ns // S5_PB):
        q = jnp.dot(perm, xn[k * blk:(k + 1) * blk], preferred_element_type=F32).astype(BF16)
        for t in range(tb):
            xs_ref[t * ns + k * S5_PB:t * ns + (k + 1) * S5_PB, :] = q[t * S5_PB:(t + 1) * S5_PB]
    wt = wt_ref[...]
    for t in range(0, tb, 2):
        ut = lax.dot_general(wt, xs_ref[t * ns:(t + 2) * ns, :], (((1,), (1,)), ((), ())),
                             preferred_element_type=F32)
        o_ref[:, t * S5_H:(t + 1) * S5_H, :] = ut[:, :ns].reshape(groups, S5_H, ns).astype(BF16)
        o_ref[:, (t + 1) * S5_H:(t + 2) * S5_H, :] = ut[:, ns:].reshape(groups, S5_H, ns).astype(BF16)
    hg = jnp.dot(xs_ref[...], wg_ref[...], preferred_element_type=F32)
    hg_ref[...] = hg.astype(BF16).reshape(tb, ns, hg.shape[1])


def _s5_inproj(x3, gs8, sh8, wt, wg, nbatch):
    ns, seg, d = x3.shape
    e = wt.shape[0]
    groups = e // S5_H
    nj = seg // S5_T
    per = S5_T // S5_TB
    return pl.pallas_call(
        functools.partial(_s5_inproj_kernel, nbatch=nbatch),
        grid=(seg // S5_TB,),
        in_specs=[pl.BlockSpec((ns, S5_TB, d), lambda i: (0, i, 0)),
                  pl.BlockSpec((8, d), lambda i: (0, 0)),
                  pl.BlockSpec((8, d), lambda i: (0, 0)),
                  pl.BlockSpec((e, d), lambda i: (0, 0)),
                  pl.BlockSpec((d, e), lambda i: (0, 0))],
        out_specs=[pl.BlockSpec((groups, S5_TB * S5_H, ns), lambda i: (0, i % per, i // per)),
                   pl.BlockSpec((S5_TB, ns, e), lambda i: (i, 0, 0))],
        out_shape=[jax.ShapeDtypeStruct((groups, S5_T * S5_H, nj * ns), BF16),
                   jax.ShapeDtypeStruct((seg, ns, e), BF16)],
        scratch_shapes=[pltpu.VMEM((S5_TB * ns, d), BF16)],
        compiler_params=_cparams(("arbitrary",)),
        name="s5_inproj",
    )(x3, gs8, sh8, wt, wg)


def _cmul_add(dr, di, hr, hi, sr, si):
    return dr * hr - di * hi + sr, dr * hi + di * hr + si


def _lane_scan(vr, vi, dr, di, slot, count, reverse):
    n = vr.shape[1]
    k = 1
    while k < count:
        if reverse:
            sr = pltpu.roll(vr, n - k, 1)
            si = pltpu.roll(vi, n - k, 1)
            keep = slot < count - k
        else:
            sr = pltpu.roll(vr, k, 1)
            si = pltpu.roll(vi, k, 1)
            keep = slot >= k
        sr = jnp.where(keep, sr, 0.0)
        si = jnp.where(keep, si, 0.0)
        vr, vi = _cmul_add(dr, di, sr, si, vr, vi)
        dr, di = dr * dr - di * di, 2.0 * dr * di
        k *= 2
    return vr, vi


def _csq(r, i):
    return r * r - i * i, 2.0 * r * i


def _cpow_bits(e, pows):
    pr = jnp.ones(e.shape, F32)
    pi = jnp.zeros(e.shape, F32)
    for bit, (qr, qi) in pows:
        on = (e & bit) != 0
        pr, pi = jnp.where(on, pr * qr - pi * qi, pr), jnp.where(on, pr * qi + pi * qr, pi)
    return pr, pi


def _s5_discretize(ar, ai, ls):
    step = jnp.exp(ls)
    mag = jnp.exp(ar * step)
    pr = mag * jnp.cos(ai * step)
    pi = mag * jnp.sin(ai * step)
    den = ar * ar + ai * ai
    nr = pr - 1.0
    return pr, pi, (nr * ar + pi * ai) / den, (pi * ar - nr * ai) / den


def _s5_operators(pcol_ref, prow_ref, bt_ref, ct1_ref, ct2_ref, dsk_ref):
    n, t_len, th = S5_N, S5_T, S5_T * S5_H
    hp = lax.Precision.HIGHEST
    lane_t = lax.broadcasted_iota(jnp.int32, (1, th), 1) // S5_H
    bits = [1 << b for b in range(t_len.bit_length() - 1)]

    blocks, bbs, decay = [], [], []
    for d in range(2):
        pr, pi, cr, ci = _s5_discretize(pcol_ref[3 * d], pcol_ref[3 * d + 1], pcol_ref[3 * d + 2])
        pows, q = [], (pr, pi)
        for bit in bits:
            pows.append((bit, q))
            q = _csq(*q)
        decay.append(q)
        btr = bt_ref[2 * d * n:(2 * d + 1) * n, :]
        bti = bt_ref[(2 * d + 1) * n:(2 * d + 2) * n, :]
        bbr = cr * btr - ci * bti
        bbi = cr * bti + ci * btr
        e = (t_len - 1 - lane_t) if d == 0 else lane_t
        wr, wi = _cpow_bits(jnp.broadcast_to(e, (n, th)), pows)
        blocks += [wr * bbr - wi * bbi, wr * bbi + wi * bbr]
        bbs.append(jnp.concatenate([bbr, bbi], axis=0))
    bpow = jnp.concatenate(blocks, axis=0).astype(BF16)

    qr, qi, _, _ = _s5_discretize(prow_ref[0:1], prow_ref[1:2], prow_ref[2:3])
    pows, q = [], (qr, qi)
    for bit in bits:
        pows.append((bit, q))
        q = _csq(*q)
    q_t = q
    row_t = lax.broadcasted_iota(jnp.int32, (th, 1), 0) // S5_H
    is_f = lax.broadcasted_iota(jnp.int32, (1, 4 * n), 1) < 2 * n
    e_k = jnp.where(is_f, row_t, (t_len - row_t) % t_len)
    kr, ki = _cpow_bits(e_k, pows)
    nr = jnp.where(is_f, kr * qr - ki * qi, jnp.where(row_t == 0, q_t[0], kr))
    ni = jnp.where(is_f, kr * qi + ki * qr, jnp.where(row_t == 0, q_t[1], ki))
    ct1 = jnp.concatenate([ct1_ref[...]] * t_len, axis=0)
    ct2 = jnp.concatenate([ct2_ref[...]] * t_len, axis=0)
    cpow = (ct1 * nr + ct2 * ni).astype(BF16)
    qk = ct1 * kr + ct2 * ki
    mf = jnp.dot(qk[:, :2 * n], bbs[0], precision=hp, preferred_element_type=F32)
    mr = jnp.dot(qk[:, 2 * n:], bbs[1], precision=hp, preferred_element_type=F32)
    m = (jnp.where(row_t <= t_len - 1 - lane_t, mf, 0.0)
         + jnp.where((row_t == 0) | (row_t >= t_len - lane_t), mr, 0.0))
    for bit in bits:
        m = jnp.where((lane_t & bit) != 0, pltpu.roll(m, S5_H * bit, 0), m)
    row = lax.broadcasted_iota(jnp.int32, (th, th), 0)
    col = lax.broadcasted_iota(jnp.int32, (th, th), 1)
    toep = (m + jnp.where(row == col, dsk_ref[...], 0.0)).astype(BF16)
    return toep, bpow, cpow, decay


def _s5_core_kernel(xt_ref, xc_ref, pcol_ref, pp_ref, bt_ref, o_ref, s_ref, *, nseg, nctx):
    n = S5_N
    h = S5_H
    ns = xc_ref.shape[-1]
    nj = xt_ref.shape[-1] // ns
    toep, bpow, cpow, decay = _s5_operators(pcol_ref, pp_ref.at[2 * h:2 * h + 3], bt_ref, pp_ref.at[0:h],
                                            pp_ref.at[h:2 * h], pp_ref.at[2 * h + 3:2 * h + 4])
    dfr = jnp.broadcast_to(decay[0][0], (n, ns))
    dfi = jnp.broadcast_to(decay[0][1], (n, ns))
    drr = jnp.broadcast_to(decay[1][0], (n, ns))
    dri = jnp.broadcast_to(decay[1][1], (n, ns))
    slot = lax.broadcasted_iota(jnp.int32, (n, ns), 1) % nseg

    sc = jnp.dot(bpow, xc_ref[...], preferred_element_type=F32)
    valid = slot < nctx
    cfr, cfi = _lane_scan(jnp.where(valid, sc[0:n], 0.0), jnp.where(valid, sc[n:2 * n], 0.0),
                          dfr, dfi, slot, nseg, False)
    crr, cri = _lane_scan(jnp.where(valid, sc[2 * n:3 * n], 0.0), jnp.where(valid, sc[3 * n:4 * n], 0.0),
                          drr, dri, slot, nseg, True)

    step = 2 if nj % 2 == 0 else 1
    for j in range(0, nj, step):
        cols = slice(j * ns, (j + step) * ns)
        s_ref[:, cols] = jnp.dot(bpow, xt_ref[:, cols], preferred_element_type=F32)

    zero = jnp.zeros((n, ns), F32)
    efr, efi, err, eri = zero, zero, zero, zero
    for j in range(nj):
        cols = slice(j * ns, (j + 1) * ns)
        efr, efi = _cmul_add(dfr, dfi, efr, efi, s_ref[0:n, cols], s_ref[n:2 * n, cols])
    for j in range(nj - 1, -1, -1):
        cols = slice(j * ns, (j + 1) * ns)
        err, eri = _cmul_add(drr, dri, err, eri, s_ref[2 * n:3 * n, cols], s_ref[3 * n:4 * n, cols])

    sfr, sfi, srr, sri = dfr, dfi, drr, dri
    k = 1
    while k < nj:
        sfr, sfi = sfr * sfr - sfi * sfi, 2.0 * sfr * sfi
        srr, sri = srr * srr - sri * sri, 2.0 * srr * sri
        k *= 2

    first = slot == 0
    last = slot == nseg - 1
    vfr = jnp.where(first, pltpu.roll(cfr, ns - (nctx - 1), 1) if nctx > 1 else cfr, pltpu.roll(efr, 1, 1))
    vfi = jnp.where(first, pltpu.roll(cfi, ns - (nctx - 1), 1) if nctx > 1 else cfi, pltpu.roll(efi, 1, 1))
    vrr = jnp.where(last, pltpu.roll(crr, nseg - 1, 1) if nseg > 1 else crr, pltpu.roll(err, ns - 1, 1))
    vri = jnp.where(last, pltpu.roll(cri, nseg - 1, 1) if nseg > 1 else cri, pltpu.roll(eri, ns - 1, 1))
    hfr, hfi = _lane_scan(vfr, vfi, sfr, sfi, slot, nseg, False)
    hrr, hri = _lane_scan(vrr, vri, srr, sri, slot, nseg, True)

    for j in range(nj):
        cols = slice(j * ns, (j + 1) * ns)
        nr, ni = _cmul_add(dfr, dfi, hfr, hfi, s_ref[0:n, cols], s_ref[n:2 * n, cols])
        s_ref[0:n, cols] = hfr
        s_ref[n:2 * n, cols] = hfi
        hfr, hfi = nr, ni
    for j in range(nj - 1, -1, -1):
        cols = slice(j * ns, (j + 1) * ns)
        nr, ni = _cmul_add(drr, dri, hrr, hri, s_ref[2 * n:3 * n, cols], s_ref[3 * n:4 * n, cols])
        s_ref[2 * n:3 * n, cols] = hrr
        s_ref[3 * n:4 * n, cols] = hri
        hrr, hri = nr, ni

    for j in range(0, nj, step):
        cols = slice(j * ns, (j + step) * ns)
        y = (jnp.dot(toep, xt_ref[:, cols], preferred_element_type=F32)
             + jnp.dot(cpow, s_ref[:, cols].astype(BF16), preferred_element_type=F32))
        for q in range(step):
            o_ref[j + q] = y[:, q * ns:(q + 1) * ns].reshape(S5_T, S5_H, ns).astype(BF16)


def _s5_core(xt, xct, params, nseg, nctx):
    groups, rows, lanes = xt.shape
    ns = xct.shape[-1]
    nj = lanes // ns
    kern = functools.partial(_s5_core_kernel, nseg=nseg, nctx=nctx)
    per_group = lambda a: pl.BlockSpec((None,) + a.shape[1:], lambda g, nd=a.ndim: (g,) + (0,) * (nd - 1))
    return pl.pallas_call(
        kern,
        grid=(groups,),
        in_specs=[per_group(xt), per_group(xct)] + [per_group(a) for a in params],
        out_specs=pl.BlockSpec((nj, S5_T, None, S5_H, ns), lambda g: (0, 0, g, 0, 0)),
        out_shape=jax.ShapeDtypeStruct((nj, S5_T, groups, S5_H, ns), BF16),
        scratch_shapes=[pltpu.VMEM((rows, lanes), F32)],
        compiler_params=_cparams(("arbitrary",)),
        name="s5_core",
    )(xt, xct, *params)


def _s5_param_layout(a_re, a_im, log_step, b_re, b_im, c_re, c_im, d_skip):
    ndir, groups, n = a_re.shape
    h = b_re.shape[-1]
    ls = jnp.broadcast_to(log_step[..., None], a_re.shape)
    pcol = jnp.stack([a_re[0], a_im[0], ls[0], a_re[1], a_im[1], ls[1]], axis=1)[..., None]
    lay = lambda v: jnp.concatenate([v[0], v[0], v[1], v[1]], axis=-1)
    prow = jnp.stack([lay(a_re), lay(a_im), lay(ls)], axis=1)
    tile_t = lambda v: jnp.tile(v, (1, 1, S5_T))
    bt = jnp.concatenate([tile_t(b_re[0]), tile_t(b_im[0]), tile_t(b_re[1]), tile_t(b_im[1])], axis=1)
    ct1 = jnp.concatenate([c_re[0], -c_im[0], c_re[1], -c_im[1]], axis=-1)
    ct2 = jnp.concatenate([-c_im[0], -c_re[0], -c_im[1], -c_re[1]], axis=-1)
    dsk = jnp.tile(d_skip.reshape(groups, 1, h), (1, 1, S5_T))
    pad = jnp.zeros((groups, (-(2 * h + 4)) % 8, S5_T * h), F32)
    pp = jnp.concatenate([ct1, ct2, prow, dsk, pad], axis=1)
    return tuple(v.astype(F32) for v in (pcol, pp, bt))


def _s5_post_kernel(y_ref, hg_ref, wglu_ref, bglu_ref, o_ref, m_ref):
    q = pl.program_id(1)
    tb, e, ns = y_ref.shape
    ys = jnp.concatenate([y_ref[t].astype(F32).T for t in range(tb)], axis=0)
    yg = _gelu_tanh(ys)
    zh = jnp.dot(yg.astype(BF16), wglu_ref[...], preferred_element_type=F32) + bglu_ref[...]
    hg = hg_ref[...].reshape(tb * ns, e).astype(F32)
    m = ((yg * (0.5 * jnp.tanh(zh) + 0.5)) * (hg * jnp.tanh(hg) + hg)).astype(BF16)
    nq = S5_T // tb
    for half in range(nq):
        @pl.when(q == half)
        def _(half=half):
            m_ref[half * tb * ns:(half + 1) * tb * ns, :] = m

    @pl.when(q == nq - 1)
    def _():
        perm = _perm_matrix(S5_PB, S5_T)
        for k in range(ns // S5_PB):
            g = jnp.concatenate([m_ref[t * ns + k * S5_PB:t * ns + (k + 1) * S5_PB, :] for t in range(S5_T)], axis=0)
            r = jnp.dot(perm, g, preferred_element_type=F32).astype(BF16)
            o_ref[k * S5_PB:(k + 1) * S5_PB] = r.reshape(S5_PB, S5_T, e)


def _s5_post(y4, hg, wglu, bglu):
    nj, t_all, e, ns = y4.shape
    nt = t_all // S5_TB
    return pl.pallas_call(
        _s5_post_kernel,
        grid=(nj, nt),
        in_specs=[pl.BlockSpec((None, S5_TB, e, ns), lambda j, q: (j, q, 0, 0)),
                  pl.BlockSpec((S5_TB, ns, e), lambda j, q: (j * nt + q, 0, 0)),
                  pl.BlockSpec((e, e), lambda j, q: (0, 0)),
                  pl.BlockSpec((1, e), lambda j, q: (0, 0))],
        out_specs=pl.BlockSpec((ns, S5_T, e), lambda j, q: (0, j, 0)),
        out_shape=jax.ShapeDtypeStruct((ns, nj * S5_T, e), BF16),
        scratch_shapes=[pltpu.VMEM((S5_T * ns, e), BF16)],
        compiler_params=_cparams(("arbitrary", "arbitrary")),
        name="s5_post",
    )(y4, hg, wglu, bglu)


def _softplus(z):
    return jnp.maximum(z, 0.0) + jnp.log1p(jnp.exp(-jnp.abs(z)))


def _gate_matmul(xc, wg_ref):
    xb = xc.astype(BF16)
    nq = wg_ref.shape[0]
    return [jnp.dot(xb[:, q * GATE_TILE:(q + 1) * GATE_TILE], wg_ref[q], preferred_element_type=F32)
            for q in range(nq)]


def _gate_cols(zs, k):
    return jnp.concatenate([z[:, k * GATE_TILE:(k + 1) * GATE_TILE] for z in zs], axis=1)


def _lru_decay_scale(lam_row):
    return (-0.5 * LRU_C * math.log2(math.e)) * _softplus(-lam_row)


def _lru_gate_math(hx, za, zx, ba, bx, c2):
    ta = jnp.tanh(za + ba)
    tx = jnp.tanh(zx + bx)
    la = ta * c2 + c2
    a = jnp.exp2(la)
    om = 1.0 - a * a
    return la, a, (om * lax.rsqrt(jnp.maximum(om, TINY))) * (tx * hx + hx)


def _lru_coeffs(hx, zs, k, gbh, c2):
    return _lru_gate_math(hx, _gate_cols(zs, 2 * k), _gate_cols(zs, 2 * k + 1),
                          gbh[2 * k:2 * k + 1], gbh[2 * k + 1:2 * k + 2], c2)[1:]


def _norm_mod_rows(x_ref, gs_ref, sh_ref, dst_ref):
    nb, w, _ = x_ref.shape
    for b in range(nb):
        dst_ref[b * w:(b + 1) * w, :] = _norm_mod(x_ref[b], gs_ref[b:b + 1], sh_ref[b:b + 1]).astype(BF16)


def _lru_ctx_kernel(x_ref, gs_ref, sh_ref, wu_ref, cw_ref, cb_ref, wg_ref, gb_ref, lam_ref,
                    ff_ref, fr_ref, af_s, bf_s, ar_s, br_s):
    rows = x_ref.shape[0]
    npos = rows // 8
    xn = _norm_mod(x_ref[...], gs_ref[...], sh_ref[...]).astype(BF16)
    u = jnp.dot(xn, wu_ref[...], preferred_element_type=F32)
    pos = lax.broadcasted_iota(jnp.int32, u.shape, 0) // 8
    um1 = jnp.where(pos >= 1, pltpu.roll(u, 8, 0), 0.0)
    up1 = jnp.where(pos < npos - 1, pltpu.roll(u, rows - 8, 0), 0.0)
    up2 = jnp.where(pos < npos - 2, pltpu.roll(u, rows - 16, 0), 0.0)
    cw = cw_ref[...]
    xc = cw[0:1] * um1 + cw[1:2] * u + cw[2:3] * up1 + cw[3:4] * up2 + cb_ref[...]
    zs = _gate_matmul(xc, wg_ref)
    gbh = gb_ref[...]
    c2 = _lru_decay_scale(lam_ref[...])
    hx = 0.5 * xc
    a, b = _lru_coeffs(hx, zs, 0, gbh, c2[0:1])
    af_s[...] = a
    bf_s[...] = b
    a, b = _lru_coeffs(hx, zs, 1, gbh, c2[1:2])
    ar_s[...] = a
    br_s[...] = b

    def body(p, carry):
        hf, hr, pr = carry
        r0 = pl.multiple_of(p * 8, 8)
        hf = af_s[pl.ds(r0, 8), :] * hf + bf_s[pl.ds(r0, 8), :]
        hr = hr + pr * br_s[pl.ds(r0, 8), :]
        pr = pr * ar_s[pl.ds(r0, 8), :]
        return hf, hr, pr

    zero = jnp.zeros((8, u.shape[1]), F32)
    hf, hr, _ = lax.fori_loop(0, npos, body, (zero, zero, zero + 1.0))
    ff_ref[...] = hf
    fr_ref[...] = hr


def _lru_ctx(xp, gs, sh, wu, cw, cb, wg, gb, lam):
    rows, d = xp.shape
    e = wu.shape[1]
    nq = e // GATE_TILE
    full = lambda a: pl.BlockSpec(a.shape, lambda q, nd=a.ndim: (0,) * nd)
    cols = lambda a: pl.BlockSpec((a.shape[0], GATE_TILE), lambda q: (0, q))
    return pl.pallas_call(
        _lru_ctx_kernel,
        grid=(nq,),
        in_specs=[full(xp), full(gs), full(sh), cols(wu), cols(cw), cols(cb),
                  pl.BlockSpec((1,) + wg.shape[1:], lambda q: (q, 0, 0)), cols(gb), cols(lam)],
        out_specs=[pl.BlockSpec((8, GATE_TILE), lambda q: (0, q)), pl.BlockSpec((8, GATE_TILE), lambda q: (0, q))],
        out_shape=[jax.ShapeDtypeStruct((8, e), F32), jax.ShapeDtypeStruct((8, e), F32)],
        scratch_shapes=[pltpu.VMEM((rows, GATE_TILE), F32) for _ in range(4)],
        compiler_params=_cparams(("arbitrary",)),
    )(xp, gs, sh, wu, cw, cb, wg, gb, lam)


def _lru_pass_a_kernel(xa_ref, xb_ref, gs_ref, sh_ref, wu_ref, cw_ref, cb_ref, wg_ref, gb_ref, lam_ref,
                       lar_ref, br_ref, hl_ref, pc_ref, hs_ref, ps_ref, pr_ref, hr_ref,
                       ring_ref, xn_ref, xc_ref, xcb_ref, z_ref, *, nrow):
    i = pl.program_id(0)
    nb, w, d = xa_ref.shape
    rows = nb * w
    nq = wg_ref.shape[0]
    gt = GATE_TILE

    @pl.when(i == 0)
    def _():
        ring_ref[...] = jnp.zeros_like(ring_ref)

    @pl.when(i <= LRU_WARM)
    def _():
        hs_ref[...] = jnp.zeros_like(hs_ref)
        ps_ref[...] = jnp.ones_like(ps_ref)
        hr_ref[...] = jnp.zeros_like(hr_ref)
        pr_ref[...] = jnp.ones_like(pr_ref)

    r = lax.broadcasted_iota(jnp.int32, (rows, rows), 0)
    c = lax.broadcasted_iota(jnp.int32, (rows, rows), 1)
    same_batch = r // w == c // w
    shifted = []
    for k, x_ref in enumerate((xa_ref, xb_ref)):
        _norm_mod_rows(x_ref, gs_ref, sh_ref, xn_ref.at[k * rows:(k + 1) * rows])
        s = 2 * i + k
        off = jnp.where(s == 0, -1, jnp.where(s > nrow, 1, 0))
        mix = ((c == r + off) & same_batch).astype(BF16)
        shifted.append(jnp.dot(mix, xn_ref[k * rows:(k + 1) * rows, :], preferred_element_type=F32).astype(BF16))
    u = jnp.dot(jnp.concatenate(shifted, axis=0), wu_ref[...], preferred_element_type=F32)
    ring_ref[(2 * i) & 7] = u[0:rows]
    ring_ref[(2 * i + 1) & 7] = u[rows:2 * rows]

    cw = cw_ref[...]
    cb = cb_ref[...]
    for k in range(LRU_RB):
        sl = [(2 * i - 2 * LRU_WARM + k + t) & 7 for t in range(4)]
        for r0 in range(0, rows, LRU_RC):
            rs = slice(r0, r0 + LRU_RC)
            ks = slice(k * rows + r0, k * rows + r0 + LRU_RC)
            xc = (cw[0:1] * ring_ref[sl[0], rs, :] + cw[1:2] * ring_ref[sl[1], rs, :]
                  + cw[2:3] * ring_ref[sl[2], rs, :] + cw[3:4] * ring_ref[sl[3], rs, :] + cb)
            xc_ref[ks, :] = xc
            xcb_ref[ks, :] = xc.astype(BF16)
    for q in range(nq):
        z_ref[q] = jnp.dot(xcb_ref[:, q * gt:(q + 1) * gt], wg_ref[q], preferred_element_type=F32)
    gbh = gb_ref[...]
    c2 = _lru_decay_scale(lam_ref[...])
    for k in range(LRU_RB):
        for q in range(nq):
            cs = slice(q * gt, (q + 1) * gt)
            for r0 in range(0, rows, LRU_RC):
                rs = slice(r0, r0 + LRU_RC)
                ks = slice(k * rows + r0, k * rows + r0 + LRU_RC)
                hx = 0.5 * xc_ref[ks, cs]
                _, af, bf = _lru_gate_math(hx, z_ref[q, ks, 0:gt], z_ref[q, ks, gt:2 * gt],
                                           gbh[0:1, cs], gbh[1:2, cs], c2[0:1, cs])
                lar, ar, br = _lru_gate_math(hx, z_ref[q, ks, 2 * gt:3 * gt], z_ref[q, ks, 3 * gt:4 * gt],
                                             gbh[2:3, cs], gbh[3:4, cs], c2[1:2, cs])
                h = af * hs_ref[rs, cs] + bf
                p = ps_ref[rs, cs] * af
                hs_ref[rs, cs] = h
                ps_ref[rs, cs] = p
                hl_ref[k, rs, cs] = h.astype(BF16)
                pc_ref[k, rs, cs] = p.astype(BF16)
                lar_ref[k, rs, cs] = lar.astype(BF16)
                br_ref[k, rs, cs] = br.astype(BF16)
                prod = pr_ref[rs, cs]
                hr_ref[rs, cs] = hr_ref[rs, cs] + prod * br
                pr_ref[rs, cs] = prod * ar


def _lru_pass_a(x4, gs8, sh8, wu, cw, cb, wg, gb, lam):
    nb, nrow, w, d = x4.shape
    e = wu.shape[1]
    rows = nb * w
    rb = LRU_RB
    kern = functools.partial(_lru_pass_a_kernel, nrow=nrow)
    const = lambda a: pl.BlockSpec(a.shape, lambda i, nd=a.ndim: (0,) * nd)
    acc = pl.BlockSpec((rows, e), lambda i: (0, 0))
    acc_shape = jax.ShapeDtypeStruct((rows, e), F32)
    per_row = pl.BlockSpec((rb, rows, e), lambda i: (jnp.maximum(i - LRU_WARM, 0), 0, 0))
    per_row_shape = jax.ShapeDtypeStruct((nrow, rows, e), BF16)
    x_row = lambda first: pl.BlockSpec((nb, None, w, d), lambda i: (0, (rb * i + first) % nrow, 0, 0))
    return pl.pallas_call(
        kern,
        grid=(nrow // rb + LRU_WARM,),
        in_specs=[x_row(nrow - 1), x_row(0),
                  const(gs8), const(sh8), const(wu), const(cw), const(cb), const(wg), const(gb), const(lam)],
        out_specs=[per_row] * 4 + [acc] * 4,
        out_shape=[per_row_shape] * 4 + [acc_shape] * 4,
        scratch_shapes=[pltpu.VMEM((8, rows, e), F32), pltpu.VMEM((rb * rows, d), BF16),
                        pltpu.VMEM((rb * rows, e), F32), pltpu.VMEM((rb * rows, e), BF16),
                        pltpu.VMEM((wg.shape[0], rb * rows, wg.shape[2]), F32)],
        compiler_params=_cparams(("arbitrary",)),
        name="lru_pass_a",
    )(x4, x4, gs8, sh8, wu, cw, cb, wg, gb, lam)


def _lru_stitch_kernel(pf_ref, hf_ref, pr_ref, hr_ref, ff_ref, fr_ref, sf_ref, sr_ref, *, w):
    rows = pf_ref.shape[0]
    col = lax.broadcasted_iota(jnp.int32, pf_ref.shape, 0) % w
    p, h = pf_ref[...], hf_ref[...]
    k = 1
    while k < w:
        keep = col >= k
        ps = jnp.where(keep, pltpu.roll(p, k, 0), 1.0)
        hs = jnp.where(keep, pltpu.roll(h, k, 0), 0.0)
        p, h = p * ps, p * hs + h
        k *= 2
    fin = ff_ref[...]
    sf_ref[...] = jnp.where(col == 0, fin, pltpu.roll(p, 1, 0) * fin + pltpu.roll(h, 1, 0))
    p, h = pr_ref[...], hr_ref[...]
    k = 1
    while k < w:
        keep = col < w - k
        ps = jnp.where(keep, pltpu.roll(p, rows - k, 0), 1.0)
        hs = jnp.where(keep, pltpu.roll(h, rows - k, 0), 0.0)
        p, h = p * ps, p * hs + h
        k *= 2
    fin = fr_ref[...]
    sr_ref[...] = jnp.where(col == w - 1, fin, pltpu.roll(p, rows - 1, 0) * fin + pltpu.roll(h, rows - 1, 0))


def _lru_stitch(pf, hf, pr, hr, ff_rows, fr_rows, w):
    rows, e = pf.shape
    big = pl.BlockSpec((rows, e), lambda i: (0, 0))
    shp = jax.ShapeDtypeStruct((rows, e), F32)
    return pl.pallas_call(
        functools.partial(_lru_stitch_kernel, w=w),
        grid=(1,),
        in_specs=[big] * 6,
        out_specs=[big, big],
        out_shape=[shp, shp],
        compiler_params=_cparams(("arbitrary",)),
        name="lru_stitch",
    )(pf, hf, pr, hr, ff_rows, fr_rows)


def _lru_rev_out_kernel(lar_ref, br_ref, hl_ref, pc_ref, x_ref, m5_ref, gs_ref, sh_ref, gt_ref, hsf_ref, hsr_ref,
                        wgl_ref, wo_ref, fg_ref, o_ref, h_ref, xb_ref, g_ref, mb_ref, mix_ref):
    i = pl.program_id(0)
    nb, nrb, w, d = x_ref.shape
    rows = nb * w
    e = h_ref.shape[-1]

    @pl.when(i == 0)
    def _():
        h_ref[...] = hsr_ref[...]

    for rr in range(nrb):
        for b in range(nb):
            xb_ref[rr * rows + b * w:rr * rows + (b + 1) * w, :] = _norm_mod(
                x_ref[b, rr], gs_ref[b:b + 1], sh_ref[b:b + 1]).astype(BF16)
    g_ref[...] = jnp.dot(xb_ref[...], wgl_ref[...], preferred_element_type=F32)
    for rr in range(nrb - 1, -1, -1):
        for c0 in range(0, e, GATE_TILE):
            cs = slice(c0, c0 + GATE_TILE)
            for r0 in range(0, rows, LRU_RC):
                rs = slice(r0, r0 + LRU_RC)
                gs_rows = slice(rr * rows + r0, rr * rows + r0 + LRU_RC)
                h = jnp.exp2(lar_ref[rr, rs, cs].astype(F32)) * h_ref[rs, cs] + br_ref[rr, rs, cs].astype(F32)
                h_ref[rs, cs] = h
                y = hl_ref[rr, rs, cs].astype(F32) + pc_ref[rr, rs, cs].astype(F32) * hsf_ref[rs, cs] + h
                hg = g_ref[gs_rows, cs]
                mb_ref[gs_rows, cs] = (y * (hg * jnp.tanh(hg) + hg)).astype(BF16)
    m5 = jnp.concatenate([m5_ref[:, rr].reshape(rows, e) for rr in range(nrb)], axis=0)
    mix_ref[...] = (jnp.dot(m5, wo_ref[0:e, :], preferred_element_type=F32)
                    + jnp.dot(mb_ref[...], wo_ref[e:2 * e, :], preferred_element_type=F32))
    fg = fg_ref[...]
    for rr in range(nrb):
        for b in range(nb):
            for r0 in range(0, w, LRU_RC):
                m0 = rr * rows + b * w + r0
                res = x_ref[b, rr, r0:r0 + LRU_RC, :] + gt_ref[b:b + 1] * mix_ref[m0:m0 + LRU_RC, :]
                ms = jnp.mean(res * res, axis=-1, keepdims=True)
                o_ref[b, rr, r0:r0 + LRU_RC, :] = res * lax.rsqrt(ms + EPS) * fg


def _lru_rev_out(lar, br, hloc, pcum, x4, m5, gs8, sh8, gt8, hs_f, hs_r, wgl, wo, fg):
    nb, nrow, w, d = x4.shape
    rows, e = hs_f.shape
    const = lambda a: pl.BlockSpec(a.shape, lambda i, nd=a.ndim: (0,) * nd)
    rb = LRU_RB
    nstep = nrow // rb
    rev3 = pl.BlockSpec((rb, rows, e), lambda i: (nstep - 1 - i, 0, 0))
    rev4 = pl.BlockSpec((nb, rb, w, d), lambda i: (0, nstep - 1 - i, 0, 0))
    rev4e = pl.BlockSpec((nb, rb, w, e), lambda i: (0, nstep - 1 - i, 0, 0))
    return pl.pallas_call(
        _lru_rev_out_kernel,
        grid=(nstep,),
        in_specs=[rev3] * 4 + [rev4, rev4e, const(gs8), const(sh8), const(gt8), const(hs_f), const(hs_r),
                               const(wgl), const(wo), const(fg)],
        out_specs=rev4,
        out_shape=jax.ShapeDtypeStruct((nb, nrow, w, d), F32),
        scratch_shapes=[pltpu.VMEM((rows, e), F32), pltpu.VMEM((rb * rows, d), BF16),
                        pltpu.VMEM((rb * rows, e), F32), pltpu.VMEM((rb * rows, e), BF16),
                        pltpu.VMEM((rb * rows, d), F32)],
        compiler_params=_cparams(("arbitrary",)),
        name="lru_rev_out",
    )(lar, br, hloc, pcum, x4, m5, gs8, sh8, gt8, hs_f, hs_r, wgl, wo, fg)


def _gate_tiles(w_a, w_x):
    ndir, heads, hd, _ = w_a.shape
    per = GATE_TILE // hd
    nq = heads // per

    def tiles(w):
        wq = w.reshape(nq, per, hd, hd)
        eye = jnp.eye(per, dtype=w.dtype)
        blk = wq[:, :, :, None, :] * eye[None, :, None, :, None]
        return blk.reshape(nq, GATE_TILE, GATE_TILE)

    return jnp.concatenate([tiles(w_a[0]), tiles(w_x[0]), tiles(w_a[1]), tiles(w_x[1])], axis=2).astype(BF16)


def kernel(x, c, ctx, c_ctx, w_mod, b_mod, norm_g, w_in, s5_a_re, s5_a_im, s5_log_step, s5_b_re, s5_b_im,
           s5_c_re, s5_c_im, s5_d, s5_w_glu, s5_b_glu, lru_conv_w, lru_conv_b, lru_w_a, lru_b_a, lru_w_x,
           lru_b_x, lru_lam, w_out, final_g):
    bsz, seq, d = x.shape
    nctx_tok = ctx.shape[1]
    e = s5_w_glu.shape[-1]
    assert w_mod.shape[0] == 1, "single-layer block"
    assert bsz <= 8 and seq % S5_SEG == 0 and seq % GRID_W == 0 and nctx_tok % S5_T == 0
    nseg = seq // S5_SEG
    ns = bsz * nseg
    nctx = nctx_tok // S5_T
    assert nctx <= nseg and ns % S5_PB == 0
    nrow = seq // GRID_W
    assert LRU_RB == 2 and nrow % LRU_RB == 0 and nrow >= 4

    c8 = jnp.zeros((8, d), F32).at[:bsz].set(c).at[bsz].set(c_ctx)
    mod = _modulation(c8, w_mod[0], b_mod[0])
    sh, sc, gt = mod[:, :d], mod[:, d:2 * d], mod[:, 2 * d:]
    gs = norm_g[0][None, :] * (1.0 + sc)

    w_in0 = w_in[0]
    wt_us5 = w_in0[:, :e].T.astype(BF16)
    w_gs5 = (0.5 * w_in0[:, e:2 * e]).astype(BF16)
    w_ulru = w_in0[:, 2 * e:3 * e].astype(BF16)
    w_glru = (0.5 * w_in0[:, 3 * e:]).astype(BF16)
    w_glu = (0.5 * s5_w_glu[0]).astype(BF16)
    wo = w_out[0].astype(BF16)

    s5_params = _s5_param_layout(s5_a_re[0], s5_a_im[0], s5_log_step[0], s5_b_re[0], s5_b_im[0],
                                 s5_c_re[0], s5_c_im[0], s5_d[0])
    x3 = x.reshape(ns, S5_SEG, d)
    xt, hg5 = _s5_inproj(x3, gs, sh, wt_us5, w_gs5, bsz)
    ctx_pad = jnp.zeros((bsz, nseg, S5_T, d), F32).at[:, :nctx].set(ctx.reshape(bsz, nctx, S5_T, d))
    gs_c = jnp.broadcast_to(gs[bsz:bsz + 1], (8, d))
    sh_c = jnp.broadcast_to(sh[bsz:bsz + 1], (8, d))
    xct, _ = _s5_inproj(ctx_pad.reshape(ns, S5_T, d), gs_c, sh_c, wt_us5, w_gs5, bsz)
    y5 = _s5_core(xt, xct, s5_params, nseg, nctx)
    nj = S5_SEG // S5_T
    m5 = _s5_post(y5.reshape(nj, S5_T, e, ns), hg5, w_glu, 0.5 * s5_b_glu[0].reshape(1, e))

    wg_all = _gate_tiles(0.5 * lru_w_a[0], 0.5 * lru_w_x[0])
    gb_all = 0.5 * jnp.stack([lru_b_a[0, 0], lru_b_x[0, 0], lru_b_a[0, 1], lru_b_x[0, 1]])
    cw = lru_conv_w[0]
    cb = lru_conv_b[0].reshape(1, e)
    lam = lru_lam[0]
    ctx_p = jnp.zeros((8, nctx_tok, d), F32).at[:bsz].set(ctx).transpose(1, 0, 2).reshape(nctx_tok * 8, d)
    ff, fr = _lru_ctx(ctx_p, gs[bsz:bsz + 1], sh[bsz:bsz + 1], w_ulru, cw, cb, wg_all, gb_all, lam)
    x4 = x.reshape(bsz, nrow, GRID_W, d)
    lar, br, hloc, pcum, hf, pf, pr, hr = _lru_pass_a(x4, gs, sh, w_ulru, cw, cb, wg_all, gb_all, lam)
    hs_f, hs_r = _lru_stitch(pf, hf, pr, hr, jnp.repeat(ff[:bsz], GRID_W, axis=0),
                             jnp.repeat(fr[:bsz], GRID_W, axis=0), GRID_W)
    out4 = _lru_rev_out(lar, br, hloc, pcum, x4, m5.reshape(bsz, nrow, GRID_W, e), gs, sh, gt, hs_f, hs_r,
                        w_glru, wo, final_g.reshape(1, d))
    return out4.reshape(bsz, seq, d)
```

```python
import functools
import math

import jax
import jax.numpy as jnp
from jax import lax
from jax.experimental import pallas as pl
from jax.experimental.pallas import tpu as pltpu

F32 = jnp.float32
BF16 = jnp.bfloat16

EPS = 1e-6
TINY = 1e-30
GRID_W = 64
LRU_C = 8.0
S5_H = 16
S5_N = 64
S5_T = 16
S5_SEG = 256
S5_TB = 8
S5_PB = 16
GATE_TILE = 256
LRU_RB = 2
LRU_WARM = 2
LRU_RC = 32
W_COL_US5, W_COL_GS5, W_COL_ULRU, W_COL_GLRU = 0, 1, 2, 3
V7X_VMEM_BYTES = 64 * 1024 * 1024
VMEM_LIMIT = V7X_VMEM_BYTES * 7 // 8


def _cparams(sem):
    return pltpu.CompilerParams(dimension_semantics=sem, vmem_limit_bytes=VMEM_LIMIT)


def _sigmoid(z):
    return 0.5 * (jnp.tanh(0.5 * z) + 1.0)


def _gelu_tanh(y):
    return 0.5 * y * (1.0 + jnp.tanh(math.sqrt(2.0 / math.pi) * (y + 0.044715 * (y * y * y))))


def _norm_mod(x, gs, sh):
    ms = jnp.mean(x * x, axis=-1, keepdims=True)
    return x * lax.rsqrt(ms + EPS) * gs + sh


def _mod_kernel(c_ref, w_ref, b_ref, o_ref):
    c = c_ref[...]
    s = c * _sigmoid(c)
    o_ref[...] = jnp.dot(s, w_ref[...], preferred_element_type=F32) + b_ref[...]


def _modulation(c8, w_mod, b_mod):
    d = c8.shape[1]
    n = w_mod.shape[1]
    nb = n // d
    return pl.pallas_call(
        _mod_kernel,
        grid=(nb,),
        in_specs=[pl.BlockSpec((8, d), lambda i: (0, 0)),
                  pl.BlockSpec((d, d), lambda i: (0, i)),
                  pl.BlockSpec((1, d), lambda i: (0, i))],
        out_specs=pl.BlockSpec((8, d), lambda i: (0, i)),
        out_shape=jax.ShapeDtypeStruct((8, n), F32),
        compiler_params=_cparams(("arbitrary",)),
    )(c8, w_mod, b_mod.reshape(1, n))


def _perm_matrix(n_a, n_b):
    n = n_a * n_b
    r = lax.broadcasted_iota(jnp.int32, (n, n), 0)
    c = lax.broadcasted_iota(jnp.int32, (n, n), 1)
    return ((r // n_b == c % n_a) & (r % n_b == c // n_a)).astype(BF16)


def _s5_inproj_kernel(x_ref, gs_ref, sh_ref, wt_ref, wg_ref, o_ref, hg_ref, xs_ref, *, nbatch):
    ns, tb, d = x_ref.shape
    groups = o_ref.shape[0]
    rows_b = (ns // nbatch) * tb
    blk = S5_PB * tb
    perm = _perm_matrix(tb, S5_PB)
    x = x_ref[...].reshape(ns * tb, d)
    xn = jnp.concatenate(
        [_norm_mod(x[b * rows_b:(b + 1) * rows_b], gs_ref[b:b + 1], sh_ref[b:b + 1]).astype(BF16)
         for b in range(nbatch)], axis=0)
    for k in range(ns // S5_PB):
        q = jnp.dot(perm, xn[k * blk:(k + 1) * blk], preferred_element_type=F32).astype(BF16)
        for t in range(tb):
            xs_ref[t * ns + k * S5_PB:t * ns + (k + 1) * S5_PB, :] = q[t * S5_PB:(t + 1) * S5_PB]
    wt = wt_ref[...]
    for t in range(0, tb, 2):
        ut = lax.dot_general(wt, xs_ref[t * ns:(t + 2) * ns, :], (((1,), (1,)), ((), ())),
                             preferred_element_type=F32)
        o_ref[:, t * S5_H:(t + 1) * S5_H, :] = ut[:, :ns].reshape(groups, S5_H, ns).astype(BF16)
        o_ref[:, (t + 1) * S5_H:(t + 2) * S5_H, :] = ut[:, ns:].reshape(groups, S5_H, ns).astype(BF16)
    hg = jnp.dot(xs_ref[...], wg_ref[...], preferred_element_type=F32)
    hg_ref[...] = hg.astype(BF16).reshape(tb, ns, hg.shape[1])


def _s5_inproj(x3, gs8, sh8, wt, w_all, nbatch):
    ns, seg, d = x3.shape
    e = wt.shape[0]
    groups = e // S5_H
    nj = seg // S5_T
    per = S5_T // S5_TB
    return pl.pallas_call(
        functools.partial(_s5_inproj_kernel, nbatch=nbatch),
        grid=(seg // S5_TB,),
        in_specs=[pl.BlockSpec((ns, S5_TB, d), lambda i: (0, i, 0)),
                  pl.BlockSpec((8, d), lambda i: (0, 0)),
                  pl.BlockSpec((8, d), lambda i: (0, 0)),
                  pl.BlockSpec((e, d), lambda i: (0, 0)),
                  pl.BlockSpec((d, e), lambda i: (0, W_COL_GS5))],
        out_specs=[pl.BlockSpec((groups, S5_TB * S5_H, ns), lambda i: (0, i % per, i // per)),
                   pl.BlockSpec((S5_TB, ns, e), lambda i: (i, 0, 0))],
        out_shape=[jax.ShapeDtypeStruct((groups, S5_T * S5_H, nj * ns), BF16),
                   jax.ShapeDtypeStruct((seg, ns, e), BF16)],
        scratch_shapes=[pltpu.VMEM((S5_TB * ns, d), BF16)],
        compiler_params=_cparams(("arbitrary",)),
        name="s5_inproj",
    )(x3, gs8, sh8, wt, w_all)


def _cmul_add(dr, di, hr, hi, sr, si):
    return dr * hr - di * hi + sr, dr * hi + di * hr + si


def _lane_scan(vr, vi, dr, di, slot, count, reverse):
    n = vr.shape[1]
    k = 1
    while k < count:
        if reverse:
            sr = pltpu.roll(vr, n - k, 1)
            si = pltpu.roll(vi, n - k, 1)
            keep = slot < count - k
        else:
            sr = pltpu.roll(vr, k, 1)
            si = pltpu.roll(vi, k, 1)
            keep = slot >= k
        sr = jnp.where(keep, sr, 0.0)
        si = jnp.where(keep, si, 0.0)
        vr, vi = _cmul_add(dr, di, sr, si, vr, vi)
        dr, di = dr * dr - di * di, 2.0 * dr * di
        k *= 2
    return vr, vi


def _csq(r, i):
    return r * r - i * i, 2.0 * r * i


def _cpow_bits(e, pows):
    pr = jnp.ones(e.shape, F32)
    pi = jnp.zeros(e.shape, F32)
    for bit, (qr, qi) in pows:
        on = (e & bit) != 0
        pr, pi = jnp.where(on, pr * qr - pi * qi, pr), jnp.where(on, pr * qi + pi * qr, pi)
    return pr, pi


def _s5_discretize(ar, ai, ls):
    step = jnp.exp(ls)
    mag = jnp.exp(ar * step)
    pr = mag * jnp.cos(ai * step)
    pi = mag * jnp.sin(ai * step)
    den = ar * ar + ai * ai
    nr = pr - 1.0
    return pr, pi, (nr * ar + pi * ai) / den, (pi * ar - nr * ai) / den


def _s5_operators(pcol_ref, prow_ref, bt_ref, ct1_ref, ct2_ref, dsk_ref):
    n, t_len, th = S5_N, S5_T, S5_T * S5_H
    hp = lax.Precision.HIGHEST
    lane_t = lax.broadcasted_iota(jnp.int32, (1, th), 1) // S5_H
    bits = [1 << b for b in range(t_len.bit_length() - 1)]

    blocks, bbs, decay = [], [], []
    for d in range(2):
        pr, pi, cr, ci = _s5_discretize(pcol_ref[3 * d], pcol_ref[3 * d + 1], pcol_ref[3 * d + 2])
        pows, q = [], (pr, pi)
        for bit in bits:
            pows.append((bit, q))
            q = _csq(*q)
        decay.append(q)
        btr = bt_ref[2 * d * n:(2 * d + 1) * n, :]
        bti = bt_ref[(2 * d + 1) * n:(2 * d + 2) * n, :]
        bbr = cr * btr - ci * bti
        bbi = cr * bti + ci * btr
        e = (t_len - 1 - lane_t) if d == 0 else lane_t
        wr, wi = _cpow_bits(jnp.broadcast_to(e, (n, th)), pows)
        blocks += [wr * bbr - wi * bbi, wr * bbi + wi * bbr]
        bbs.append(jnp.concatenate([bbr, bbi], axis=0))
    bpow = jnp.concatenate(blocks, axis=0).astype(BF16)

    qr, qi, _, _ = _s5_discretize(prow_ref[0:1], prow_ref[1:2], prow_ref[2:3])
    pows, q = [], (qr, qi)
    for bit in bits:
        pows.append((bit, q))
        q = _csq(*q)
    q_t = q
    row_t = lax.broadcasted_iota(jnp.int32, (th, 1), 0) // S5_H
    is_f = lax.broadcasted_iota(jnp.int32, (1, 4 * n), 1) < 2 * n
    e_k = jnp.where(is_f, row_t, (t_len - row_t) % t_len)
    kr, ki = _cpow_bits(e_k, pows)
    nr = jnp.where(is_f, kr * qr - ki * qi, jnp.where(row_t == 0, q_t[0], kr))
    ni = jnp.where(is_f, kr * qi + ki * qr, jnp.where(row_t == 0, q_t[1], ki))
    ct1 = jnp.concatenate([ct1_ref[...]] * t_len, axis=0)
    ct2 = jnp.concatenate([ct2_ref[...]] * t_len, axis=0)
    cpow = (ct1 * nr + ct2 * ni).astype(BF16)
    qk = ct1 * kr + ct2 * ki
    mf = jnp.dot(qk[:, :2 * n], bbs[0], precision=hp, preferred_element_type=F32)
    mr = jnp.dot(qk[:, 2 * n:], bbs[1], precision=hp, preferred_element_type=F32)
    m = (jnp.where(row_t <= t_len - 1 - lane_t, mf, 0.0)
         + jnp.where((row_t == 0) | (row_t >= t_len - lane_t), mr, 0.0))
    for bit in bits:
        m = jnp.where((lane_t & bit) != 0, pltpu.roll(m, S5_H * bit, 0), m)
    row = lax.broadcasted_iota(jnp.int32, (th, th), 0)
    col = lax.broadcasted_iota(jnp.int32, (th, th), 1)
    toep = (m + jnp.where(row == col, dsk_ref[...], 0.0)).astype(BF16)
    return toep, bpow, cpow, decay


def _s5_core_kernel(xt_ref, xc_ref, pcol_ref, pp_ref, bt_ref, o_ref, s_ref, *, nseg, nctx):
    n = S5_N
    h = S5_H
    ns = xc_ref.shape[-1]
    nj = xt_ref.shape[-1] // ns
    toep, bpow, cpow, decay = _s5_operators(pcol_ref, pp_ref.at[2 * h:2 * h + 3], bt_ref, pp_ref.at[0:h],
                                            pp_ref.at[h:2 * h], pp_ref.at[2 * h + 3:2 * h + 4])
    dfr = jnp.broadcast_to(decay[0][0], (n, ns))
    dfi = jnp.broadcast_to(decay[0][1], (n, ns))
    drr = jnp.broadcast_to(decay[1][0], (n, ns))
    dri = jnp.broadcast_to(decay[1][1], (n, ns))
    slot = lax.broadcasted_iota(jnp.int32, (n, ns), 1) % nseg

    sc = jnp.dot(bpow, xc_ref[...], preferred_element_type=F32)
    valid = slot < nctx
    cfr, cfi = _lane_scan(jnp.where(valid, sc[0:n], 0.0), jnp.where(valid, sc[n:2 * n], 0.0),
                          dfr, dfi, slot, nseg, False)
    crr, cri = _lane_scan(jnp.where(valid, sc[2 * n:3 * n], 0.0), jnp.where(valid, sc[3 * n:4 * n], 0.0),
                          drr, dri, slot, nseg, True)

    step = 2 if nj % 2 == 0 else 1
    for j in range(0, nj, step):
        cols = slice(j * ns, (j + step) * ns)
        s_ref[:, cols] = jnp.dot(bpow, xt_ref[:, cols], preferred_element_type=F32)

    zero = jnp.zeros((n, ns), F32)
    efr, efi, err, eri = zero, zero, zero, zero
    for j in range(nj):
        cols = slice(j * ns, (j + 1) * ns)
        efr, efi = _cmul_add(dfr, dfi, efr, efi, s_ref[0:n, cols], s_ref[n:2 * n, cols])
    for j in range(nj - 1, -1, -1):
        cols = slice(j * ns, (j + 1) * ns)
        err, eri = _cmul_add(drr, dri, err, eri, s_ref[2 * n:3 * n, cols], s_ref[3 * n:4 * n, cols])

    sfr, sfi, srr, sri = dfr, dfi, drr, dri
    k = 1
    while k < nj:
        sfr, sfi = sfr * sfr - sfi * sfi, 2.0 * sfr * sfi
        srr, sri = srr * srr - sri * sri, 2.0 * srr * sri
        k *= 2

    first = slot == 0
    last = slot == nseg - 1
    vfr = jnp.where(first, pltpu.roll(cfr, ns - (nctx - 1), 1) if nctx > 1 else cfr, pltpu.roll(efr, 1, 1))
    vfi = jnp.where(first, pltpu.roll(cfi, ns - (nctx - 1), 1) if nctx > 1 else cfi, pltpu.roll(efi, 1, 1))
    vrr = jnp.where(last, pltpu.roll(crr, nseg - 1, 1) if nseg > 1 else crr, pltpu.roll(err, ns - 1, 1))
    vri = jnp.where(last, pltpu.roll(cri, nseg - 1, 1) if nseg > 1 else cri, pltpu.roll(eri, ns - 1, 1))
    hfr, hfi = _lane_scan(vfr, vfi, sfr, sfi, slot, nseg, False)
    hrr, hri = _lane_scan(vrr, vri, srr, sri, slot, nseg, True)

    for j in range(nj):
        cols = slice(j * ns, (j + 1) * ns)
        nr, ni = _cmul_add(dfr, dfi, hfr, hfi, s_ref[0:n, cols], s_ref[n:2 * n, cols])
        s_ref[0:n, cols] = hfr
        s_ref[n:2 * n, cols] = hfi
        hfr, hfi = nr, ni
    for j in range(nj - 1, -1, -1):
        cols = slice(j * ns, (j + 1) * ns)
        nr, ni = _cmul_add(drr, dri, hrr, hri, s_ref[2 * n:3 * n, cols], s_ref[3 * n:4 * n, cols])
        s_ref[2 * n:3 * n, cols] = hrr
        s_ref[3 * n:4 * n, cols] = hri
        hrr, hri = nr, ni

    for j in range(0, nj, step):
        cols = slice(j * ns, (j + step) * ns)
        y = (jnp.dot(toep, xt_ref[:, cols], preferred_element_type=F32)
             + jnp.dot(cpow, s_ref[:, cols].astype(BF16), preferred_element_type=F32))
        for q in range(step):
            o_ref[j + q] = y[:, q * ns:(q + 1) * ns].reshape(S5_T, S5_H, ns).astype(BF16)


def _s5_core(xt, xct, params, nseg, nctx):
    groups, rows, lanes = xt.shape
    ns = xct.shape[-1]
    nj = lanes // ns
    kern = functools.partial(_s5_core_kernel, nseg=nseg, nctx=nctx)
    per_group = lambda a: pl.BlockSpec((None,) + a.shape[1:], lambda g, nd=a.ndim: (g,) + (0,) * (nd - 1))
    return pl.pallas_call(
        kern,
        grid=(groups,),
        in_specs=[per_group(xt), per_group(xct)] + [per_group(a) for a in params],
        out_specs=pl.BlockSpec((nj, S5_T, None, S5_H, ns), lambda g: (0, 0, g, 0, 0)),
        out_shape=jax.ShapeDtypeStruct((nj, S5_T, groups, S5_H, ns), BF16),
        scratch_shapes=[pltpu.VMEM((rows, lanes), F32)],
        compiler_params=_cparams(("arbitrary",)),
        name="s5_core",
    )(xt, xct, *params)


def _s5_param_layout(a_re, a_im, log_step, b_re, b_im, c_re, c_im, d_skip):
    ndir, groups, n = a_re.shape
    h = b_re.shape[-1]
    ls = jnp.broadcast_to(log_step[..., None], a_re.shape)
    pcol = jnp.stack([a_re[0], a_im[0], ls[0], a_re[1], a_im[1], ls[1]], axis=1)[..., None]
    lay = lambda v: jnp.concatenate([v[0], v[0], v[1], v[1]], axis=-1)
    prow = jnp.stack([lay(a_re), lay(a_im), lay(ls)], axis=1)
    tile_t = lambda v: jnp.tile(v, (1, 1, S5_T))
    bt = jnp.concatenate([tile_t(b_re[0]), tile_t(b_im[0]), tile_t(b_re[1]), tile_t(b_im[1])], axis=1)
    ct1 = jnp.concatenate([c_re[0], -c_im[0], c_re[1], -c_im[1]], axis=-1)
    ct2 = jnp.concatenate([-c_im[0], -c_re[0], -c_im[1], -c_re[1]], axis=-1)
    dsk = jnp.tile(d_skip.reshape(groups, 1, h), (1, 1, S5_T))
    pad = jnp.zeros((groups, (-(2 * h + 4)) % 8, S5_T * h), F32)
    pp = jnp.concatenate([ct1, ct2, prow, dsk, pad], axis=1)
    return tuple(v.astype(F32) for v in (pcol, pp, bt))


def _s5_post_kernel(y_ref, hg_ref, wglu_ref, bglu_ref, o_ref, m_ref):
    q = pl.program_id(1)
    tb, e, ns = y_ref.shape
    ys = jnp.concatenate([y_ref[t].astype(F32).T for t in range(tb)], axis=0)
    yg = _gelu_tanh(ys)
    zh = jnp.dot(yg.astype(BF16), wglu_ref[...], preferred_element_type=F32) + bglu_ref[...]
    hg = hg_ref[...].reshape(tb * ns, e).astype(F32)
    m = ((yg * (0.5 * jnp.tanh(zh) + 0.5)) * (hg * jnp.tanh(hg) + hg)).astype(BF16)
    nq = S5_T // tb
    for half in range(nq):
        @pl.when(q == half)
        def _(half=half):
            m_ref[half * tb * ns:(half + 1) * tb * ns, :] = m

    @pl.when(q == nq - 1)
    def _():
        perm = _perm_matrix(S5_PB, S5_T)
        for k in range(ns // S5_PB):
            g = jnp.concatenate([m_ref[t * ns + k * S5_PB:t * ns + (k + 1) * S5_PB, :] for t in range(S5_T)], axis=0)
            r = jnp.dot(perm, g, preferred_element_type=F32).astype(BF16)
            o_ref[k * S5_PB:(k + 1) * S5_PB] = r.reshape(S5_PB, S5_T, e)


def _s5_post(y4, hg, wglu, bglu):
    nj, t_all, e, ns = y4.shape
    nt = t_all // S5_TB
    return pl.pallas_call(
        _s5_post_kernel,
        grid=(nj, nt),
        in_specs=[pl.BlockSpec((None, S5_TB, e, ns), lambda j, q: (j, q, 0, 0)),
                  pl.BlockSpec((S5_TB, ns, e), lambda j, q: (j * nt + q, 0, 0)),
                  pl.BlockSpec((e, e), lambda j, q: (0, 0)),
                  pl.BlockSpec((1, e), lambda j, q: (0, 0))],
        out_specs=pl.BlockSpec((ns, S5_T, e), lambda j, q: (0, j, 0)),
        out_shape=jax.ShapeDtypeStruct((ns, nj * S5_T, e), BF16),
        scratch_shapes=[pltpu.VMEM((S5_T * ns, e), BF16)],
        compiler_params=_cparams(("arbitrary", "arbitrary")),
        name="s5_post",
    )(y4, hg, wglu, bglu)


def _softplus(z):
    return jnp.maximum(z, 0.0) + jnp.log1p(jnp.exp(-jnp.abs(z)))


def _gate_matmul(xc, wg_ref):
    xb = xc.astype(BF16)
    nq = wg_ref.shape[0]
    return [jnp.dot(xb[:, q * GATE_TILE:(q + 1) * GATE_TILE], wg_ref[q], preferred_element_type=F32)
            for q in range(nq)]


def _gate_cols(zs, k):
    return jnp.concatenate([z[:, k * GATE_TILE:(k + 1) * GATE_TILE] for z in zs], axis=1)


def _lru_decay_scale(lam_row):
    return (-0.5 * LRU_C * math.log2(math.e)) * _softplus(-lam_row)


def _lru_gate_math(hx, za, zx, ba, bx, c2):
    ta = jnp.tanh(za + ba)
    tx = jnp.tanh(zx + bx)
    la = ta * c2 + c2
    a = jnp.exp2(la)
    om = 1.0 - a * a
    return la, a, (om * lax.rsqrt(jnp.maximum(om, TINY))) * (tx * hx + hx)


def _lru_coeffs(hx, zs, k, gbh, c2):
    return _lru_gate_math(hx, _gate_cols(zs, 2 * k), _gate_cols(zs, 2 * k + 1),
                          gbh[2 * k:2 * k + 1], gbh[2 * k + 1:2 * k + 2], c2)[1:]


def _norm_mod_rows(x_ref, gs_ref, sh_ref, dst_ref):
    nb, w, _ = x_ref.shape
    for b in range(nb):
        dst_ref[b * w:(b + 1) * w, :] = _norm_mod(x_ref[b], gs_ref[b:b + 1], sh_ref[b:b + 1]).astype(BF16)


def _lru_ctx_kernel(x_ref, gs_ref, sh_ref, wu_ref, cw_ref, cb_ref, wg_ref, gb_ref, lam_ref,
                    ff_ref, fr_ref, af_s, bf_s, ar_s, br_s):
    rows = x_ref.shape[0]
    npos = rows // 8
    xn = _norm_mod(x_ref[...], gs_ref[...], sh_ref[...]).astype(BF16)
    u = jnp.dot(xn, wu_ref[...], preferred_element_type=F32)
    pos = lax.broadcasted_iota(jnp.int32, u.shape, 0) // 8
    um1 = jnp.where(pos >= 1, pltpu.roll(u, 8, 0), 0.0)
    up1 = jnp.where(pos < npos - 1, pltpu.roll(u, rows - 8, 0), 0.0)
    up2 = jnp.where(pos < npos - 2, pltpu.roll(u, rows - 16, 0), 0.0)
    cw = cw_ref[...]
    xc = cw[0:1] * um1 + cw[1:2] * u + cw[2:3] * up1 + cw[3:4] * up2 + cb_ref[...]
    zs = _gate_matmul(xc, wg_ref)
    gbh = gb_ref[...]
    c2 = _lru_decay_scale(lam_ref[...])
    hx = 0.5 * xc
    a, b = _lru_coeffs(hx, zs, 0, gbh, c2[0:1])
    af_s[...] = a
    bf_s[...] = b
    a, b = _lru_coeffs(hx, zs, 1, gbh, c2[1:2])
    ar_s[...] = a
    br_s[...] = b

    def body(p, carry):
        hf, hr, pr = carry
        r0 = pl.multiple_of(p * 8, 8)
        hf = af_s[pl.ds(r0, 8), :] * hf + bf_s[pl.ds(r0, 8), :]
        hr = hr + pr * br_s[pl.ds(r0, 8), :]
        pr = pr * ar_s[pl.ds(r0, 8), :]
        return hf, hr, pr

    zero = jnp.zeros((8, u.shape[1]), F32)
    hf, hr, _ = lax.fori_loop(0, npos, body, (zero, zero, zero + 1.0))
    ff_ref[...] = hf
    fr_ref[...] = hr


def _lru_ctx(xp, gs, sh, w_all, cw, cb, wg, gb, lam):
    rows, d = xp.shape
    e = cw.shape[1]
    nq = e // GATE_TILE
    full = lambda a: pl.BlockSpec(a.shape, lambda q, nd=a.ndim: (0,) * nd)
    cols = lambda a: pl.BlockSpec((a.shape[0], GATE_TILE), lambda q: (0, q))
    wu_cols = pl.BlockSpec((d, GATE_TILE), lambda q: (0, W_COL_ULRU * nq + q))
    return pl.pallas_call(
        _lru_ctx_kernel,
        grid=(nq,),
        in_specs=[full(xp), full(gs), full(sh), wu_cols, cols(cw), cols(cb),
                  pl.BlockSpec((1,) + wg.shape[1:], lambda q: (q, 0, 0)), cols(gb), cols(lam)],
        out_specs=[pl.BlockSpec((8, GATE_TILE), lambda q: (0, q)), pl.BlockSpec((8, GATE_TILE), lambda q: (0, q))],
        out_shape=[jax.ShapeDtypeStruct((8, e), F32), jax.ShapeDtypeStruct((8, e), F32)],
        scratch_shapes=[pltpu.VMEM((rows, GATE_TILE), F32) for _ in range(4)],
        compiler_params=_cparams(("arbitrary",)),
    )(xp, gs, sh, w_all, cw, cb, wg, gb, lam)


def _lru_pass_a_kernel(xa_ref, xb_ref, gs_ref, sh_ref, wu_ref, cw_ref, cb_ref, wg_ref, gb_ref, lam_ref,
                       lar_ref, br_ref, hl_ref, pc_ref, hs_ref, ps_ref, pr_ref, hr_ref,
                       ring_ref, xn_ref, xc_ref, xcb_ref, z_ref, *, nrow):
    i = pl.program_id(0)
    nb, w, d = xa_ref.shape
    rows = nb * w
    nq = wg_ref.shape[0]
    gt = GATE_TILE

    @pl.when(i == 0)
    def _():
        ring_ref[...] = jnp.zeros_like(ring_ref)

    @pl.when(i <= LRU_WARM)
    def _():
        hs_ref[...] = jnp.zeros_like(hs_ref)
        ps_ref[...] = jnp.ones_like(ps_ref)
        hr_ref[...] = jnp.zeros_like(hr_ref)
        pr_ref[...] = jnp.ones_like(pr_ref)

    r = lax.broadcasted_iota(jnp.int32, (rows, rows), 0)
    c = lax.broadcasted_iota(jnp.int32, (rows, rows), 1)
    same_batch = r // w == c // w
    shifted = []
    for k, x_ref in enumerate((xa_ref, xb_ref)):
        _norm_mod_rows(x_ref, gs_ref, sh_ref, xn_ref.at[k * rows:(k + 1) * rows])
        s = 2 * i + k
        off = jnp.where(s == 0, -1, jnp.where(s > nrow, 1, 0))
        mix = ((c == r + off) & same_batch).astype(BF16)
        shifted.append(jnp.dot(mix, xn_ref[k * rows:(k + 1) * rows, :], preferred_element_type=F32).astype(BF16))
    u = jnp.dot(jnp.concatenate(shifted, axis=0), wu_ref[...], preferred_element_type=F32)
    ring_ref[(2 * i) & 7] = u[0:rows]
    ring_ref[(2 * i + 1) & 7] = u[rows:2 * rows]

    cw = cw_ref[...]
    cb = cb_ref[...]
    for k in range(LRU_RB):
        sl = [(2 * i - 2 * LRU_WARM + k + t) & 7 for t in range(4)]
        for r0 in range(0, rows, LRU_RC):
            rs = slice(r0, r0 + LRU_RC)
            ks = slice(k * rows + r0, k * rows + r0 + LRU_RC)
            xc = (cw[0:1] * ring_ref[sl[0], rs, :] + cw[1:2] * ring_ref[sl[1], rs, :]
                  + cw[2:3] * ring_ref[sl[2], rs, :] + cw[3:4] * ring_ref[sl[3], rs, :] + cb)
            xc_ref[ks, :] = xc
            xcb_ref[ks, :] = xc.astype(BF16)
    for q in range(nq):
        z_ref[q] = jnp.dot(xcb_ref[:, q * gt:(q + 1) * gt], wg_ref[q], preferred_element_type=F32)
    gbh = gb_ref[...]
    c2 = _lru_decay_scale(lam_ref[...])
    for k in range(LRU_RB):
        for q in range(nq):
            cs = slice(q * gt, (q + 1) * gt)
            for r0 in range(0, rows, LRU_RC):
                rs = slice(r0, r0 + LRU_RC)
                ks = slice(k * rows + r0, k * rows + r0 + LRU_RC)
                hx = 0.5 * xc_ref[ks, cs]
                _, af, bf = _lru_gate_math(hx, z_ref[q, ks, 0:gt], z_ref[q, ks, gt:2 * gt],
                                           gbh[0:1, cs], gbh[1:2, cs], c2[0:1, cs])
                lar, ar, br = _lru_gate_math(hx, z_ref[q, ks, 2 * gt:3 * gt], z_ref[q, ks, 3 * gt:4 * gt],
                                             gbh[2:3, cs], gbh[3:4, cs], c2[1:2, cs])
                h = af * hs_ref[rs, cs] + bf
                p = ps_ref[rs, cs] * af
                hs_ref[rs, cs] = h
                ps_ref[rs, cs] = p
                hl_ref[k, rs, cs] = h.astype(BF16)
                pc_ref[k, rs, cs] = p.astype(BF16)
                lar_ref[k, rs, cs] = lar.astype(BF16)
                br_ref[k, rs, cs] = br.astype(BF16)
                prod = pr_ref[rs, cs]
                hr_ref[rs, cs] = hr_ref[rs, cs] + prod * br
                pr_ref[rs, cs] = prod * ar


def _lru_pass_a(x4, gs8, sh8, w_all, cw, cb, wg, gb, lam):
    nb, nrow, w, d = x4.shape
    e = cw.shape[1]
    rows = nb * w
    rb = LRU_RB
    kern = functools.partial(_lru_pass_a_kernel, nrow=nrow)
    const = lambda a: pl.BlockSpec(a.shape, lambda i, nd=a.ndim: (0,) * nd)
    acc = pl.BlockSpec((rows, e), lambda i: (0, 0))
    acc_shape = jax.ShapeDtypeStruct((rows, e), F32)
    per_row = pl.BlockSpec((rb, rows, e), lambda i: (jnp.maximum(i - LRU_WARM, 0), 0, 0))
    per_row_shape = jax.ShapeDtypeStruct((nrow, rows, e), BF16)
    x_row = lambda first: pl.BlockSpec((nb, None, w, d), lambda i: (0, (rb * i + first) % nrow, 0, 0))
    return pl.pallas_call(
        kern,
        grid=(nrow // rb + LRU_WARM,),
        in_specs=[x_row(nrow - 1), x_row(0),
                  const(gs8), const(sh8), pl.BlockSpec((d, e), lambda i: (0, W_COL_ULRU)), const(cw), const(cb),
                  const(wg), const(gb), const(lam)],
        out_specs=[per_row] * 4 + [acc] * 4,
        out_shape=[per_row_shape] * 4 + [acc_shape] * 4,
        scratch_shapes=[pltpu.VMEM((8, rows, e), F32), pltpu.VMEM((rb * rows, d), BF16),
                        pltpu.VMEM((rb * rows, e), F32), pltpu.VMEM((rb * rows, e), BF16),
                        pltpu.VMEM((wg.shape[0], rb * rows, wg.shape[2]), F32)],
        compiler_params=_cparams(("arbitrary",)),
        name="lru_pass_a",
    )(x4, x4, gs8, sh8, w_all, cw, cb, wg, gb, lam)


def _lru_stitch_kernel(pf_ref, hf_ref, pr_ref, hr_ref, ff_ref, fr_ref, sf_ref, sr_ref, *, w):
    rows = pf_ref.shape[0]
    col = lax.broadcasted_iota(jnp.int32, pf_ref.shape, 0) % w
    p, h = pf_ref[...], hf_ref[...]
    k = 1
    while k < w:
        keep = col >= k
        ps = jnp.where(keep, pltpu.roll(p, k, 0), 1.0)
        hs = jnp.where(keep, pltpu.roll(h, k, 0), 0.0)
        p, h = p * ps, p * hs + h
        k *= 2
    fin = ff_ref[...]
    sf_ref[...] = jnp.where(col == 0, fin, pltpu.roll(p, 1, 0) * fin + pltpu.roll(h, 1, 0))
    p, h = pr_ref[...], hr_ref[...]
    k = 1
    while k < w:
        keep = col < w - k
        ps = jnp.where(keep, pltpu.roll(p, rows - k, 0), 1.0)
        hs = jnp.where(keep, pltpu.roll(h, rows - k, 0), 0.0)
        p, h = p * ps, p * hs + h
        k *= 2
    fin = fr_ref[...]
    sr_ref[...] = jnp.where(col == w - 1, fin, pltpu.roll(p, rows - 1, 0) * fin + pltpu.roll(h, rows - 1, 0))


def _lru_stitch(pf, hf, pr, hr, ff_rows, fr_rows, w):
    rows, e = pf.shape
    big = pl.BlockSpec((rows, e), lambda i: (0, 0))
    shp = jax.ShapeDtypeStruct((rows, e), F32)
    return pl.pallas_call(
        functools.partial(_lru_stitch_kernel, w=w),
        grid=(1,),
        in_specs=[big] * 6,
        out_specs=[big, big],
        out_shape=[shp, shp],
        compiler_params=_cparams(("arbitrary",)),
        name="lru_stitch",
    )(pf, hf, pr, hr, ff_rows, fr_rows)


def _lru_rev_out_kernel(lar_ref, br_ref, hl_ref, pc_ref, x_ref, m5_ref, gs_ref, sh_ref, gt_ref, hsf_ref, hsr_ref,
                        wgl_ref, wo_ref, fg_ref, o_ref, h_ref, xb_ref, g_ref, mb_ref, mix_ref):
    i = pl.program_id(0)
    nb, nrb, w, d = x_ref.shape
    rows = nb * w
    e = h_ref.shape[-1]

    @pl.when(i == 0)
    def _():
        h_ref[...] = hsr_ref[...]

    for rr in range(nrb):
        for b in range(nb):
            xb_ref[rr * rows + b * w:rr * rows + (b + 1) * w, :] = _norm_mod(
                x_ref[b, rr], gs_ref[b:b + 1], sh_ref[b:b + 1]).astype(BF16)
    g_ref[...] = jnp.dot(xb_ref[...], wgl_ref[...], preferred_element_type=F32)
    for rr in range(nrb - 1, -1, -1):
        for c0 in range(0, e, GATE_TILE):
            cs = slice(c0, c0 + GATE_TILE)
            for r0 in range(0, rows, LRU_RC):
                rs = slice(r0, r0 + LRU_RC)
                gs_rows = slice(rr * rows + r0, rr * rows + r0 + LRU_RC)
                h = jnp.exp2(lar_ref[rr, rs, cs].astype(F32)) * h_ref[rs, cs] + br_ref[rr, rs, cs].astype(F32)
                h_ref[rs, cs] = h
                y = hl_ref[rr, rs, cs].astype(F32) + pc_ref[rr, rs, cs].astype(F32) * hsf_ref[rs, cs] + h
                hg = g_ref[gs_rows, cs]
                mb_ref[gs_rows, cs] = (y * (hg * jnp.tanh(hg) + hg)).astype(BF16)
    m5 = jnp.concatenate([m5_ref[:, rr].reshape(rows, e) for rr in range(nrb)], axis=0)
    mix_ref[...] = (jnp.dot(m5, wo_ref[0:e, :], preferred_element_type=F32)
                    + jnp.dot(mb_ref[...], wo_ref[e:2 * e, :], preferred_element_type=F32))
    fg = fg_ref[...]
    for rr in range(nrb):
        for b in range(nb):
            for r0 in range(0, w, LRU_RC):
                m0 = rr * rows + b * w + r0
                res = x_ref[b, rr, r0:r0 + LRU_RC, :] + gt_ref[b:b + 1] * mix_ref[m0:m0 + LRU_RC, :]
                ms = jnp.mean(res * res, axis=-1, keepdims=True)
                o_ref[b, rr, r0:r0 + LRU_RC, :] = res * lax.rsqrt(ms + EPS) * fg


def _lru_rev_out(lar, br, hloc, pcum, x4, m5, gs8, sh8, gt8, hs_f, hs_r, w_all, wo, fg):
    nb, nrow, w, d = x4.shape
    rows, e = hs_f.shape
    const = lambda a: pl.BlockSpec(a.shape, lambda i, nd=a.ndim: (0,) * nd)
    rb = LRU_RB
    nstep = nrow // rb
    rev3 = pl.BlockSpec((rb, rows, e), lambda i: (nstep - 1 - i, 0, 0))
    rev4 = pl.BlockSpec((nb, rb, w, d), lambda i: (0, nstep - 1 - i, 0, 0))
    rev4e = pl.BlockSpec((nb, rb, w, e), lambda i: (0, nstep - 1 - i, 0, 0))
    return pl.pallas_call(
        _lru_rev_out_kernel,
        grid=(nstep,),
        in_specs=[rev3] * 4 + [rev4, rev4e, const(gs8), const(sh8), const(gt8), const(hs_f), const(hs_r),
                               pl.BlockSpec((d, e), lambda i: (0, W_COL_GLRU)), const(wo), const(fg)],
        out_specs=rev4,
        out_shape=jax.ShapeDtypeStruct((nb, nrow, w, d), F32),
        scratch_shapes=[pltpu.VMEM((rows, e), F32), pltpu.VMEM((rb * rows, d), BF16),
                        pltpu.VMEM((rb * rows, e), F32), pltpu.VMEM((rb * rows, e), BF16),
                        pltpu.VMEM((rb * rows, d), F32)],
        compiler_params=_cparams(("arbitrary",)),
        name="lru_rev_out",
    )(lar, br, hloc, pcum, x4, m5, gs8, sh8, gt8, hs_f, hs_r, w_all, wo, fg)


def _gate_tiles(w_a, w_x):
    ndir, heads, hd, _ = w_a.shape
    per = GATE_TILE // hd
    nq = heads // per

    def tiles(w):
        wq = w.reshape(nq, per, hd, hd)
        eye = jnp.eye(per, dtype=w.dtype)
        blk = wq[:, :, :, None, :] * eye[None, :, None, :, None]
        return blk.reshape(nq, GATE_TILE, GATE_TILE)

    return jnp.concatenate([tiles(w_a[0]), tiles(w_x[0]), tiles(w_a[1]), tiles(w_x[1])], axis=2).astype(BF16)


def kernel(x, c, ctx, c_ctx, w_mod, b_mod, norm_g, w_in, s5_a_re, s5_a_im, s5_log_step, s5_b_re, s5_b_im,
           s5_c_re, s5_c_im, s5_d, s5_w_glu, s5_b_glu, lru_conv_w, lru_conv_b, lru_w_a, lru_b_a, lru_w_x,
           lru_b_x, lru_lam, w_out, final_g):
    bsz, seq, d = x.shape
    nctx_tok = ctx.shape[1]
    e = s5_w_glu.shape[-1]
    assert w_mod.shape[0] == 1, "single-layer block"
    assert bsz <= 8 and seq % S5_SEG == 0 and seq % GRID_W == 0 and nctx_tok % S5_T == 0
    nseg = seq // S5_SEG
    ns = bsz * nseg
    nctx = nctx_tok // S5_T
    assert nctx <= nseg and ns % S5_PB == 0
    nrow = seq // GRID_W
    assert LRU_RB == 2 and nrow % LRU_RB == 0 and nrow >= 4

    c8 = jnp.zeros((8, d), F32).at[:bsz].set(c).at[bsz].set(c_ctx)
    mod = _modulation(c8, w_mod[0], b_mod[0])
    sh, sc, gt = mod[:, :d], mod[:, d:2 * d], mod[:, 2 * d:]
    gs = norm_g[0][None, :] * (1.0 + sc)

    w_in0 = w_in[0]
    assert w_in0.shape == (d, 4 * e)
    wt_us5 = w_in0[:, W_COL_US5 * e:(W_COL_US5 + 1) * e].T.astype(BF16)
    gate_cols = (jnp.arange(4 * e) // e == W_COL_GS5) | (jnp.arange(4 * e) // e == W_COL_GLRU)
    w_all = (w_in0 * jnp.where(gate_cols, 0.5, 1.0)[None, :]).astype(BF16)
    w_glu = (0.5 * s5_w_glu[0]).astype(BF16)
    wo = w_out[0].astype(BF16)

    s5_params = _s5_param_layout(s5_a_re[0], s5_a_im[0], s5_log_step[0], s5_b_re[0], s5_b_im[0],
                                 s5_c_re[0], s5_c_im[0], s5_d[0])
    x3 = x.reshape(ns, S5_SEG, d)
    xt, hg5 = _s5_inproj(x3, gs, sh, wt_us5, w_all, bsz)
    ctx_pad = jnp.zeros((bsz, nseg, S5_T, d), F32).at[:, :nctx].set(ctx.reshape(bsz, nctx, S5_T, d))
    gs_c = jnp.broadcast_to(gs[bsz:bsz + 1], (8, d))
    sh_c = jnp.broadcast_to(sh[bsz:bsz + 1], (8, d))
    xct, _ = _s5_inproj(ctx_pad.reshape(ns, S5_T, d), gs_c, sh_c, wt_us5, w_all, bsz)
    y5 = _s5_core(xt, xct, s5_params, nseg, nctx)
    nj = S5_SEG // S5_T
    m5 = _s5_post(y5.reshape(nj, S5_T, e, ns), hg5, w_glu, 0.5 * s5_b_glu[0].reshape(1, e))

    wg_all = _gate_tiles(0.5 * lru_w_a[0], 0.5 * lru_w_x[0])
    gb_all = 0.5 * jnp.stack([lru_b_a[0, 0], lru_b_x[0, 0], lru_b_a[0, 1], lru_b_x[0, 1]])
    cw = lru_conv_w[0]
    cb = lru_conv_b[0].reshape(1, e)
    lam = lru_lam[0]
    ctx_p = jnp.zeros((8, nctx_tok, d), F32).at[:bsz].set(ctx).transpose(1, 0, 2).reshape(nctx_tok * 8, d)
    ff, fr = _lru_ctx(ctx_p, gs[bsz:bsz + 1], sh[bsz:bsz + 1], w_all, cw, cb, wg_all, gb_all, lam)
    x4 = x.reshape(bsz, nrow, GRID_W, d)
    lar, br, hloc, pcum, hf, pf, pr, hr = _lru_pass_a(x4, gs, sh, w_all, cw, cb, wg_all, gb_all, lam)
    hs_f, hs_r = _lru_stitch(pf, hf, pr, hr, jnp.repeat(ff[:bsz], GRID_W, axis=0),
                             jnp.repeat(fr[:bsz], GRID_W, axis=0), GRID_W)
    out4 = _lru_rev_out(lar, br, hloc, pcum, x4, m5.reshape(bsz, nrow, GRID_W, e), gs, sh, gt, hs_f, hs_r,
                        w_all, wo, final_g.reshape(1, d))
    return out4.reshape(bsz, seq, d)
```

```python
import functools
import math

import jax
import jax.numpy as jnp
from jax import lax
from jax.experimental import pallas as pl
from jax.experimental.pallas import tpu as pltpu

F32 = jnp.float32
BF16 = jnp.bfloat16

EPS = 1e-6
TINY = 1e-30
GRID_W = 64
LRU_C = 8.0
S5_H = 16
S5_N = 64
S5_T = 16
S5_SEG = 256
S5_TB = 8
S5_PB = 16
GATE_TILE = 256
LRU_RB = 2
LRU_WARM = 2
LRU_RC = 32
W_COL_US5, W_COL_GS5, W_COL_ULRU, W_COL_GLRU = 0, 1, 2, 3
V7X_VMEM_BYTES = 64 * 1024 * 1024
VMEM_LIMIT = V7X_VMEM_BYTES * 7 // 8


def _cparams(sem):
    return pltpu.CompilerParams(dimension_semantics=sem, vmem_limit_bytes=VMEM_LIMIT)


def _sigmoid(z):
    return 0.5 * (jnp.tanh(0.5 * z) + 1.0)


def _gelu_tanh(y):
    return 0.5 * y * (1.0 + jnp.tanh(math.sqrt(2.0 / math.pi) * (y + 0.044715 * (y * y * y))))


def _norm_mod(x, gs, sh):
    ms = jnp.mean(x * x, axis=-1, keepdims=True)
    return x * lax.rsqrt(ms + EPS) * gs + sh


def _mod_kernel(c_ref, w_ref, b_ref, o_ref):
    c = c_ref[...]
    s = c * _sigmoid(c)
    o_ref[...] = jnp.dot(s, w_ref[...], preferred_element_type=F32) + b_ref[...]


def _modulation(c8, w_mod, b_mod):
    d = c8.shape[1]
    n = w_mod.shape[1]
    nb = n // d
    return pl.pallas_call(
        _mod_kernel,
        grid=(nb,),
        in_specs=[pl.BlockSpec((8, d), lambda i: (0, 0)),
                  pl.BlockSpec((d, d), lambda i: (0, i)),
                  pl.BlockSpec((1, d), lambda i: (0, i))],
        out_specs=pl.BlockSpec((8, d), lambda i: (0, i)),
        out_shape=jax.ShapeDtypeStruct((8, n), F32),
        compiler_params=_cparams(("arbitrary",)),
    )(c8, w_mod, b_mod.reshape(1, n))


def _perm_matrix(n_a, n_b):
    n = n_a * n_b
    r = lax.broadcasted_iota(jnp.int32, (n, n), 0)
    c = lax.broadcasted_iota(jnp.int32, (n, n), 1)
    return ((r // n_b == c % n_a) & (r % n_b == c // n_a)).astype(BF16)


def _s5_inproj_kernel(x_ref, gs_ref, sh_ref, wt_ref, wg_ref, o_ref, hg_ref, xs_ref, *, nbatch):
    ns, tb, d = x_ref.shape
    groups = o_ref.shape[0]
    rows_b = (ns // nbatch) * tb
    blk = S5_PB * tb
    perm = _perm_matrix(tb, S5_PB)
    x = x_ref[...].reshape(ns * tb, d)
    xn = jnp.concatenate(
        [_norm_mod(x[b * rows_b:(b + 1) * rows_b], gs_ref[b:b + 1], sh_ref[b:b + 1]).astype(BF16)
         for b in range(nbatch)], axis=0)
    for k in range(ns // S5_PB):
        q = jnp.dot(perm, xn[k * blk:(k + 1) * blk], preferred_element_type=F32).astype(BF16)
        for t in range(tb):
            xs_ref[t * ns + k * S5_PB:t * ns + (k + 1) * S5_PB, :] = q[t * S5_PB:(t + 1) * S5_PB]
    wt = wt_ref[...]
    for t in range(0, tb, 2):
        ut = lax.dot_general(wt, xs_ref[t * ns:(t + 2) * ns, :], (((1,), (1,)), ((), ())),
                             preferred_element_type=F32)
        o_ref[:, t * S5_H:(t + 1) * S5_H, :] = ut[:, :ns].reshape(groups, S5_H, ns).astype(BF16)
        o_ref[:, (t + 1) * S5_H:(t + 2) * S5_H, :] = ut[:, ns:].reshape(groups, S5_H, ns).astype(BF16)
    hg = jnp.dot(xs_ref[...], wg_ref[...], preferred_element_type=F32)
    hg_ref[...] = hg.astype(BF16).reshape(tb, ns, hg.shape[1])


def _s5_inproj(x3, gs8, sh8, wt, w_all, nbatch):
    ns, seg, d = x3.shape
    e = wt.shape[0]
    groups = e // S5_H
    nj = seg // S5_T
    per = S5_T // S5_TB
    return pl.pallas_call(
        functools.partial(_s5_inproj_kernel, nbatch=nbatch),
        grid=(seg // S5_TB,),
        in_specs=[pl.BlockSpec((ns, S5_TB, d), lambda i: (0, i, 0)),
                  pl.BlockSpec((8, d), lambda i: (0, 0)),
                  pl.BlockSpec((8, d), lambda i: (0, 0)),
                  pl.BlockSpec((e, d), lambda i: (0, 0)),
                  pl.BlockSpec((d, e), lambda i: (0, W_COL_GS5))],
        out_specs=[pl.BlockSpec((groups, S5_TB * S5_H, ns), lambda i: (0, i % per, i // per)),
                   pl.BlockSpec((S5_TB, ns, e), lambda i: (i, 0, 0))],
        out_shape=[jax.ShapeDtypeStruct((groups, S5_T * S5_H, nj * ns), BF16),
                   jax.ShapeDtypeStruct((seg, ns, e), BF16)],
        scratch_shapes=[pltpu.VMEM((S5_TB * ns, d), BF16)],
        compiler_params=_cparams(("arbitrary",)),
        name="s5_inproj",
    )(x3, gs8, sh8, wt, w_all)


def _cmul_add(dr, di, hr, hi, sr, si):
    return dr * hr - di * hi + sr, dr * hi + di * hr + si


def _lane_scan(vr, vi, dr, di, slot, count, reverse):
    n = vr.shape[1]
    k = 1
    while k < count:
        if reverse:
            sr = pltpu.roll(vr, n - k, 1)
            si = pltpu.roll(vi, n - k, 1)
            keep = slot < count - k
        else:
            sr = pltpu.roll(vr, k, 1)
            si = pltpu.roll(vi, k, 1)
            keep = slot >= k
        sr = jnp.where(keep, sr, 0.0)
        si = jnp.where(keep, si, 0.0)
        vr, vi = _cmul_add(dr, di, sr, si, vr, vi)
        dr, di = dr * dr - di * di, 2.0 * dr * di
        k *= 2
    return vr, vi


def _csq(r, i):
    return r * r - i * i, 2.0 * r * i


def _cpow_bits(e, pows):
    pr = jnp.ones(e.shape, F32)
    pi = jnp.zeros(e.shape, F32)
    for bit, (qr, qi) in pows:
        on = (e & bit) != 0
        pr, pi = jnp.where(on, pr * qr - pi * qi, pr), jnp.where(on, pr * qi + pi * qr, pi)
    return pr, pi


def _s5_discretize(ar, ai, ls):
    step = jnp.exp(ls)
    mag = jnp.exp(ar * step)
    pr = mag * jnp.cos(ai * step)
    pi = mag * jnp.sin(ai * step)
    den = ar * ar + ai * ai
    nr = pr - 1.0
    return pr, pi, (nr * ar + pi * ai) / den, (pi * ar - nr * ai) / den


def _s5_operators(pcol_ref, prow_ref, bt_ref, ct1_ref, ct2_ref, dsk_ref):
    n, t_len, th = S5_N, S5_T, S5_T * S5_H
    hp = lax.Precision.HIGHEST
    lane_t = lax.broadcasted_iota(jnp.int32, (1, th), 1) // S5_H
    bits = [1 << b for b in range(t_len.bit_length() - 1)]

    tile_h = (lax.broadcasted_iota(jnp.int32, (S5_H, th), 1) % S5_H
              == lax.broadcasted_iota(jnp.int32, (S5_H, th), 0)).astype(F32)
    bt = jnp.dot(bt_ref[...], tile_h, precision=hp, preferred_element_type=F32)

    blocks, bbs, decay = [], [], []
    for d in range(2):
        pr, pi, cr, ci = _s5_discretize(pcol_ref[3 * d], pcol_ref[3 * d + 1], pcol_ref[3 * d + 2])
        pows, q = [], (pr, pi)
        for bit in bits:
            pows.append((bit, q))
            q = _csq(*q)
        decay.append(q)
        btr = bt[2 * d * n:(2 * d + 1) * n, :]
        bti = bt[(2 * d + 1) * n:(2 * d + 2) * n, :]
        bbr = cr * btr - ci * bti
        bbi = cr * bti + ci * btr
        e = (t_len - 1 - lane_t) if d == 0 else lane_t
        wr, wi = _cpow_bits(jnp.broadcast_to(e, (n, th)), pows)
        blocks += [wr * bbr - wi * bbi, wr * bbi + wi * bbr]
        bbs.append(jnp.concatenate([bbr, bbi], axis=0))
    bpow = jnp.concatenate(blocks, axis=0).astype(BF16)

    qr, qi, _, _ = _s5_discretize(prow_ref[0:1], prow_ref[1:2], prow_ref[2:3])
    pows, q = [], (qr, qi)
    for bit in bits:
        pows.append((bit, q))
        q = _csq(*q)
    q_t = q
    row_t = lax.broadcasted_iota(jnp.int32, (th, 1), 0) // S5_H
    is_f = lax.broadcasted_iota(jnp.int32, (1, 4 * n), 1) < 2 * n
    e_k = jnp.where(is_f, row_t, (t_len - row_t) % t_len)
    kr, ki = _cpow_bits(e_k, pows)
    nr = jnp.where(is_f, kr * qr - ki * qi, jnp.where(row_t == 0, q_t[0], kr))
    ni = jnp.where(is_f, kr * qi + ki * qr, jnp.where(row_t == 0, q_t[1], ki))
    ct1 = jnp.concatenate([ct1_ref[...]] * t_len, axis=0)
    ct2 = jnp.concatenate([ct2_ref[...]] * t_len, axis=0)
    cpow = (ct1 * nr + ct2 * ni).astype(BF16)
    qk = ct1 * kr + ct2 * ki
    mf = jnp.dot(qk[:, :2 * n], bbs[0], precision=hp, preferred_element_type=F32)
    mr = jnp.dot(qk[:, 2 * n:], bbs[1], precision=hp, preferred_element_type=F32)
    m = (jnp.where(row_t <= t_len - 1 - lane_t, mf, 0.0)
         + jnp.where((row_t == 0) | (row_t >= t_len - lane_t), mr, 0.0))
    for bit in bits:
        m = jnp.where((lane_t & bit) != 0, pltpu.roll(m, S5_H * bit, 0), m)
    row = lax.broadcasted_iota(jnp.int32, (th, th), 0)
    col = lax.broadcasted_iota(jnp.int32, (th, th), 1)
    toep = (m + jnp.where(row == col, dsk_ref[...], 0.0)).astype(BF16)
    return toep, bpow, cpow, decay


def _s5_core_kernel(xt_ref, xc_ref, pcol_ref, pp_ref, bt_ref, o_ref, s_ref, *, nseg, nctx):
    n = S5_N
    h = S5_H
    ns = xc_ref.shape[-1]
    nj = xt_ref.shape[-1] // ns
    toep, bpow, cpow, decay = _s5_operators(pcol_ref, pp_ref.at[2 * h:2 * h + 3], bt_ref, pp_ref.at[0:h],
                                            pp_ref.at[h:2 * h], pp_ref.at[2 * h + 3:2 * h + 4])
    dfr = jnp.broadcast_to(decay[0][0], (n, ns))
    dfi = jnp.broadcast_to(decay[0][1], (n, ns))
    drr = jnp.broadcast_to(decay[1][0], (n, ns))
    dri = jnp.broadcast_to(decay[1][1], (n, ns))
    slot = lax.broadcasted_iota(jnp.int32, (n, ns), 1) % nseg

    sc = jnp.dot(bpow, xc_ref[...], preferred_element_type=F32)
    valid = slot < nctx
    cfr, cfi = _lane_scan(jnp.where(valid, sc[0:n], 0.0), jnp.where(valid, sc[n:2 * n], 0.0),
                          dfr, dfi, slot, nseg, False)
    crr, cri = _lane_scan(jnp.where(valid, sc[2 * n:3 * n], 0.0), jnp.where(valid, sc[3 * n:4 * n], 0.0),
                          drr, dri, slot, nseg, True)

    step = 2 if nj % 2 == 0 else 1
    for j in range(0, nj, step):
        cols = slice(j * ns, (j + step) * ns)
        s_ref[:, cols] = jnp.dot(bpow, xt_ref[:, cols], preferred_element_type=F32)

    zero = jnp.zeros((n, ns), F32)
    efr, efi, err, eri = zero, zero, zero, zero
    for j in range(nj):
        cols = slice(j * ns, (j + 1) * ns)
        efr, efi = _cmul_add(dfr, dfi, efr, efi, s_ref[0:n, cols], s_ref[n:2 * n, cols])
    for j in range(nj - 1, -1, -1):
        cols = slice(j * ns, (j + 1) * ns)
        err, eri = _cmul_add(drr, dri, err, eri, s_ref[2 * n:3 * n, cols], s_ref[3 * n:4 * n, cols])

    sfr, sfi, srr, sri = dfr, dfi, drr, dri
    k = 1
    while k < nj:
        sfr, sfi = sfr * sfr - sfi * sfi, 2.0 * sfr * sfi
        srr, sri = srr * srr - sri * sri, 2.0 * srr * sri
        k *= 2

    first = slot == 0
    last = slot == nseg - 1
    vfr = jnp.where(first, pltpu.roll(cfr, ns - (nctx - 1), 1) if nctx > 1 else cfr, pltpu.roll(efr, 1, 1))
    vfi = jnp.where(first, pltpu.roll(cfi, ns - (nctx - 1), 1) if nctx > 1 else cfi, pltpu.roll(efi, 1, 1))
    vrr = jnp.where(last, pltpu.roll(crr, nseg - 1, 1) if nseg > 1 else crr, pltpu.roll(err, ns - 1, 1))
    vri = jnp.where(last, pltpu.roll(cri, nseg - 1, 1) if nseg > 1 else cri, pltpu.roll(eri, ns - 1, 1))
    hfr, hfi = _lane_scan(vfr, vfi, sfr, sfi, slot, nseg, False)
    hrr, hri = _lane_scan(vrr, vri, srr, sri, slot, nseg, True)

    for j in range(nj):
        cols = slice(j * ns, (j + 1) * ns)
        nr, ni = _cmul_add(dfr, dfi, hfr, hfi, s_ref[0:n, cols], s_ref[n:2 * n, cols])
        s_ref[0:n, cols] = hfr
        s_ref[n:2 * n, cols] = hfi
        hfr, hfi = nr, ni
    for j in range(nj - 1, -1, -1):
        cols = slice(j * ns, (j + 1) * ns)
        nr, ni = _cmul_add(drr, dri, hrr, hri, s_ref[2 * n:3 * n, cols], s_ref[3 * n:4 * n, cols])
        s_ref[2 * n:3 * n, cols] = hrr
        s_ref[3 * n:4 * n, cols] = hri
        hrr, hri = nr, ni

    for j in range(0, nj, step):
        cols = slice(j * ns, (j + step) * ns)
        y = (jnp.dot(toep, xt_ref[:, cols], preferred_element_type=F32)
             + jnp.dot(cpow, s_ref[:, cols].astype(BF16), preferred_element_type=F32))
        for q in range(step):
            o_ref[j + q] = y[:, q * ns:(q + 1) * ns].reshape(S5_T, S5_H, ns).astype(BF16)


def _s5_core(xt, xct, params, nseg, nctx):
    groups, rows, lanes = xt.shape
    ns = xct.shape[-1]
    nj = lanes // ns
    kern = functools.partial(_s5_core_kernel, nseg=nseg, nctx=nctx)
    per_group = lambda a: pl.BlockSpec((None,) + a.shape[1:], lambda g, nd=a.ndim: (g,) + (0,) * (nd - 1))
    return pl.pallas_call(
        kern,
        grid=(groups,),
        in_specs=[per_group(xt), per_group(xct)] + [per_group(a) for a in params],
        out_specs=pl.BlockSpec((nj, S5_T, None, S5_H, ns), lambda g: (0, 0, g, 0, 0)),
        out_shape=jax.ShapeDtypeStruct((nj, S5_T, groups, S5_H, ns), BF16),
        scratch_shapes=[pltpu.VMEM((rows, lanes), F32)],
        compiler_params=_cparams(("arbitrary",)),
        name="s5_core",
    )(xt, xct, *params)


def _s5_param_layout(a_re, a_im, log_step, b_re, b_im, c_re, c_im, d_skip):
    ndir, groups, n = a_re.shape
    h = b_re.shape[-1]
    ls = jnp.broadcast_to(log_step[..., None], a_re.shape)
    pcol = jnp.stack([a_re[0], a_im[0], ls[0], a_re[1], a_im[1], ls[1]], axis=1)[..., None]
    lay = lambda v: jnp.concatenate([v[0], v[0], v[1], v[1]], axis=-1)
    prow = jnp.stack([lay(a_re), lay(a_im), lay(ls)], axis=1)
    bt = jnp.concatenate([b_re[0], b_im[0], b_re[1], b_im[1]], axis=1)
    ct1 = jnp.concatenate([c_re[0], -c_im[0], c_re[1], -c_im[1]], axis=-1)
    ct2 = jnp.concatenate([-c_im[0], -c_re[0], -c_im[1], -c_re[1]], axis=-1)
    dsk = jnp.tile(d_skip.reshape(groups, 1, h), (1, 1, S5_T))
    pad = jnp.zeros((groups, (-(2 * h + 4)) % 8, S5_T * h), F32)
    pp = jnp.concatenate([ct1, ct2, prow, dsk, pad], axis=1)
    return tuple(v.astype(F32) for v in (pcol, pp, bt))


def _s5_post_kernel(y_ref, hg_ref, wglu_ref, bglu_ref, o_ref, m_ref):
    q = pl.program_id(1)
    tb, e, ns = y_ref.shape
    ys = jnp.concatenate([y_ref[t].astype(F32).T for t in range(tb)], axis=0)
    yg = _gelu_tanh(ys)
    zh = jnp.dot(yg.astype(BF16), wglu_ref[...], preferred_element_type=F32) + bglu_ref[...]
    hg = hg_ref[...].reshape(tb * ns, e).astype(F32)
    m = ((yg * (0.5 * jnp.tanh(zh) + 0.5)) * (hg * jnp.tanh(hg) + hg)).astype(BF16)
    nq = S5_T // tb
    for half in range(nq):
        @pl.when(q == half)
        def _(half=half):
            m_ref[half * tb * ns:(half + 1) * tb * ns, :] = m

    @pl.when(q == nq - 1)
    def _():
        perm = _perm_matrix(S5_PB, S5_T)
        for k in range(ns // S5_PB):
            g = jnp.concatenate([m_ref[t * ns + k * S5_PB:t * ns + (k + 1) * S5_PB, :] for t in range(S5_T)], axis=0)
            r = jnp.dot(perm, g, preferred_element_type=F32).astype(BF16)
            o_ref[k * S5_PB:(k + 1) * S5_PB] = r.reshape(S5_PB, S5_T, e)


def _s5_post(y4, hg, wglu, bglu):
    nj, t_all, e, ns = y4.shape
    nt = t_all // S5_TB
    return pl.pallas_call(
        _s5_post_kernel,
        grid=(nj, nt),
        in_specs=[pl.BlockSpec((None, S5_TB, e, ns), lambda j, q: (j, q, 0, 0)),
                  pl.BlockSpec((S5_TB, ns, e), lambda j, q: (j * nt + q, 0, 0)),
                  pl.BlockSpec((e, e), lambda j, q: (0, 0)),
                  pl.BlockSpec((1, e), lambda j, q: (0, 0))],
        out_specs=pl.BlockSpec((ns, S5_T, e), lambda j, q: (0, j, 0)),
        out_shape=jax.ShapeDtypeStruct((ns, nj * S5_T, e), BF16),
        scratch_shapes=[pltpu.VMEM((S5_T * ns, e), BF16)],
        compiler_params=_cparams(("arbitrary", "arbitrary")),
        name="s5_post",
    )(y4, hg, wglu, bglu)


def _softplus(z):
    return jnp.maximum(z, 0.0) + jnp.log1p(jnp.exp(-jnp.abs(z)))


def _gate_matmul(xc, wg_ref):
    xb = xc.astype(BF16)
    nq = wg_ref.shape[0]
    return [jnp.dot(xb[:, q * GATE_TILE:(q + 1) * GATE_TILE], wg_ref[q], preferred_element_type=F32)
            for q in range(nq)]


def _gate_cols(zs, k):
    return jnp.concatenate([z[:, k * GATE_TILE:(k + 1) * GATE_TILE] for z in zs], axis=1)


def _lru_decay_scale(lam_row):
    return (-0.5 * LRU_C * math.log2(math.e)) * _softplus(-lam_row)


def _lru_gate_math(hx, za, zx, ba, bx, c2):
    ta = jnp.tanh(za + ba)
    tx = jnp.tanh(zx + bx)
    la = ta * c2 + c2
    a = jnp.exp2(la)
    om = 1.0 - a * a
    return la, a, (om * lax.rsqrt(jnp.maximum(om, TINY))) * (tx * hx + hx)


def _lru_coeffs(hx, zs, k, gbh, c2):
    return _lru_gate_math(hx, _gate_cols(zs, 2 * k), _gate_cols(zs, 2 * k + 1),
                          gbh[2 * k:2 * k + 1], gbh[2 * k + 1:2 * k + 2], c2)[1:]


def _norm_mod_rows(x_ref, gs_ref, sh_ref, dst_ref):
    nb, w, _ = x_ref.shape
    for b in range(nb):
        dst_ref[b * w:(b + 1) * w, :] = _norm_mod(x_ref[b], gs_ref[b:b + 1], sh_ref[b:b + 1]).astype(BF16)


def _lru_ctx_kernel(x_ref, gs_ref, sh_ref, wu_ref, cw_ref, cb_ref, wg_ref, gb_ref, lam_ref,
                    ff_ref, fr_ref, af_s, bf_s, ar_s, br_s):
    rows = x_ref.shape[0]
    npos = rows // 8
    xn = _norm_mod(x_ref[...], gs_ref[...], sh_ref[...]).astype(BF16)
    u = jnp.dot(xn, wu_ref[...], preferred_element_type=F32)
    pos = lax.broadcasted_iota(jnp.int32, u.shape, 0) // 8
    um1 = jnp.where(pos >= 1, pltpu.roll(u, 8, 0), 0.0)
    up1 = jnp.where(pos < npos - 1, pltpu.roll(u, rows - 8, 0), 0.0)
    up2 = jnp.where(pos < npos - 2, pltpu.roll(u, rows - 16, 0), 0.0)
    cw = cw_ref[...]
    xc = cw[0:1] * um1 + cw[1:2] * u + cw[2:3] * up1 + cw[3:4] * up2 + cb_ref[...]
    zs = _gate_matmul(xc, wg_ref)
    gbh = gb_ref[...]
    c2 = _lru_decay_scale(lam_ref[...])
    hx = 0.5 * xc
    a, b = _lru_coeffs(hx, zs, 0, gbh, c2[0:1])
    af_s[...] = a
    bf_s[...] = b
    a, b = _lru_coeffs(hx, zs, 1, gbh, c2[1:2])
    ar_s[...] = a
    br_s[...] = b

    def body(p, carry):
        hf, hr, pr = carry
        r0 = pl.multiple_of(p * 8, 8)
        hf = af_s[pl.ds(r0, 8), :] * hf + bf_s[pl.ds(r0, 8), :]
        hr = hr + pr * br_s[pl.ds(r0, 8), :]
        pr = pr * ar_s[pl.ds(r0, 8), :]
        return hf, hr, pr

    zero = jnp.zeros((8, u.shape[1]), F32)
    hf, hr, _ = lax.fori_loop(0, npos, body, (zero, zero, zero + 1.0))
    ff_ref[...] = hf
    fr_ref[...] = hr


def _lru_ctx(xp, gs, sh, w_all, cw, cb, wg, gb, lam):
    rows, d = xp.shape
    e = cw.shape[1]
    nq = e // GATE_TILE
    full = lambda a: pl.BlockSpec(a.shape, lambda q, nd=a.ndim: (0,) * nd)
    cols = lambda a: pl.BlockSpec((a.shape[0], GATE_TILE), lambda q: (0, q))
    wu_cols = pl.BlockSpec((d, GATE_TILE), lambda q: (0, W_COL_ULRU * nq + q))
    return pl.pallas_call(
        _lru_ctx_kernel,
        grid=(nq,),
        in_specs=[full(xp), full(gs), full(sh), wu_cols, cols(cw), cols(cb),
                  pl.BlockSpec((1,) + wg.shape[1:], lambda q: (q, 0, 0)), cols(gb), cols(lam)],
        out_specs=[pl.BlockSpec((8, GATE_TILE), lambda q: (0, q)), pl.BlockSpec((8, GATE_TILE), lambda q: (0, q))],
        out_shape=[jax.ShapeDtypeStruct((8, e), F32), jax.ShapeDtypeStruct((8, e), F32)],
        scratch_shapes=[pltpu.VMEM((rows, GATE_TILE), F32) for _ in range(4)],
        compiler_params=_cparams(("arbitrary",)),
    )(xp, gs, sh, w_all, cw, cb, wg, gb, lam)


def _lru_pass_a_kernel(xa_ref, xb_ref, gs_ref, sh_ref, wu_ref, cw_ref, cb_ref, wg_ref, gb_ref, lam_ref,
                       lar_ref, br_ref, hl_ref, pc_ref, hs_ref, ps_ref, pr_ref, hr_ref,
                       ring_ref, xn_ref, xc_ref, xcb_ref, z_ref, *, nrow):
    i = pl.program_id(0)
    nb, w, d = xa_ref.shape
    rows = nb * w
    nq = wg_ref.shape[0]
    gt = GATE_TILE

    @pl.when(i == 0)
    def _():
        ring_ref[...] = jnp.zeros_like(ring_ref)

    @pl.when(i <= LRU_WARM)
    def _():
        hs_ref[...] = jnp.zeros_like(hs_ref)
        ps_ref[...] = jnp.ones_like(ps_ref)
        hr_ref[...] = jnp.zeros_like(hr_ref)
        pr_ref[...] = jnp.ones_like(pr_ref)

    r = lax.broadcasted_iota(jnp.int32, (rows, rows), 0)
    c = lax.broadcasted_iota(jnp.int32, (rows, rows), 1)
    same_batch = r // w == c // w
    shifted = []
    for k, x_ref in enumerate((xa_ref, xb_ref)):
        _norm_mod_rows(x_ref, gs_ref, sh_ref, xn_ref.at[k * rows:(k + 1) * rows])
        s = 2 * i + k
        off = jnp.where(s == 0, -1, jnp.where(s > nrow, 1, 0))
        mix = ((c == r + off) & same_batch).astype(BF16)
        shifted.append(jnp.dot(mix, xn_ref[k * rows:(k + 1) * rows, :], preferred_element_type=F32).astype(BF16))
    u = jnp.dot(jnp.concatenate(shifted, axis=0), wu_ref[...], preferred_element_type=F32)
    ring_ref[(2 * i) & 7] = u[0:rows]
    ring_ref[(2 * i + 1) & 7] = u[rows:2 * rows]

    cw = cw_ref[...]
    cb = cb_ref[...]
    for k in range(LRU_RB):
        sl = [(2 * i - 2 * LRU_WARM + k + t) & 7 for t in range(4)]
        for r0 in range(0, rows, LRU_RC):
            rs = slice(r0, r0 + LRU_RC)
            ks = slice(k * rows + r0, k * rows + r0 + LRU_RC)
            xc = (cw[0:1] * ring_ref[sl[0], rs, :] + cw[1:2] * ring_ref[sl[1], rs, :]
                  + cw[2:3] * ring_ref[sl[2], rs, :] + cw[3:4] * ring_ref[sl[3], rs, :] + cb)
            xc_ref[ks, :] = xc
            xcb_ref[ks, :] = xc.astype(BF16)
    for q in range(nq):
        z_ref[q] = jnp.dot(xcb_ref[:, q * gt:(q + 1) * gt], wg_ref[q], preferred_element_type=F32)
    gbh = gb_ref[...]
    c2 = _lru_decay_scale(lam_ref[...])
    for k in range(LRU_RB):
        for q in range(nq):
            cs = slice(q * gt, (q + 1) * gt)
            for r0 in range(0, rows, LRU_RC):
                rs = slice(r0, r0 + LRU_RC)
                ks = slice(k * rows + r0, k * rows + r0 + LRU_RC)
                hx = 0.5 * xc_ref[ks, cs]
                _, af, bf = _lru_gate_math(hx, z_ref[q, ks, 0:gt], z_ref[q, ks, gt:2 * gt],
                                           gbh[0:1, cs], gbh[1:2, cs], c2[0:1, cs])
                lar, ar, br = _lru_gate_math(hx, z_ref[q, ks, 2 * gt:3 * gt], z_ref[q, ks, 3 * gt:4 * gt],
                                             gbh[2:3, cs], gbh[3:4, cs], c2[1:2, cs])
                h = af * hs_ref[rs, cs] + bf
                p = ps_ref[rs, cs] * af
                hs_ref[rs, cs] = h
                ps_ref[rs, cs] = p
                hl_ref[k, rs, cs] = h.astype(BF16)
                pc_ref[k, rs, cs] = p.astype(BF16)
                lar_ref[k, rs, cs] = lar.astype(BF16)
                br_ref[k, rs, cs] = br.astype(BF16)
                prod = pr_ref[rs, cs]
                hr_ref[rs, cs] = hr_ref[rs, cs] + prod * br
                pr_ref[rs, cs] = prod * ar


def _lru_pass_a(x4, gs8, sh8, w_all, cw, cb, wg, gb, lam):
    nb, nrow, w, d = x4.shape
    e = cw.shape[1]
    rows = nb * w
    rb = LRU_RB
    kern = functools.partial(_lru_pass_a_kernel, nrow=nrow)
    const = lambda a: pl.BlockSpec(a.shape, lambda i, nd=a.ndim: (0,) * nd)
    acc = pl.BlockSpec((rows, e), lambda i: (0, 0))
    acc_shape = jax.ShapeDtypeStruct((rows, e), F32)
    per_row = pl.BlockSpec((rb, rows, e), lambda i: (jnp.maximum(i - LRU_WARM, 0), 0, 0))
    per_row_shape = jax.ShapeDtypeStruct((nrow, rows, e), BF16)
    x_row = lambda first: pl.BlockSpec((nb, None, w, d), lambda i: (0, (rb * i + first) % nrow, 0, 0))
    return pl.pallas_call(
        kern,
        grid=(nrow // rb + LRU_WARM,),
        in_specs=[x_row(nrow - 1), x_row(0),
                  const(gs8), const(sh8), pl.BlockSpec((d, e), lambda i: (0, W_COL_ULRU)), const(cw), const(cb),
                  const(wg), const(gb), const(lam)],
        out_specs=[per_row] * 4 + [acc] * 4,
        out_shape=[per_row_shape] * 4 + [acc_shape] * 4,
        scratch_shapes=[pltpu.VMEM((8, rows, e), F32), pltpu.VMEM((rb * rows, d), BF16),
                        pltpu.VMEM((rb * rows, e), F32), pltpu.VMEM((rb * rows, e), BF16),
                        pltpu.VMEM((wg.shape[0], rb * rows, wg.shape[2]), F32)],
        compiler_params=_cparams(("arbitrary",)),
        name="lru_pass_a",
    )(x4, x4, gs8, sh8, w_all, cw, cb, wg, gb, lam)


def _lru_stitch_kernel(pf_ref, hf_ref, pr_ref, hr_ref, ff_ref, fr_ref, sf_ref, sr_ref, *, w):
    rows = pf_ref.shape[0]
    col = lax.broadcasted_iota(jnp.int32, pf_ref.shape, 0) % w
    p, h = pf_ref[...], hf_ref[...]
    k = 1
    while k < w:
        keep = col >= k
        ps = jnp.where(keep, pltpu.roll(p, k, 0), 1.0)
        hs = jnp.where(keep, pltpu.roll(h, k, 0), 0.0)
        p, h = p * ps, p * hs + h
        k *= 2
    fin = ff_ref[...]
    sf_ref[...] = jnp.where(col == 0, fin, pltpu.roll(p, 1, 0) * fin + pltpu.roll(h, 1, 0))
    p, h = pr_ref[...], hr_ref[...]
    k = 1
    while k < w:
        keep = col < w - k
        ps = jnp.where(keep, pltpu.roll(p, rows - k, 0), 1.0)
        hs = jnp.where(keep, pltpu.roll(h, rows - k, 0), 0.0)
        p, h = p * ps, p * hs + h
        k *= 2
    fin = fr_ref[...]
    sr_ref[...] = jnp.where(col == w - 1, fin, pltpu.roll(p, rows - 1, 0) * fin + pltpu.roll(h, rows - 1, 0))


def _lru_stitch(pf, hf, pr, hr, ff_rows, fr_rows, w):
    rows, e = pf.shape
    big = pl.BlockSpec((rows, e), lambda i: (0, 0))
    shp = jax.ShapeDtypeStruct((rows, e), F32)
    return pl.pallas_call(
        functools.partial(_lru_stitch_kernel, w=w),
        grid=(1,),
        in_specs=[big] * 6,
        out_specs=[big, big],
        out_shape=[shp, shp],
        compiler_params=_cparams(("arbitrary",)),
        name="lru_stitch",
    )(pf, hf, pr, hr, ff_rows, fr_rows)


def _lru_rev_out_kernel(lar_ref, br_ref, hl_ref, pc_ref, x_ref, m5_ref, gs_ref, sh_ref, gt_ref, hsf_ref, hsr_ref,
                        wgl_ref, wo_ref, fg_ref, o_ref, h_ref, xb_ref, g_ref, mb_ref, mix_ref):
    i = pl.program_id(0)
    nb, nrb, w, d = x_ref.shape
    rows = nb * w
    e = h_ref.shape[-1]

    @pl.when(i == 0)
    def _():
        h_ref[...] = hsr_ref[...]

    for rr in range(nrb):
        for b in range(nb):
            xb_ref[rr * rows + b * w:rr * rows + (b + 1) * w, :] = _norm_mod(
                x_ref[b, rr], gs_ref[b:b + 1], sh_ref[b:b + 1]).astype(BF16)
    g_ref[...] = jnp.dot(xb_ref[...], wgl_ref[...], preferred_element_type=F32)
    for rr in range(nrb - 1, -1, -1):
        for c0 in range(0, e, GATE_TILE):
            cs = slice(c0, c0 + GATE_TILE)
            for r0 in range(0, rows, LRU_RC):
                rs = slice(r0, r0 + LRU_RC)
                gs_rows = slice(rr * rows + r0, rr * rows + r0 + LRU_RC)
                h = jnp.exp2(lar_ref[rr, rs, cs].astype(F32)) * h_ref[rs, cs] + br_ref[rr, rs, cs].astype(F32)
                h_ref[rs, cs] = h
                y = hl_ref[rr, rs, cs].astype(F32) + pc_ref[rr, rs, cs].astype(F32) * hsf_ref[rs, cs] + h
                hg = g_ref[gs_rows, cs]
                mb_ref[gs_rows, cs] = (y * (hg * jnp.tanh(hg) + hg)).astype(BF16)
    m5 = jnp.concatenate([m5_ref[:, rr].reshape(rows, e) for rr in range(nrb)], axis=0)
    mix_ref[...] = (jnp.dot(m5, wo_ref[0:e, :], preferred_element_type=F32)
                    + jnp.dot(mb_ref[...], wo_ref[e:2 * e, :], preferred_element_type=F32))
    fg = fg_ref[...]
    for rr in range(nrb):
        for b in range(nb):
            for r0 in range(0, w, LRU_RC):
                m0 = rr * rows + b * w + r0
                res = x_ref[b, rr, r0:r0 + LRU_RC, :] + gt_ref[b:b + 1] * mix_ref[m0:m0 + LRU_RC, :]
                ms = jnp.mean(res * res, axis=-1, keepdims=True)
                o_ref[b, rr, r0:r0 + LRU_RC, :] = res * lax.rsqrt(ms + EPS) * fg


def _lru_rev_out(lar, br, hloc, pcum, x4, m5, gs8, sh8, gt8, hs_f, hs_r, w_all, wo, fg):
    nb, nrow, w, d = x4.shape
    rows, e = hs_f.shape
    const = lambda a: pl.BlockSpec(a.shape, lambda i, nd=a.ndim: (0,) * nd)
    rb = LRU_RB
    nstep = nrow // rb
    rev3 = pl.BlockSpec((rb, rows, e), lambda i: (nstep - 1 - i, 0, 0))
    rev4 = pl.BlockSpec((nb, rb, w, d), lambda i: (0, nstep - 1 - i, 0, 0))
    rev4e = pl.BlockSpec((nb, rb, w, e), lambda i: (0, nstep - 1 - i, 0, 0))
    return pl.pallas_call(
        _lru_rev_out_kernel,
        grid=(nstep,),
        in_specs=[rev3] * 4 + [rev4, rev4e, const(gs8), const(sh8), const(gt8), const(hs_f), const(hs_r),
                               pl.BlockSpec((d, e), lambda i: (0, W_COL_GLRU)), const(wo), const(fg)],
        out_specs=rev4,
        out_shape=jax.ShapeDtypeStruct((nb, nrow, w, d), F32),
        scratch_shapes=[pltpu.VMEM((rows, e), F32), pltpu.VMEM((rb * rows, d), BF16),
                        pltpu.VMEM((rb * rows, e), F32), pltpu.VMEM((rb * rows, e), BF16),
                        pltpu.VMEM((rb * rows, d), F32)],
        compiler_params=_cparams(("arbitrary",)),
        name="lru_rev_out",
    )(lar, br, hloc, pcum, x4, m5, gs8, sh8, gt8, hs_f, hs_r, w_all, wo, fg)


def _gate_tiles(w_a, w_x):
    ndir, heads, hd, _ = w_a.shape
    per = GATE_TILE // hd
    nq = heads // per

    def tiles(w):
        wq = w.reshape(nq, per, hd, hd)
        eye = jnp.eye(per, dtype=w.dtype)
        blk = wq[:, :, :, None, :] * eye[None, :, None, :, None]
        return blk.reshape(nq, GATE_TILE, GATE_TILE)

    return jnp.concatenate([tiles(w_a[0]), tiles(w_x[0]), tiles(w_a[1]), tiles(w_x[1])], axis=2).astype(BF16)


def kernel(x, c, ctx, c_ctx, w_mod, b_mod, norm_g, w_in, s5_a_re, s5_a_im, s5_log_step, s5_b_re, s5_b_im,
           s5_c_re, s5_c_im, s5_d, s5_w_glu, s5_b_glu, lru_conv_w, lru_conv_b, lru_w_a, lru_b_a, lru_w_x,
           lru_b_x, lru_lam, w_out, final_g):
    bsz, seq, d = x.shape
    nctx_tok = ctx.shape[1]
    e = s5_w_glu.shape[-1]
    assert w_mod.shape[0] == 1, "single-layer block"
    assert bsz <= 8 and seq % S5_SEG == 0 and seq % GRID_W == 0 and nctx_tok % S5_T == 0
    nseg = seq // S5_SEG
    ns = bsz * nseg
    nctx = nctx_tok // S5_T
    assert nctx <= nseg and ns % S5_PB == 0
    nrow = seq // GRID_W
    assert LRU_RB == 2 and nrow % LRU_RB == 0 and nrow >= 4

    c8 = jnp.zeros((8, d), F32).at[:bsz].set(c).at[bsz].set(c_ctx)
    mod = _modulation(c8, w_mod[0], b_mod[0])
    sh, sc, gt = mod[:, :d], mod[:, d:2 * d], mod[:, 2 * d:]
    gs = norm_g[0][None, :] * (1.0 + sc)

    w_in0 = w_in[0]
    assert w_in0.shape == (d, 4 * e)
    wt_us5 = w_in0[:, W_COL_US5 * e:(W_COL_US5 + 1) * e].T.astype(BF16)
    gate_cols = (jnp.arange(4 * e) // e == W_COL_GS5) | (jnp.arange(4 * e) // e == W_COL_GLRU)
    w_all = (w_in0 * jnp.where(gate_cols, 0.5, 1.0)[None, :]).astype(BF16)
    w_glu = (0.5 * s5_w_glu[0]).astype(BF16)
    wo = w_out[0].astype(BF16)

    s5_params = _s5_param_layout(s5_a_re[0], s5_a_im[0], s5_log_step[0], s5_b_re[0], s5_b_im[0],
                                 s5_c_re[0], s5_c_im[0], s5_d[0])
    x3 = x.reshape(ns, S5_SEG, d)
    xt, hg5 = _s5_inproj(x3, gs, sh, wt_us5, w_all, bsz)
    ctx_pad = jnp.zeros((bsz, nseg, S5_T, d), F32).at[:, :nctx].set(ctx.reshape(bsz, nctx, S5_T, d))
    gs_c = jnp.broadcast_to(gs[bsz:bsz + 1], (8, d))
    sh_c = jnp.broadcast_to(sh[bsz:bsz + 1], (8, d))
    xct, _ = _s5_inproj(ctx_pad.reshape(ns, S5_T, d), gs_c, sh_c, wt_us5, w_all, bsz)
    y5 = _s5_core(xt, xct, s5_params, nseg, nctx)
    nj = S5_SEG // S5_T
    m5 = _s5_post(y5.reshape(nj, S5_T, e, ns), hg5, w_glu, 0.5 * s5_b_glu[0].reshape(1, e))

    wg_all = _gate_tiles(0.5 * lru_w_a[0], 0.5 * lru_w_x[0])
    gb_all = 0.5 * jnp.stack([lru_b_a[0, 0], lru_b_x[0, 0], lru_b_a[0, 1], lru_b_x[0, 1]])
    cw = lru_conv_w[0]
    cb = lru_conv_b[0].reshape(1, e)
    lam = lru_lam[0]
    ctx_p = jnp.zeros((8, nctx_tok, d), F32).at[:bsz].set(ctx).transpose(1, 0, 2).reshape(nctx_tok * 8, d)
    ff, fr = _lru_ctx(ctx_p, gs[bsz:bsz + 1], sh[bsz:bsz + 1], w_all, cw, cb, wg_all, gb_all, lam)
    x4 = x.reshape(bsz, nrow, GRID_W, d)
    lar, br, hloc, pcum, hf, pf, pr, hr = _lru_pass_a(x4, gs, sh, w_all, cw, cb, wg_all, gb_all, lam)
    hs_f, hs_r = _lru_stitch(pf, hf, pr, hr, jnp.repeat(ff[:bsz], GRID_W, axis=0),
                             jnp.repeat(fr[:bsz], GRID_W, axis=0), GRID_W)
    out4 = _lru_rev_out(lar, br, hloc, pcum, x4, m5.reshape(bsz, nrow, GRID_W, e), gs, sh, gt, hs_f, hs_r,
                        w_all, wo, final_g.reshape(1, d))
    return out4.reshape(bsz, seq, d)
```

```python
import functools
import math

import jax
import jax.numpy as jnp
from jax import lax
from jax.experimental import pallas as pl
from jax.experimental.pallas import tpu as pltpu

F32 = jnp.float32
BF16 = jnp.bfloat16

EPS = 1e-6
TINY = 1e-30
GRID_W = 64
LRU_C = 8.0
S5_H = 16
S5_N = 64
S5_T = 16
S5_SEG = 256
S5_TB = 8
S5_PB = 16
S5_GB = 2
GATE_TILE = 256
LRU_RB = 2
LRU_WARM = 2
LRU_RC = 32
W_COL_US5, W_COL_GS5, W_COL_ULRU, W_COL_GLRU = 0, 1, 2, 3
V7X_VMEM_BYTES = 64 * 1024 * 1024
VMEM_LIMIT = V7X_VMEM_BYTES * 7 // 8


def _cparams(sem):
    return pltpu.CompilerParams(dimension_semantics=sem, vmem_limit_bytes=VMEM_LIMIT)


def _sigmoid(z):
    return 0.5 * (jnp.tanh(0.5 * z) + 1.0)


def _gelu_tanh(y):
    return 0.5 * y * (1.0 + jnp.tanh(math.sqrt(2.0 / math.pi) * (y + 0.044715 * (y * y * y))))


def _norm_mod(x, gs, sh):
    ms = jnp.mean(x * x, axis=-1, keepdims=True)
    return x * lax.rsqrt(ms + EPS) * gs + sh


def _mod_kernel(c_ref, w_ref, b_ref, o_ref):
    c = c_ref[...]
    s = c * _sigmoid(c)
    o_ref[...] = jnp.dot(s, w_ref[...], preferred_element_type=F32) + b_ref[...]


def _modulation(c8, w_mod, b_mod):
    d = c8.shape[1]
    n = w_mod.shape[1]
    nb = n // d
    return pl.pallas_call(
        _mod_kernel,
        grid=(nb,),
        in_specs=[pl.BlockSpec((8, d), lambda i: (0, 0)),
                  pl.BlockSpec((d, d), lambda i: (0, i)),
                  pl.BlockSpec((1, d), lambda i: (0, i))],
        out_specs=pl.BlockSpec((8, d), lambda i: (0, i)),
        out_shape=jax.ShapeDtypeStruct((8, n), F32),
        compiler_params=_cparams(("arbitrary",)),
    )(c8, w_mod, b_mod.reshape(1, n))


def _perm_matrix(n_a, n_b):
    n = n_a * n_b
    r = lax.broadcasted_iota(jnp.int32, (n, n), 0)
    c = lax.broadcasted_iota(jnp.int32, (n, n), 1)
    return ((r // n_b == c % n_a) & (r % n_b == c // n_a)).astype(BF16)


def _s5_inproj_kernel(x_ref, gs_ref, sh_ref, wt_ref, wg_ref, o_ref, hg_ref, xs_ref, *, nbatch):
    ns, tb, d = x_ref.shape
    groups = o_ref.shape[0]
    rows_b = (ns // nbatch) * tb
    blk = S5_PB * tb
    perm = _perm_matrix(tb, S5_PB)
    x = x_ref[...].reshape(ns * tb, d)
    xn = jnp.concatenate(
        [_norm_mod(x[b * rows_b:(b + 1) * rows_b], gs_ref[b:b + 1], sh_ref[b:b + 1]).astype(BF16)
         for b in range(nbatch)], axis=0)
    for k in range(ns // S5_PB):
        q = jnp.dot(perm, xn[k * blk:(k + 1) * blk], preferred_element_type=F32).astype(BF16)
        for t in range(tb):
            xs_ref[t * ns + k * S5_PB:t * ns + (k + 1) * S5_PB, :] = q[t * S5_PB:(t + 1) * S5_PB]
    wt = wt_ref[...]
    for t in range(0, tb, 2):
        ut = lax.dot_general(wt, xs_ref[t * ns:(t + 2) * ns, :], (((1,), (1,)), ((), ())),
                             preferred_element_type=F32)
        o_ref[:, t * S5_H:(t + 1) * S5_H, :] = ut[:, :ns].reshape(groups, S5_H, ns).astype(BF16)
        o_ref[:, (t + 1) * S5_H:(t + 2) * S5_H, :] = ut[:, ns:].reshape(groups, S5_H, ns).astype(BF16)
    hg = jnp.dot(xs_ref[...], wg_ref[...], preferred_element_type=F32)
    hg_ref[...] = hg.astype(BF16).reshape(tb, ns, hg.shape[1])


def _s5_inproj(x3, gs8, sh8, wt, w_all, nbatch):
    ns, seg, d = x3.shape
    e = wt.shape[0]
    groups = e // S5_H
    nj = seg // S5_T
    per = S5_T // S5_TB
    return pl.pallas_call(
        functools.partial(_s5_inproj_kernel, nbatch=nbatch),
        grid=(seg // S5_TB,),
        in_specs=[pl.BlockSpec((ns, S5_TB, d), lambda i: (0, i, 0)),
                  pl.BlockSpec((8, d), lambda i: (0, 0)),
                  pl.BlockSpec((8, d), lambda i: (0, 0)),
                  pl.BlockSpec((e, d), lambda i: (0, 0)),
                  pl.BlockSpec((d, e), lambda i: (0, W_COL_GS5))],
        out_specs=[pl.BlockSpec((groups, S5_TB * S5_H, ns), lambda i: (0, i % per, i // per)),
                   pl.BlockSpec((S5_TB, ns, e), lambda i: (i, 0, 0))],
        out_shape=[jax.ShapeDtypeStruct((groups, S5_T * S5_H, nj * ns), BF16),
                   jax.ShapeDtypeStruct((seg, ns, e), BF16)],
        scratch_shapes=[pltpu.VMEM((S5_TB * ns, d), BF16)],
        compiler_params=_cparams(("arbitrary",)),
        name="s5_inproj",
    )(x3, gs8, sh8, wt, w_all)


def _cmul_add(dr, di, hr, hi, sr, si):
    return dr * hr - di * hi + sr, dr * hi + di * hr + si


def _lane_scan(vr, vi, dr, di, slot, count, reverse):
    n = vr.shape[1]
    k = 1
    while k < count:
        if reverse:
            sr = pltpu.roll(vr, n - k, 1)
            si = pltpu.roll(vi, n - k, 1)
            keep = slot < count - k
        else:
            sr = pltpu.roll(vr, k, 1)
            si = pltpu.roll(vi, k, 1)
            keep = slot >= k
        sr = jnp.where(keep, sr, 0.0)
        si = jnp.where(keep, si, 0.0)
        vr, vi = _cmul_add(dr, di, sr, si, vr, vi)
        dr, di = dr * dr - di * di, 2.0 * dr * di
        k *= 2
    return vr, vi


def _csq(r, i):
    return r * r - i * i, 2.0 * r * i


def _cpow_bits(e, pows):
    pr = jnp.ones(e.shape, F32)
    pi = jnp.zeros(e.shape, F32)
    for bit, (qr, qi) in pows:
        on = (e & bit) != 0
        pr, pi = jnp.where(on, pr * qr - pi * qi, pr), jnp.where(on, pr * qi + pi * qr, pi)
    return pr, pi


def _s5_discretize(ar, ai, ls):
    step = jnp.exp(ls)
    mag = jnp.exp(ar * step)
    pr = mag * jnp.cos(ai * step)
    pi = mag * jnp.sin(ai * step)
    den = ar * ar + ai * ai
    nr = pr - 1.0
    return pr, pi, (nr * ar + pi * ai) / den, (pi * ar - nr * ai) / den


def _s5_operators(pcol_ref, prow_ref, bt_ref, ct1_ref, ct2_ref, dsk_ref):
    n, t_len, th = S5_N, S5_T, S5_T * S5_H
    hp = lax.Precision.HIGHEST
    lane_t = lax.broadcasted_iota(jnp.int32, (1, th), 1) // S5_H
    bits = [1 << b for b in range(t_len.bit_length() - 1)]

    tile_h = (lax.broadcasted_iota(jnp.int32, (S5_H, th), 1) % S5_H
              == lax.broadcasted_iota(jnp.int32, (S5_H, th), 0)).astype(F32)
    bt = jnp.dot(bt_ref[...], tile_h, precision=hp, preferred_element_type=F32)

    blocks, bbs, decay = [], [], []
    for d in range(2):
        pr, pi, cr, ci = _s5_discretize(pcol_ref[3 * d], pcol_ref[3 * d + 1], pcol_ref[3 * d + 2])
        pows, q = [], (pr, pi)
        for bit in bits:
            pows.append((bit, q))
            q = _csq(*q)
        decay.append(q)
        btr = bt[2 * d * n:(2 * d + 1) * n, :]
        bti = bt[(2 * d + 1) * n:(2 * d + 2) * n, :]
        bbr = cr * btr - ci * bti
        bbi = cr * bti + ci * btr
        e = (t_len - 1 - lane_t) if d == 0 else lane_t
        wr, wi = _cpow_bits(jnp.broadcast_to(e, (n, th)), pows)
        blocks += [wr * bbr - wi * bbi, wr * bbi + wi * bbr]
        bbs.append(jnp.concatenate([bbr, bbi], axis=0))
    bpow = jnp.concatenate(blocks, axis=0).astype(BF16)

    qr, qi, _, _ = _s5_discretize(prow_ref[0:1], prow_ref[1:2], prow_ref[2:3])
    pows, q = [], (qr, qi)
    for bit in bits:
        pows.append((bit, q))
        q = _csq(*q)
    q_t = q
    row_t = lax.broadcasted_iota(jnp.int32, (th, 1), 0) // S5_H
    is_f = lax.broadcasted_iota(jnp.int32, (1, 4 * n), 1) < 2 * n
    e_k = jnp.where(is_f, row_t, (t_len - row_t) % t_len)
    kr, ki = _cpow_bits(e_k, pows)
    nr = jnp.where(is_f, kr * qr - ki * qi, jnp.where(row_t == 0, q_t[0], kr))
    ni = jnp.where(is_f, kr * qi + ki * qr, jnp.where(row_t == 0, q_t[1], ki))
    ct1 = jnp.concatenate([ct1_ref[...]] * t_len, axis=0)
    ct2 = jnp.concatenate([ct2_ref[...]] * t_len, axis=0)
    cpow = (ct1 * nr + ct2 * ni).astype(BF16)
    qk = ct1 * kr + ct2 * ki
    mf = jnp.dot(qk[:, :2 * n], bbs[0], precision=hp, preferred_element_type=F32)
    mr = jnp.dot(qk[:, 2 * n:], bbs[1], precision=hp, preferred_element_type=F32)
    m = (jnp.where(row_t <= t_len - 1 - lane_t, mf, 0.0)
         + jnp.where((row_t == 0) | (row_t >= t_len - lane_t), mr, 0.0))
    for bit in bits:
        m = jnp.where((lane_t & bit) != 0, pltpu.roll(m, S5_H * bit, 0), m)
    row = lax.broadcasted_iota(jnp.int32, (th, th), 0)
    col = lax.broadcasted_iota(jnp.int32, (th, th), 1)
    toep = (m + jnp.where(row == col, dsk_ref[...], 0.0)).astype(BF16)
    return toep, bpow, cpow, decay


def _s5_core_kernel(xt_ref, xc_ref, pcol_ref, pp_ref, bt_ref, o_ref, s_ref, *, nseg, nctx):
    for gi in range(xt_ref.shape[0]):
        _s5_core_group(xt_ref.at[gi], xc_ref.at[gi], pcol_ref.at[gi], pp_ref.at[gi], bt_ref.at[gi],
                       o_ref.at[:, :, gi], s_ref, nseg, nctx)


def _s5_core_group(xt_ref, xc_ref, pcol_ref, pp_ref, bt_ref, o_ref, s_ref, nseg, nctx):
    n = S5_N
    h = S5_H
    ns = xc_ref.shape[-1]
    nj = xt_ref.shape[-1] // ns
    toep, bpow, cpow, decay = _s5_operators(pcol_ref, pp_ref.at[2 * h:2 * h + 3], bt_ref, pp_ref.at[0:h],
                                            pp_ref.at[h:2 * h], pp_ref.at[2 * h + 3:2 * h + 4])
    dfr = jnp.broadcast_to(decay[0][0], (n, ns))
    dfi = jnp.broadcast_to(decay[0][1], (n, ns))
    drr = jnp.broadcast_to(decay[1][0], (n, ns))
    dri = jnp.broadcast_to(decay[1][1], (n, ns))
    slot = lax.broadcasted_iota(jnp.int32, (n, ns), 1) % nseg

    sc = jnp.dot(bpow, xc_ref[...], preferred_element_type=F32)
    valid = slot < nctx
    cfr, cfi = _lane_scan(jnp.where(valid, sc[0:n], 0.0), jnp.where(valid, sc[n:2 * n], 0.0),
                          dfr, dfi, slot, nseg, False)
    crr, cri = _lane_scan(jnp.where(valid, sc[2 * n:3 * n], 0.0), jnp.where(valid, sc[3 * n:4 * n], 0.0),
                          drr, dri, slot, nseg, True)

    step = 2 if nj % 2 == 0 else 1
    for j in range(0, nj, step):
        cols = slice(j * ns, (j + step) * ns)
        s_ref[:, cols] = jnp.dot(bpow, xt_ref[:, cols], preferred_element_type=F32)

    zero = jnp.zeros((n, ns), F32)
    efr, efi, err, eri = zero, zero, zero, zero
    for j in range(nj):
        cols = slice(j * ns, (j + 1) * ns)
        efr, efi = _cmul_add(dfr, dfi, efr, efi, s_ref[0:n, cols], s_ref[n:2 * n, cols])
    for j in range(nj - 1, -1, -1):
        cols = slice(j * ns, (j + 1) * ns)
        err, eri = _cmul_add(drr, dri, err, eri, s_ref[2 * n:3 * n, cols], s_ref[3 * n:4 * n, cols])

    sfr, sfi, srr, sri = dfr, dfi, drr, dri
    k = 1
    while k < nj:
        sfr, sfi = sfr * sfr - sfi * sfi, 2.0 * sfr * sfi
        srr, sri = srr * srr - sri * sri, 2.0 * srr * sri
        k *= 2

    first = slot == 0
    last = slot == nseg - 1
    vfr = jnp.where(first, pltpu.roll(cfr, ns - (nctx - 1), 1) if nctx > 1 else cfr, pltpu.roll(efr, 1, 1))
    vfi = jnp.where(first, pltpu.roll(cfi, ns - (nctx - 1), 1) if nctx > 1 else cfi, pltpu.roll(efi, 1, 1))
    vrr = jnp.where(last, pltpu.roll(crr, nseg - 1, 1) if nseg > 1 else crr, pltpu.roll(err, ns - 1, 1))
    vri = jnp.where(last, pltpu.roll(cri, nseg - 1, 1) if nseg > 1 else cri, pltpu.roll(eri, ns - 1, 1))
    hfr, hfi = _lane_scan(vfr, vfi, sfr, sfi, slot, nseg, False)
    hrr, hri = _lane_scan(vrr, vri, srr, sri, slot, nseg, True)

    for j in range(nj):
        cols = slice(j * ns, (j + 1) * ns)
        nr, ni = _cmul_add(dfr, dfi, hfr, hfi, s_ref[0:n, cols], s_ref[n:2 * n, cols])
        s_ref[0:n, cols] = hfr
        s_ref[n:2 * n, cols] = hfi
        hfr, hfi = nr, ni
    for j in range(nj - 1, -1, -1):
        cols = slice(j * ns, (j + 1) * ns)
        nr, ni = _cmul_add(drr, dri, hrr, hri, s_ref[2 * n:3 * n, cols], s_ref[3 * n:4 * n, cols])
        s_ref[2 * n:3 * n, cols] = hrr
        s_ref[3 * n:4 * n, cols] = hri
        hrr, hri = nr, ni

    for j in range(0, nj, step):
        cols = slice(j * ns, (j + step) * ns)
        y = (jnp.dot(toep, xt_ref[:, cols], preferred_element_type=F32)
             + jnp.dot(cpow, s_ref[:, cols].astype(BF16), preferred_element_type=F32))
        for q in range(step):
            o_ref[j + q] = y[:, q * ns:(q + 1) * ns].reshape(S5_T, S5_H, ns).astype(BF16)


def _s5_core(xt, xct, params, nseg, nctx):
    groups, rows, lanes = xt.shape
    ns = xct.shape[-1]
    nj = lanes // ns
    kern = functools.partial(_s5_core_kernel, nseg=nseg, nctx=nctx)
    per_group = lambda a: pl.BlockSpec((S5_GB,) + a.shape[1:], lambda g, nd=a.ndim: (g,) + (0,) * (nd - 1))
    return pl.pallas_call(
        kern,
        grid=(groups // S5_GB,),
        in_specs=[per_group(xt), per_group(xct)] + [per_group(a) for a in params],
        out_specs=pl.BlockSpec((nj, S5_T, S5_GB, S5_H, ns), lambda g: (0, 0, g, 0, 0)),
        out_shape=jax.ShapeDtypeStruct((nj, S5_T, groups, S5_H, ns), BF16),
        scratch_shapes=[pltpu.VMEM((rows, lanes), F32)],
        compiler_params=_cparams(("arbitrary",)),
        name="s5_core",
    )(xt, xct, *params)


def _s5_param_layout(a_re, a_im, log_step, b_re, b_im, c_re, c_im, d_skip):
    ndir, groups, n = a_re.shape
    h = b_re.shape[-1]
    ls = jnp.broadcast_to(log_step[..., None], a_re.shape)
    pcol = jnp.stack([a_re[0], a_im[0], ls[0], a_re[1], a_im[1], ls[1]], axis=1)[..., None]
    lay = lambda v: jnp.concatenate([v[0], v[0], v[1], v[1]], axis=-1)
    prow = jnp.stack([lay(a_re), lay(a_im), lay(ls)], axis=1)
    bt = jnp.concatenate([b_re[0], b_im[0], b_re[1], b_im[1]], axis=1)
    ct1 = jnp.concatenate([c_re[0], -c_im[0], c_re[1], -c_im[1]], axis=-1)
    ct2 = jnp.concatenate([-c_im[0], -c_re[0], -c_im[1], -c_re[1]], axis=-1)
    dsk = jnp.tile(d_skip.reshape(groups, 1, h), (1, 1, S5_T))
    pad = jnp.zeros((groups, (-(2 * h + 4)) % 8, S5_T * h), F32)
    pp = jnp.concatenate([ct1, ct2, prow, dsk, pad], axis=1)
    return tuple(v.astype(F32) for v in (pcol, pp, bt))


def _s5_post_kernel(y_ref, hg_ref, wglu_ref, bglu_ref, o_ref, m_ref):
    q = pl.program_id(1)
    tb, e, ns = y_ref.shape
    ys = jnp.concatenate([y_ref[t].astype(F32).T for t in range(tb)], axis=0)
    yg = _gelu_tanh(ys)
    zh = jnp.dot(yg.astype(BF16), wglu_ref[...], preferred_element_type=F32) + bglu_ref[...]
    hg = hg_ref[...].reshape(tb * ns, e).astype(F32)
    m = ((yg * (0.5 * jnp.tanh(zh) + 0.5)) * (hg * jnp.tanh(hg) + hg)).astype(BF16)
    nq = S5_T // tb
    for half in range(nq):
        @pl.when(q == half)
        def _(half=half):
            m_ref[half * tb * ns:(half + 1) * tb * ns, :] = m

    @pl.when(q == nq - 1)
    def _():
        perm = _perm_matrix(S5_PB, S5_T)
        for k in range(ns // S5_PB):
            g = jnp.concatenate([m_ref[t * ns + k * S5_PB:t * ns + (k + 1) * S5_PB, :] for t in range(S5_T)], axis=0)
            r = jnp.dot(perm, g, preferred_element_type=F32).astype(BF16)
            o_ref[k * S5_PB:(k + 1) * S5_PB] = r.reshape(S5_PB, S5_T, e)


def _s5_post(y4, hg, wglu, bglu):
    nj, t_all, e, ns = y4.shape
    nt = t_all // S5_TB
    return pl.pallas_call(
        _s5_post_kernel,
        grid=(nj, nt),
        in_specs=[pl.BlockSpec((None, S5_TB, e, ns), lambda j, q: (j, q, 0, 0)),
                  pl.BlockSpec((S5_TB, ns, e), lambda j, q: (j * nt + q, 0, 0)),
                  pl.BlockSpec((e, e), lambda j, q: (0, 0)),
                  pl.BlockSpec((1, e), lambda j, q: (0, 0))],
        out_specs=pl.BlockSpec((ns, S5_T, e), lambda j, q: (0, j, 0)),
        out_shape=jax.ShapeDtypeStruct((ns, nj * S5_T, e), BF16),
        scratch_shapes=[pltpu.VMEM((S5_T * ns, e), BF16)],
        compiler_params=_cparams(("arbitrary", "arbitrary")),
        name="s5_post",
    )(y4, hg, wglu, bglu)


def _softplus(z):
    return jnp.maximum(z, 0.0) + jnp.log1p(jnp.exp(-jnp.abs(z)))


def _gate_matmul(xc, wg_ref):
    xb = xc.astype(BF16)
    nq = wg_ref.shape[0]
    return [jnp.dot(xb[:, q * GATE_TILE:(q + 1) * GATE_TILE], wg_ref[q], preferred_element_type=F32)
            for q in range(nq)]


def _gate_cols(zs, k):
    return jnp.concatenate([z[:, k * GATE_TILE:(k + 1) * GATE_TILE] for z in zs], axis=1)


def _lru_decay_scale(lam_row):
    return (-0.5 * LRU_C * math.log2(math.e)) * _softplus(-lam_row)


def _lru_gate_math(hx, za, zx, ba, bx, c2):
    ta = jnp.tanh(za + ba)
    tx = jnp.tanh(zx + bx)
    la = ta * c2 + c2
    a = jnp.exp2(la)
    om = 1.0 - a * a
    return la, a, (om * lax.rsqrt(jnp.maximum(om, TINY))) * (tx * hx + hx)


def _lru_coeffs(hx, zs, k, gbh, c2):
    return _lru_gate_math(hx, _gate_cols(zs, 2 * k), _gate_cols(zs, 2 * k + 1),
                          gbh[2 * k:2 * k + 1], gbh[2 * k + 1:2 * k + 2], c2)[1:]


def _norm_mod_rows(x_ref, gs_ref, sh_ref, dst_ref):
    nb, w, _ = x_ref.shape
    for b in range(nb):
        dst_ref[b * w:(b + 1) * w, :] = _norm_mod(x_ref[b], gs_ref[b:b + 1], sh_ref[b:b + 1]).astype(BF16)


def _lru_ctx_kernel(x_ref, gs_ref, sh_ref, wu_ref, cw_ref, cb_ref, wg_ref, gb_ref, lam_ref,
                    ff_ref, fr_ref, af_s, bf_s, ar_s, br_s):
    rows = x_ref.shape[0]
    npos = rows // 8
    xn = _norm_mod(x_ref[...], gs_ref[...], sh_ref[...]).astype(BF16)
    u = jnp.dot(xn, wu_ref[...], preferred_element_type=F32)
    pos = lax.broadcasted_iota(jnp.int32, u.shape, 0) // 8
    um1 = jnp.where(pos >= 1, pltpu.roll(u, 8, 0), 0.0)
    up1 = jnp.where(pos < npos - 1, pltpu.roll(u, rows - 8, 0), 0.0)
    up2 = jnp.where(pos < npos - 2, pltpu.roll(u, rows - 16, 0), 0.0)
    cw = cw_ref[...]
    xc = cw[0:1] * um1 + cw[1:2] * u + cw[2:3] * up1 + cw[3:4] * up2 + cb_ref[...]
    zs = _gate_matmul(xc, wg_ref)
    gbh = gb_ref[...]
    c2 = _lru_decay_scale(lam_ref[...])
    hx = 0.5 * xc
    a, b = _lru_coeffs(hx, zs, 0, gbh, c2[0:1])
    af_s[...] = a
    bf_s[...] = b
    a, b = _lru_coeffs(hx, zs, 1, gbh, c2[1:2])
    ar_s[...] = a
    br_s[...] = b

    def body(p, carry):
        hf, hr, pr = carry
        r0 = pl.multiple_of(p * 8, 8)
        hf = af_s[pl.ds(r0, 8), :] * hf + bf_s[pl.ds(r0, 8), :]
        hr = hr + pr * br_s[pl.ds(r0, 8), :]
        pr = pr * ar_s[pl.ds(r0, 8), :]
        return hf, hr, pr

    zero = jnp.zeros((8, u.shape[1]), F32)
    hf, hr, _ = lax.fori_loop(0, npos, body, (zero, zero, zero + 1.0))
    ff_ref[...] = hf
    fr_ref[...] = hr


def _lru_ctx(xp, gs, sh, w_all, cw, cb, wg, gb, lam):
    rows, d = xp.shape
    e = cw.shape[1]
    nq = e // GATE_TILE
    full = lambda a: pl.BlockSpec(a.shape, lambda q, nd=a.ndim: (0,) * nd)
    cols = lambda a: pl.BlockSpec((a.shape[0], GATE_TILE), lambda q: (0, q))
    wu_cols = pl.BlockSpec((d, GATE_TILE), lambda q: (0, W_COL_ULRU * nq + q))
    return pl.pallas_call(
        _lru_ctx_kernel,
        grid=(nq,),
        in_specs=[full(xp), full(gs), full(sh), wu_cols, cols(cw), cols(cb),
                  pl.BlockSpec((1,) + wg.shape[1:], lambda q: (q, 0, 0)), cols(gb), cols(lam)],
        out_specs=[pl.BlockSpec((8, GATE_TILE), lambda q: (0, q)), pl.BlockSpec((8, GATE_TILE), lambda q: (0, q))],
        out_shape=[jax.ShapeDtypeStruct((8, e), F32), jax.ShapeDtypeStruct((8, e), F32)],
        scratch_shapes=[pltpu.VMEM((rows, GATE_TILE), F32) for _ in range(4)],
        compiler_params=_cparams(("arbitrary",)),
    )(xp, gs, sh, w_all, cw, cb, wg, gb, lam)


def _lru_pass_a_kernel(xa_ref, xb_ref, gs_ref, sh_ref, wu_ref, cw_ref, cb_ref, wg_ref, gb_ref, lam_ref,
                       lar_ref, br_ref, hl_ref, pc_ref, hs_ref, ps_ref, pr_ref, hr_ref,
                       ring_ref, xn_ref, xc_ref, xcb_ref, z_ref, *, nrow):
    i = pl.program_id(0)
    nb, w, d = xa_ref.shape
    rows = nb * w
    nq = wg_ref.shape[0]
    gt = GATE_TILE

    @pl.when(i == 0)
    def _():
        ring_ref[...] = jnp.zeros_like(ring_ref)

    @pl.when(i <= LRU_WARM)
    def _():
        hs_ref[...] = jnp.zeros_like(hs_ref)
        ps_ref[...] = jnp.ones_like(ps_ref)
        hr_ref[...] = jnp.zeros_like(hr_ref)
        pr_ref[...] = jnp.ones_like(pr_ref)

    r = lax.broadcasted_iota(jnp.int32, (rows, rows), 0)
    c = lax.broadcasted_iota(jnp.int32, (rows, rows), 1)
    same_batch = r // w == c // w
    shifted = []
    for k, x_ref in enumerate((xa_ref, xb_ref)):
        _norm_mod_rows(x_ref, gs_ref, sh_ref, xn_ref.at[k * rows:(k + 1) * rows])
        s = 2 * i + k
        off = jnp.where(s == 0, -1, jnp.where(s > nrow, 1, 0))
        mix = ((c == r + off) & same_batch).astype(BF16)
        shifted.append(jnp.dot(mix, xn_ref[k * rows:(k + 1) * rows, :], preferred_element_type=F32).astype(BF16))
    u = jnp.dot(jnp.concatenate(shifted, axis=0), wu_ref[...], preferred_element_type=F32)
    ring_ref[(2 * i) & 7] = u[0:rows]
    ring_ref[(2 * i + 1) & 7] = u[rows:2 * rows]

    cw = cw_ref[...]
    cb = cb_ref[...]
    for k in range(LRU_RB):
        sl = [(2 * i - 2 * LRU_WARM + k + t) & 7 for t in range(4)]
        for r0 in range(0, rows, LRU_RC):
            rs = slice(r0, r0 + LRU_RC)
            ks = slice(k * rows + r0, k * rows + r0 + LRU_RC)
            xc = (cw[0:1] * ring_ref[sl[0], rs, :] + cw[1:2] * ring_ref[sl[1], rs, :]
                  + cw[2:3] * ring_ref[sl[2], rs, :] + cw[3:4] * ring_ref[sl[3], rs, :] + cb)
            xc_ref[ks, :] = xc
            xcb_ref[ks, :] = xc.astype(BF16)
    for q in range(nq):
        z_ref[q] = jnp.dot(xcb_ref[:, q * gt:(q + 1) * gt], wg_ref[q], preferred_element_type=F32)
    gbh = gb_ref[...]
    c2 = _lru_decay_scale(lam_ref[...])
    for k in range(LRU_RB):
        for q in range(nq):
            cs = slice(q * gt, (q + 1) * gt)
            for r0 in range(0, rows, LRU_RC):
                rs = slice(r0, r0 + LRU_RC)
                ks = slice(k * rows + r0, k * rows + r0 + LRU_RC)
                hx = 0.5 * xc_ref[ks, cs]
                _, af, bf = _lru_gate_math(hx, z_ref[q, ks, 0:gt], z_ref[q, ks, gt:2 * gt],
                                           gbh[0:1, cs], gbh[1:2, cs], c2[0:1, cs])
                lar, ar, br = _lru_gate_math(hx, z_ref[q, ks, 2 * gt:3 * gt], z_ref[q, ks, 3 * gt:4 * gt],
                                             gbh[2:3, cs], gbh[3:4, cs], c2[1:2, cs])
                h = af * hs_ref[rs, cs] + bf
                p = ps_ref[rs, cs] * af
                hs_ref[rs, cs] = h
                ps_ref[rs, cs] = p
                hl_ref[k, rs, cs] = h.astype(BF16)
                pc_ref[k, rs, cs] = p.astype(BF16)
                lar_ref[k, rs, cs] = lar.astype(BF16)
                br_ref[k, rs, cs] = br.astype(BF16)
                prod = pr_ref[rs, cs]
                hr_ref[rs, cs] = hr_ref[rs, cs] + prod * br
                pr_ref[rs, cs] = prod * ar


def _lru_pass_a(x4, gs8, sh8, w_all, cw, cb, wg, gb, lam):
    nb, nrow, w, d = x4.shape
    e = cw.shape[1]
    rows = nb * w
    rb = LRU_RB
    kern = functools.partial(_lru_pass_a_kernel, nrow=nrow)
    const = lambda a: pl.BlockSpec(a.shape, lambda i, nd=a.ndim: (0,) * nd)
    acc = pl.BlockSpec((rows, e), lambda i: (0, 0))
    acc_shape = jax.ShapeDtypeStruct((rows, e), F32)
    per_row = pl.BlockSpec((rb, rows, e), lambda i: (jnp.maximum(i - LRU_WARM, 0), 0, 0))
    per_row_shape = jax.ShapeDtypeStruct((nrow, rows, e), BF16)
    x_row = lambda first: pl.BlockSpec((nb, None, w, d), lambda i: (0, (rb * i + first) % nrow, 0, 0))
    return pl.pallas_call(
        kern,
        grid=(nrow // rb + LRU_WARM,),
        in_specs=[x_row(nrow - 1), x_row(0),
                  const(gs8), const(sh8), pl.BlockSpec((d, e), lambda i: (0, W_COL_ULRU)), const(cw), const(cb),
                  const(wg), const(gb), const(lam)],
        out_specs=[per_row] * 4 + [acc] * 4,
        out_shape=[per_row_shape] * 4 + [acc_shape] * 4,
        scratch_shapes=[pltpu.VMEM((8, rows, e), F32), pltpu.VMEM((rb * rows, d), BF16),
                        pltpu.VMEM((rb * rows, e), F32), pltpu.VMEM((rb * rows, e), BF16),
                        pltpu.VMEM((wg.shape[0], rb * rows, wg.shape[2]), F32)],
        compiler_params=_cparams(("arbitrary",)),
        name="lru_pass_a",
    )(x4, x4, gs8, sh8, w_all, cw, cb, wg, gb, lam)


def _lru_stitch_kernel(pf_ref, hf_ref, pr_ref, hr_ref, ff_ref, fr_ref, sf_ref, sr_ref, *, w):
    rows = pf_ref.shape[0]
    col = lax.broadcasted_iota(jnp.int32, pf_ref.shape, 0) % w
    p, h = pf_ref[...], hf_ref[...]
    k = 1
    while k < w:
        keep = col >= k
        ps = jnp.where(keep, pltpu.roll(p, k, 0), 1.0)
        hs = jnp.where(keep, pltpu.roll(h, k, 0), 0.0)
        p, h = p * ps, p * hs + h
        k *= 2
    fin = ff_ref[...]
    sf_ref[...] = jnp.where(col == 0, fin, pltpu.roll(p, 1, 0) * fin + pltpu.roll(h, 1, 0))
    p, h = pr_ref[...], hr_ref[...]
    k = 1
    while k < w:
        keep = col < w - k
        ps = jnp.where(keep, pltpu.roll(p, rows - k, 0), 1.0)
        hs = jnp.where(keep, pltpu.roll(h, rows - k, 0), 0.0)
        p, h = p * ps, p * hs + h
        k *= 2
    fin = fr_ref[...]
    sr_ref[...] = jnp.where(col == w - 1, fin, pltpu.roll(p, rows - 1, 0) * fin + pltpu.roll(h, rows - 1, 0))


def _lru_stitch(pf, hf, pr, hr, ff_rows, fr_rows, w):
    rows, e = pf.shape
    big = pl.BlockSpec((rows, e), lambda i: (0, 0))
    shp = jax.ShapeDtypeStruct((rows, e), F32)
    return pl.pallas_call(
        functools.partial(_lru_stitch_kernel, w=w),
        grid=(1,),
        in_specs=[big] * 6,
        out_specs=[big, big],
        out_shape=[shp, shp],
        compiler_params=_cparams(("arbitrary",)),
        name="lru_stitch",
    )(pf, hf, pr, hr, ff_rows, fr_rows)


def _lru_rev_out_kernel(lar_ref, br_ref, hl_ref, pc_ref, x_ref, m5_ref, gs_ref, sh_ref, gt_ref, hsf_ref, hsr_ref,
                        wgl_ref, wo_ref, fg_ref, o_ref, h_ref, xb_ref, g_ref, mb_ref, mix_ref):
    i = pl.program_id(0)
    nb, nrb, w, d = x_ref.shape
    rows = nb * w
    e = h_ref.shape[-1]

    @pl.when(i == 0)
    def _():
        h_ref[...] = hsr_ref[...]

    for rr in range(nrb):
        for b in range(nb):
            xb_ref[rr * rows + b * w:rr * rows + (b + 1) * w, :] = _norm_mod(
                x_ref[b, rr], gs_ref[b:b + 1], sh_ref[b:b + 1]).astype(BF16)
    g_ref[...] = jnp.dot(xb_ref[...], wgl_ref[...], preferred_element_type=F32)
    for rr in range(nrb - 1, -1, -1):
        for c0 in range(0, e, GATE_TILE):
            cs = slice(c0, c0 + GATE_TILE)
            for r0 in range(0, rows, LRU_RC):
                rs = slice(r0, r0 + LRU_RC)
                gs_rows = slice(rr * rows + r0, rr * rows + r0 + LRU_RC)
                h = jnp.exp2(lar_ref[rr, rs, cs].astype(F32)) * h_ref[rs, cs] + br_ref[rr, rs, cs].astype(F32)
                h_ref[rs, cs] = h
                y = hl_ref[rr, rs, cs].astype(F32) + pc_ref[rr, rs, cs].astype(F32) * hsf_ref[rs, cs] + h
                hg = g_ref[gs_rows, cs]
                mb_ref[gs_rows, cs] = (y * (hg * jnp.tanh(hg) + hg)).astype(BF16)
    m5 = jnp.concatenate([m5_ref[:, rr].reshape(rows, e) for rr in range(nrb)], axis=0)
    mix_ref[...] = (jnp.dot(m5, wo_ref[0:e, :], preferred_element_type=F32)
                    + jnp.dot(mb_ref[...], wo_ref[e:2 * e, :], preferred_element_type=F32))
    fg = fg_ref[...]
    for rr in range(nrb):
        for b in range(nb):
            for r0 in range(0, w, LRU_RC):
                m0 = rr * rows + b * w + r0
                res = x_ref[b, rr, r0:r0 + LRU_RC, :] + gt_ref[b:b + 1] * mix_ref[m0:m0 + LRU_RC, :]
                ms = jnp.mean(res * res, axis=-1, keepdims=True)
                o_ref[b, rr, r0:r0 + LRU_RC, :] = res * lax.rsqrt(ms + EPS) * fg


def _lru_rev_out(lar, br, hloc, pcum, x4, m5, gs8, sh8, gt8, hs_f, hs_r, w_all, wo, fg):
    nb, nrow, w, d = x4.shape
    rows, e = hs_f.shape
    const = lambda a: pl.BlockSpec(a.shape, lambda i, nd=a.ndim: (0,) * nd)
    rb = LRU_RB
    nstep = nrow // rb
    rev3 = pl.BlockSpec((rb, rows, e), lambda i: (nstep - 1 - i, 0, 0))
    rev4 = pl.BlockSpec((nb, rb, w, d), lambda i: (0, nstep - 1 - i, 0, 0))
    rev4e = pl.BlockSpec((nb, rb, w, e), lambda i: (0, nstep - 1 - i, 0, 0))
    return pl.pallas_call(
        _lru_rev_out_kernel,
        grid=(nstep,),
        in_specs=[rev3] * 4 + [rev4, rev4e, const(gs8), const(sh8), const(gt8), const(hs_f), const(hs_r),
                               pl.BlockSpec((d, e), lambda i: (0, W_COL_GLRU)), const(wo), const(fg)],
        out_specs=rev4,
        out_shape=jax.ShapeDtypeStruct((nb, nrow, w, d), F32),
        scratch_shapes=[pltpu.VMEM((rows, e), F32), pltpu.VMEM((rb * rows, d), BF16),
                        pltpu.VMEM((rb * rows, e), F32), pltpu.VMEM((rb * rows, e), BF16),
                        pltpu.VMEM((rb * rows, d), F32)],
        compiler_params=_cparams(("arbitrary",)),
        name="lru_rev_out",
    )(lar, br, hloc, pcum, x4, m5, gs8, sh8, gt8, hs_f, hs_r, w_all, wo, fg)


def _gate_tiles(w_a, w_x):
    ndir, heads, hd, _ = w_a.shape
    per = GATE_TILE // hd
    nq = heads // per

    def tiles(w):
        wq = w.reshape(nq, per, hd, hd)
        eye = jnp.eye(per, dtype=w.dtype)
        blk = wq[:, :, :, None, :] * eye[None, :, None, :, None]
        return blk.reshape(nq, GATE_TILE, GATE_TILE)

    return jnp.concatenate([tiles(w_a[0]), tiles(w_x[0]), tiles(w_a[1]), tiles(w_x[1])], axis=2).astype(BF16)


def kernel(x, c, ctx, c_ctx, w_mod, b_mod, norm_g, w_in, s5_a_re, s5_a_im, s5_log_step, s5_b_re, s5_b_im,
           s5_c_re, s5_c_im, s5_d, s5_w_glu, s5_b_glu, lru_conv_w, lru_conv_b, lru_w_a, lru_b_a, lru_w_x,
           lru_b_x, lru_lam, w_out, final_g):
    bsz, seq, d = x.shape
    nctx_tok = ctx.shape[1]
    e = s5_w_glu.shape[-1]
    assert w_mod.shape[0] == 1, "single-layer block"
    assert bsz <= 8 and seq % S5_SEG == 0 and seq % GRID_W == 0 and nctx_tok % S5_T == 0
    nseg = seq // S5_SEG
    ns = bsz * nseg
    nctx = nctx_tok // S5_T
    assert nctx <= nseg and ns % S5_PB == 0
    nrow = seq // GRID_W
    assert LRU_RB == 2 and nrow % LRU_RB == 0 and nrow >= 4

    c8 = jnp.zeros((8, d), F32).at[:bsz].set(c).at[bsz].set(c_ctx)
    mod = _modulation(c8, w_mod[0], b_mod[0])
    sh, sc, gt = mod[:, :d], mod[:, d:2 * d], mod[:, 2 * d:]
    gs = norm_g[0][None, :] * (1.0 + sc)

    w_in0 = w_in[0]
    assert w_in0.shape == (d, 4 * e)
    wt_us5 = w_in0[:, W_COL_US5 * e:(W_COL_US5 + 1) * e].T.astype(BF16)
    gate_cols = (jnp.arange(4 * e) // e == W_COL_GS5) | (jnp.arange(4 * e) // e == W_COL_GLRU)
    w_all = (w_in0 * jnp.where(gate_cols, 0.5, 1.0)[None, :]).astype(BF16)
    w_glu = (0.5 * s5_w_glu[0]).astype(BF16)
    wo = w_out[0].astype(BF16)

    s5_params = _s5_param_layout(s5_a_re[0], s5_a_im[0], s5_log_step[0], s5_b_re[0], s5_b_im[0],
                                 s5_c_re[0], s5_c_im[0], s5_d[0])
    x3 = x.reshape(ns, S5_SEG, d)
    xt, hg5 = _s5_inproj(x3, gs, sh, wt_us5, w_all, bsz)
    ctx_pad = jnp.zeros((bsz, nseg, S5_T, d), F32).at[:, :nctx].set(ctx.reshape(bsz, nctx, S5_T, d))
    gs_c = jnp.broadcast_to(gs[bsz:bsz + 1], (8, d))
    sh_c = jnp.broadcast_to(sh[bsz:bsz + 1], (8, d))
    xct, _ = _s5_inproj(ctx_pad.reshape(ns, S5_T, d), gs_c, sh_c, wt_us5, w_all, bsz)
    y5 = _s5_core(xt, xct, s5_params, nseg, nctx)
    nj = S5_SEG // S5_T
    m5 = _s5_post(y5.reshape(nj, S5_T, e, ns), hg5, w_glu, 0.5 * s5_b_glu[0].reshape(1, e))

    wg_all = _gate_tiles(0.5 * lru_w_a[0], 0.5 * lru_w_x[0])
    gb_all = 0.5 * jnp.stack([lru_b_a[0, 0], lru_b_x[0, 0], lru_b_a[0, 1], lru_b_x[0, 1]])
    cw = lru_conv_w[0]
    cb = lru_conv_b[0].reshape(1, e)
    lam = lru_lam[0]
    ctx_p = jnp.zeros((8, nctx_tok, d), F32).at[:bsz].set(ctx).transpose(1, 0, 2).reshape(nctx_tok * 8, d)
    ff, fr = _lru_ctx(ctx_p, gs[bsz:bsz + 1], sh[bsz:bsz + 1], w_all, cw, cb, wg_all, gb_all, lam)
    x4 = x.reshape(bsz, nrow, GRID_W, d)
    lar, br, hloc, pcum, hf, pf, pr, hr = _lru_pass_a(x4, gs, sh, w_all, cw, cb, wg_all, gb_all, lam)
    hs_f, hs_r = _lru_stitch(pf, hf, pr, hr, jnp.repeat(ff[:bsz], GRID_W, axis=0),
                             jnp.repeat(fr[:bsz], GRID_W, axis=0), GRID_W)
    out4 = _lru_rev_out(lar, br, hloc, pcum, x4, m5.reshape(bsz, nrow, GRID_W, e), gs, sh, gt, hs_f, hs_r,
                        w_all, wo, final_g.reshape(1, d))
    return out4.reshape(bsz, seq, d)
```

```python
import functools
import math

import jax
import jax.numpy as jnp
from jax import lax
from jax.experimental import pallas as pl
from jax.experimental.pallas import tpu as pltpu

F32 = jnp.float32
BF16 = jnp.bfloat16

EPS = 1e-6
TINY = 1e-30
GRID_W = 64
LRU_C = 8.0
S5_H = 16
S5_N = 64
S5_T = 16
S5_SEG = 256
S5_TB = 8
S5_PB = 16
GATE_TILE = 256
LRU_RB = 2
LRU_WARM = 2
LRU_RC = 32
W_COL_US5, W_COL_GS5, W_COL_ULRU, W_COL_GLRU = 0, 1, 2, 3
V7X_VMEM_BYTES = 64 * 1024 * 1024
VMEM_LIMIT = V7X_VMEM_BYTES * 7 // 8


def _cparams(sem):
    return pltpu.CompilerParams(dimension_semantics=sem, vmem_limit_bytes=VMEM_LIMIT)


def _sigmoid(z):
    return 0.5 * (jnp.tanh(0.5 * z) + 1.0)


def _gelu_tanh(y):
    return 0.5 * y * (1.0 + jnp.tanh(math.sqrt(2.0 / math.pi) * (y + 0.044715 * (y * y * y))))


def _norm_mod(x, gs, sh):
    ms = jnp.mean(x * x, axis=-1, keepdims=True)
    return x * lax.rsqrt(ms + EPS) * gs + sh


def _mod_kernel(c_ref, w_ref, b_ref, o_ref):
    c = c_ref[...]
    s = c * _sigmoid(c)
    o_ref[...] = jnp.dot(s, w_ref[...], preferred_element_type=F32) + b_ref[...]


def _modulation(c8, w_mod, b_mod):
    d = c8.shape[1]
    n = w_mod.shape[1]
    nb = n // d
    return pl.pallas_call(
        _mod_kernel,
        grid=(nb,),
        in_specs=[pl.BlockSpec((8, d), lambda i: (0, 0)),
                  pl.BlockSpec((d, d), lambda i: (0, i)),
                  pl.BlockSpec((1, d), lambda i: (0, i))],
        out_specs=pl.BlockSpec((8, d), lambda i: (0, i)),
        out_shape=jax.ShapeDtypeStruct((8, n), F32),
        compiler_params=_cparams(("arbitrary",)),
    )(c8, w_mod, b_mod.reshape(1, n))


def _perm_matrix(n_a, n_b):
    n = n_a * n_b
    r = lax.broadcasted_iota(jnp.int32, (n, n), 0)
    c = lax.broadcasted_iota(jnp.int32, (n, n), 1)
    return ((r // n_b == c % n_a) & (r % n_b == c // n_a)).astype(BF16)


def _s5_inproj_kernel(x_ref, gs_ref, sh_ref, wt_ref, wg_ref, o_ref, hg_ref, xn_ref, xs_ref, *, nbatch):
    i = pl.program_id(0)
    ns, tb, d = x_ref.shape
    groups = o_ref.shape[0]
    spb = ns // nbatch
    blk = S5_PB * tb
    perm = _perm_matrix(tb, S5_PB)
    xn3 = jnp.concatenate(
        [_norm_mod(x_ref[b * spb:(b + 1) * spb], gs_ref[b:b + 1], sh_ref[b:b + 1]) for b in range(nbatch)], axis=0)
    xn3b = xn3.astype(BF16)
    for half in range(S5_T // tb):
        @pl.when(i % (S5_T // tb) == half)
        def _(half=half):
            xn_ref[:, half * tb:(half + 1) * tb, :] = xn3b
    xn = xn3.reshape(ns * tb, d).astype(BF16)
    for k in range(ns // S5_PB):
        q = jnp.dot(perm, xn[k * blk:(k + 1) * blk], preferred_element_type=F32).astype(BF16)
        for t in range(tb):
            xs_ref[t * ns + k * S5_PB:t * ns + (k + 1) * S5_PB, :] = q[t * S5_PB:(t + 1) * S5_PB]
    wt = wt_ref[...]
    for t in range(0, tb, 2):
        ut = lax.dot_general(wt, xs_ref[t * ns:(t + 2) * ns, :], (((1,), (1,)), ((), ())),
                             preferred_element_type=F32)
        o_ref[:, t * S5_H:(t + 1) * S5_H, :] = ut[:, :ns].reshape(groups, S5_H, ns).astype(BF16)
        o_ref[:, (t + 1) * S5_H:(t + 2) * S5_H, :] = ut[:, ns:].reshape(groups, S5_H, ns).astype(BF16)
    hg = jnp.dot(xs_ref[...], wg_ref[...], preferred_element_type=F32)
    hg_ref[...] = hg.astype(BF16).reshape(tb, ns, hg.shape[1])


def _s5_inproj(x3, gs8, sh8, wt, w_all, nbatch):
    ns, seg, d = x3.shape
    e = wt.shape[0]
    groups = e // S5_H
    nj = seg // S5_T
    per = S5_T // S5_TB
    return pl.pallas_call(
        functools.partial(_s5_inproj_kernel, nbatch=nbatch),
        grid=(seg // S5_TB,),
        in_specs=[pl.BlockSpec((ns, S5_TB, d), lambda i: (0, i, 0)),
                  pl.BlockSpec((8, d), lambda i: (0, 0)),
                  pl.BlockSpec((8, d), lambda i: (0, 0)),
                  pl.BlockSpec((e, d), lambda i: (0, 0)),
                  pl.BlockSpec((d, e), lambda i: (0, W_COL_GS5))],
        out_specs=[pl.BlockSpec((groups, S5_TB * S5_H, ns), lambda i: (0, i % per, i // per)),
                   pl.BlockSpec((S5_TB, ns, e), lambda i: (i, 0, 0)),
                   pl.BlockSpec((ns, S5_T, d), lambda i: (0, i // per, 0))],
        out_shape=[jax.ShapeDtypeStruct((groups, S5_T * S5_H, nj * ns), BF16),
                   jax.ShapeDtypeStruct((seg, ns, e), BF16),
                   jax.ShapeDtypeStruct((ns, seg, d), BF16)],
        scratch_shapes=[pltpu.VMEM((S5_TB * ns, d), BF16)],
        compiler_params=_cparams(("arbitrary",)),
        name="s5_inproj",
    )(x3, gs8, sh8, wt, w_all)


def _cmul_add(dr, di, hr, hi, sr, si):
    return dr * hr - di * hi + sr, dr * hi + di * hr + si


def _lane_scan(vr, vi, dr, di, slot, count, reverse):
    n = vr.shape[1]
    k = 1
    while k < count:
        if reverse:
            sr = pltpu.roll(vr, n - k, 1)
            si = pltpu.roll(vi, n - k, 1)
            keep = slot < count - k
        else:
            sr = pltpu.roll(vr, k, 1)
            si = pltpu.roll(vi, k, 1)
            keep = slot >= k
        sr = jnp.where(keep, sr, 0.0)
        si = jnp.where(keep, si, 0.0)
        vr, vi = _cmul_add(dr, di, sr, si, vr, vi)
        dr, di = dr * dr - di * di, 2.0 * dr * di
        k *= 2
    return vr, vi


def _csq(r, i):
    return r * r - i * i, 2.0 * r * i


def _cpow_bits(e, pows):
    pr = jnp.ones(e.shape, F32)
    pi = jnp.zeros(e.shape, F32)
    for bit, (qr, qi) in pows:
        on = (e & bit) != 0
        pr, pi = jnp.where(on, pr * qr - pi * qi, pr), jnp.where(on, pr * qi + pi * qr, pi)
    return pr, pi


def _s5_discretize(ar, ai, ls):
    step = jnp.exp(ls)
    mag = jnp.exp(ar * step)
    pr = mag * jnp.cos(ai * step)
    pi = mag * jnp.sin(ai * step)
    den = ar * ar + ai * ai
    nr = pr - 1.0
    return pr, pi, (nr * ar + pi * ai) / den, (pi * ar - nr * ai) / den


def _s5_operators(pcol_ref, prow_ref, bt_ref, ct1_ref, ct2_ref, dsk_ref):
    n, t_len, th = S5_N, S5_T, S5_T * S5_H
    hp = lax.Precision.HIGHEST
    lane_t = lax.broadcasted_iota(jnp.int32, (1, th), 1) // S5_H
    bits = [1 << b for b in range(t_len.bit_length() - 1)]

    tile_h = (lax.broadcasted_iota(jnp.int32, (S5_H, th), 1) % S5_H
              == lax.broadcasted_iota(jnp.int32, (S5_H, th), 0)).astype(F32)
    bt = jnp.dot(bt_ref[...], tile_h, precision=hp, preferred_element_type=F32)

    blocks, bbs, decay = [], [], []
    for d in range(2):
        pr, pi, cr, ci = _s5_discretize(pcol_ref[3 * d], pcol_ref[3 * d + 1], pcol_ref[3 * d + 2])
        pows, q = [], (pr, pi)
        for bit in bits:
            pows.append((bit, q))
            q = _csq(*q)
        decay.append(q)
        btr = bt[2 * d * n:(2 * d + 1) * n, :]
        bti = bt[(2 * d + 1) * n:(2 * d + 2) * n, :]
        bbr = cr * btr - ci * bti
        bbi = cr * bti + ci * btr
        e = (t_len - 1 - lane_t) if d == 0 else lane_t
        wr, wi = _cpow_bits(jnp.broadcast_to(e, (n, th)), pows)
        blocks += [wr * bbr - wi * bbi, wr * bbi + wi * bbr]
        bbs.append(jnp.concatenate([bbr, bbi], axis=0))
    bpow = jnp.concatenate(blocks, axis=0).astype(BF16)

    qr, qi, _, _ = _s5_discretize(prow_ref[0:1], prow_ref[1:2], prow_ref[2:3])
    pows, q = [], (qr, qi)
    for bit in bits:
        pows.append((bit, q))
        q = _csq(*q)
    q_t = q
    row_t = lax.broadcasted_iota(jnp.int32, (th, 1), 0) // S5_H
    is_f = lax.broadcasted_iota(jnp.int32, (1, 4 * n), 1) < 2 * n
    e_k = jnp.where(is_f, row_t, (t_len - row_t) % t_len)
    kr, ki = _cpow_bits(e_k, pows)
    nr = jnp.where(is_f, kr * qr - ki * qi, jnp.where(row_t == 0, q_t[0], kr))
    ni = jnp.where(is_f, kr * qi + ki * qr, jnp.where(row_t == 0, q_t[1], ki))
    ct1 = jnp.concatenate([ct1_ref[...]] * t_len, axis=0)
    ct2 = jnp.concatenate([ct2_ref[...]] * t_len, axis=0)
    cpow = (ct1 * nr + ct2 * ni).astype(BF16)
    qk = ct1 * kr + ct2 * ki
    mf = jnp.dot(qk[:, :2 * n], bbs[0], precision=hp, preferred_element_type=F32)
    mr = jnp.dot(qk[:, 2 * n:], bbs[1], precision=hp, preferred_element_type=F32)
    m = (jnp.where(row_t <= t_len - 1 - lane_t, mf, 0.0)
         + jnp.where((row_t == 0) | (row_t >= t_len - lane_t), mr, 0.0))
    for bit in bits:
        m = jnp.where((lane_t & bit) != 0, pltpu.roll(m, S5_H * bit, 0), m)
    row = lax.broadcasted_iota(jnp.int32, (th, th), 0)
    col = lax.broadcasted_iota(jnp.int32, (th, th), 1)
    toep = (m + jnp.where(row == col, dsk_ref[...], 0.0)).astype(BF16)
    return toep, bpow, cpow, decay


def _s5_core_kernel(xt_ref, xc_ref, pcol_ref, pp_ref, bt_ref, o_ref, s_ref, *, nseg, nctx):
    n = S5_N
    h = S5_H
    ns = xc_ref.shape[-1]
    nj = xt_ref.shape[-1] // ns
    toep, bpow, cpow, decay = _s5_operators(pcol_ref, pp_ref.at[2 * h:2 * h + 3], bt_ref, pp_ref.at[0:h],
                                            pp_ref.at[h:2 * h], pp_ref.at[2 * h + 3:2 * h + 4])
    dfr = jnp.broadcast_to(decay[0][0], (n, ns))
    dfi = jnp.broadcast_to(decay[0][1], (n, ns))
    drr = jnp.broadcast_to(decay[1][0], (n, ns))
    dri = jnp.broadcast_to(decay[1][1], (n, ns))
    slot = lax.broadcasted_iota(jnp.int32, (n, ns), 1) % nseg

    sc = jnp.dot(bpow, xc_ref[...], preferred_element_type=F32)
    valid = slot < nctx
    cfr, cfi = _lane_scan(jnp.where(valid, sc[0:n], 0.0), jnp.where(valid, sc[n:2 * n], 0.0),
                          dfr, dfi, slot, nseg, False)
    crr, cri = _lane_scan(jnp.where(valid, sc[2 * n:3 * n], 0.0), jnp.where(valid, sc[3 * n:4 * n], 0.0),
                          drr, dri, slot, nseg, True)

    step = 2 if nj % 2 == 0 else 1
    for j in range(0, nj, step):
        cols = slice(j * ns, (j + step) * ns)
        s_ref[:, cols] = jnp.dot(bpow, xt_ref[:, cols], preferred_element_type=F32)

    zero = jnp.zeros((n, ns), F32)
    efr, efi, err, eri = zero, zero, zero, zero
    for j in range(nj):
        cols = slice(j * ns, (j + 1) * ns)
        efr, efi = _cmul_add(dfr, dfi, efr, efi, s_ref[0:n, cols], s_ref[n:2 * n, cols])
    for j in range(nj - 1, -1, -1):
        cols = slice(j * ns, (j + 1) * ns)
        err, eri = _cmul_add(drr, dri, err, eri, s_ref[2 * n:3 * n, cols], s_ref[3 * n:4 * n, cols])

    sfr, sfi, srr, sri = dfr, dfi, drr, dri
    k = 1
    while k < nj:
        sfr, sfi = sfr * sfr - sfi * sfi, 2.0 * sfr * sfi
        srr, sri = srr * srr - sri * sri, 2.0 * srr * sri
        k *= 2

    first = slot == 0
    last = slot == nseg - 1
    vfr = jnp.where(first, pltpu.roll(cfr, ns - (nctx - 1), 1) if nctx > 1 else cfr, pltpu.roll(efr, 1, 1))
    vfi = jnp.where(first, pltpu.roll(cfi, ns - (nctx - 1), 1) if nctx > 1 else cfi, pltpu.roll(efi, 1, 1))
    vrr = jnp.where(last, pltpu.roll(crr, nseg - 1, 1) if nseg > 1 else crr, pltpu.roll(err, ns - 1, 1))
    vri = jnp.where(last, pltpu.roll(cri, nseg - 1, 1) if nseg > 1 else cri, pltpu.roll(eri, ns - 1, 1))
    hfr, hfi = _lane_scan(vfr, vfi, sfr, sfi, slot, nseg, False)
    hrr, hri = _lane_scan(vrr, vri, srr, sri, slot, nseg, True)

    for j in range(nj):
        cols = slice(j * ns, (j + 1) * ns)
        nr, ni = _cmul_add(dfr, dfi, hfr, hfi, s_ref[0:n, cols], s_ref[n:2 * n, cols])
        s_ref[0:n, cols] = hfr
        s_ref[n:2 * n, cols] = hfi
        hfr, hfi = nr, ni
    for j in range(nj - 1, -1, -1):
        cols = slice(j * ns, (j + 1) * ns)
        nr, ni = _cmul_add(drr, dri, hrr, hri, s_ref[2 * n:3 * n, cols], s_ref[3 * n:4 * n, cols])
        s_ref[2 * n:3 * n, cols] = hrr
        s_ref[3 * n:4 * n, cols] = hri
        hrr, hri = nr, ni

    for j in range(0, nj, step):
        cols = slice(j * ns, (j + step) * ns)
        y = (jnp.dot(toep, xt_ref[:, cols], preferred_element_type=F32)
             + jnp.dot(cpow, s_ref[:, cols].astype(BF16), preferred_element_type=F32))
        for q in range(step):
            o_ref[j + q] = y[:, q * ns:(q + 1) * ns].reshape(S5_T, S5_H, ns).astype(BF16)


def _s5_core(xt, xct, params, nseg, nctx):
    groups, rows, lanes = xt.shape
    ns = xct.shape[-1]
    nj = lanes // ns
    kern = functools.partial(_s5_core_kernel, nseg=nseg, nctx=nctx)
    per_group = lambda a: pl.BlockSpec((None,) + a.shape[1:], lambda g, nd=a.ndim: (g,) + (0,) * (nd - 1))
    return pl.pallas_call(
        kern,
        grid=(groups,),
        in_specs=[per_group(xt), per_group(xct)] + [per_group(a) for a in params],
        out_specs=pl.BlockSpec((nj, S5_T, None, S5_H, ns), lambda g: (0, 0, g, 0, 0)),
        out_shape=jax.ShapeDtypeStruct((nj, S5_T, groups, S5_H, ns), BF16),
        scratch_shapes=[pltpu.VMEM((rows, lanes), F32)],
        compiler_params=_cparams(("arbitrary",)),
        name="s5_core",
    )(xt, xct, *params)


def _s5_param_layout(a_re, a_im, log_step, b_re, b_im, c_re, c_im, d_skip):
    ndir, groups, n = a_re.shape
    h = b_re.shape[-1]
    ls = jnp.broadcast_to(log_step[..., None], a_re.shape)
    pcol = jnp.stack([a_re[0], a_im[0], ls[0], a_re[1], a_im[1], ls[1]], axis=1)[..., None]
    lay = lambda v: jnp.concatenate([v[0], v[0], v[1], v[1]], axis=-1)
    prow = jnp.stack([lay(a_re), lay(a_im), lay(ls)], axis=1)
    bt = jnp.concatenate([b_re[0], b_im[0], b_re[1], b_im[1]], axis=1)
    ct1 = jnp.concatenate([c_re[0], -c_im[0], c_re[1], -c_im[1]], axis=-1)
    ct2 = jnp.concatenate([-c_im[0], -c_re[0], -c_im[1], -c_re[1]], axis=-1)
    dsk = jnp.tile(d_skip.reshape(groups, 1, h), (1, 1, S5_T))
    pad = jnp.zeros((groups, (-(2 * h + 4)) % 8, S5_T * h), F32)
    pp = jnp.concatenate([ct1, ct2, prow, dsk, pad], axis=1)
    return tuple(v.astype(F32) for v in (pcol, pp, bt))


def _s5_post_kernel(y_ref, hg_ref, wglu_ref, bglu_ref, o_ref, m_ref):
    q = pl.program_id(1)
    tb, e, ns = y_ref.shape
    ys = jnp.concatenate([y_ref[t].astype(F32).T for t in range(tb)], axis=0)
    yg = _gelu_tanh(ys)
    zh = jnp.dot(yg.astype(BF16), wglu_ref[...], preferred_element_type=F32) + bglu_ref[...]
    hg = hg_ref[...].reshape(tb * ns, e).astype(F32)
    m = ((yg * (0.5 * jnp.tanh(zh) + 0.5)) * (hg * jnp.tanh(hg) + hg)).astype(BF16)
    nq = S5_T // tb
    for half in range(nq):
        @pl.when(q == half)
        def _(half=half):
            m_ref[half * tb * ns:(half + 1) * tb * ns, :] = m

    @pl.when(q == nq - 1)
    def _():
        perm = _perm_matrix(S5_PB, S5_T)
        for k in range(ns // S5_PB):
            g = jnp.concatenate([m_ref[t * ns + k * S5_PB:t * ns + (k + 1) * S5_PB, :] for t in range(S5_T)], axis=0)
            r = jnp.dot(perm, g, preferred_element_type=F32).astype(BF16)
            o_ref[k * S5_PB:(k + 1) * S5_PB] = r.reshape(S5_PB, S5_T, e)


def _s5_post(y4, hg, wglu, bglu):
    nj, t_all, e, ns = y4.shape
    nt = t_all // S5_TB
    return pl.pallas_call(
        _s5_post_kernel,
        grid=(nj, nt),
        in_specs=[pl.BlockSpec((None, S5_TB, e, ns), lambda j, q: (j, q, 0, 0)),
                  pl.BlockSpec((S5_TB, ns, e), lambda j, q: (j * nt + q, 0, 0)),
                  pl.BlockSpec((e, e), lambda j, q: (0, 0)),
                  pl.BlockSpec((1, e), lambda j, q: (0, 0))],
        out_specs=pl.BlockSpec((ns, S5_T, e), lambda j, q: (0, j, 0)),
        out_shape=jax.ShapeDtypeStruct((ns, nj * S5_T, e), BF16),
        scratch_shapes=[pltpu.VMEM((S5_T * ns, e), BF16)],
        compiler_params=_cparams(("arbitrary", "arbitrary")),
        name="s5_post",
    )(y4, hg, wglu, bglu)


def _softplus(z):
    return jnp.maximum(z, 0.0) + jnp.log1p(jnp.exp(-jnp.abs(z)))


def _gate_matmul(xc, wg_ref):
    xb = xc.astype(BF16)
    nq = wg_ref.shape[0]
    return [jnp.dot(xb[:, q * GATE_TILE:(q + 1) * GATE_TILE], wg_ref[q], preferred_element_type=F32)
            for q in range(nq)]


def _gate_cols(zs, k):
    return jnp.concatenate([z[:, k * GATE_TILE:(k + 1) * GATE_TILE] for z in zs], axis=1)


def _lru_decay_scale(lam_row):
    return (-0.5 * LRU_C * math.log2(math.e)) * _softplus(-lam_row)


def _lru_gate_math(hx, za, zx, ba, bx, c2):
    ta = jnp.tanh(za + ba)
    tx = jnp.tanh(zx + bx)
    la = ta * c2 + c2
    a = jnp.exp2(la)
    om = 1.0 - a * a
    return la, a, (om * lax.rsqrt(jnp.maximum(om, TINY))) * (tx * hx + hx)


def _lru_coeffs(hx, zs, k, gbh, c2):
    return _lru_gate_math(hx, _gate_cols(zs, 2 * k), _gate_cols(zs, 2 * k + 1),
                          gbh[2 * k:2 * k + 1], gbh[2 * k + 1:2 * k + 2], c2)[1:]


def _lru_ctx_kernel(x_ref, gs_ref, sh_ref, wu_ref, cw_ref, cb_ref, wg_ref, gb_ref, lam_ref,
                    ff_ref, fr_ref, af_s, bf_s, ar_s, br_s):
    rows = x_ref.shape[0]
    npos = rows // 8
    xn = _norm_mod(x_ref[...], gs_ref[...], sh_ref[...]).astype(BF16)
    u = jnp.dot(xn, wu_ref[...], preferred_element_type=F32)
    pos = lax.broadcasted_iota(jnp.int32, u.shape, 0) // 8
    um1 = jnp.where(pos >= 1, pltpu.roll(u, 8, 0), 0.0)
    up1 = jnp.where(pos < npos - 1, pltpu.roll(u, rows - 8, 0), 0.0)
    up2 = jnp.where(pos < npos - 2, pltpu.roll(u, rows - 16, 0), 0.0)
    cw = cw_ref[...]
    xc = cw[0:1] * um1 + cw[1:2] * u + cw[2:3] * up1 + cw[3:4] * up2 + cb_ref[...]
    zs = _gate_matmul(xc, wg_ref)
    gbh = gb_ref[...]
    c2 = _lru_decay_scale(lam_ref[...])
    hx = 0.5 * xc
    a, b = _lru_coeffs(hx, zs, 0, gbh, c2[0:1])
    af_s[...] = a
    bf_s[...] = b
    a, b = _lru_coeffs(hx, zs, 1, gbh, c2[1:2])
    ar_s[...] = a
    br_s[...] = b

    def body(p, carry):
        hf, hr, pr = carry
        r0 = pl.multiple_of(p * 8, 8)
        hf = af_s[pl.ds(r0, 8), :] * hf + bf_s[pl.ds(r0, 8), :]
        hr = hr + pr * br_s[pl.ds(r0, 8), :]
        pr = pr * ar_s[pl.ds(r0, 8), :]
        return hf, hr, pr

    zero = jnp.zeros((8, u.shape[1]), F32)
    hf, hr, _ = lax.fori_loop(0, npos, body, (zero, zero, zero + 1.0))
    ff_ref[...] = hf
    fr_ref[...] = hr


def _lru_ctx(xp, gs, sh, w_all, cw, cb, wg, gb, lam):
    rows, d = xp.shape
    e = cw.shape[1]
    nq = e // GATE_TILE
    full = lambda a: pl.BlockSpec(a.shape, lambda q, nd=a.ndim: (0,) * nd)
    cols = lambda a: pl.BlockSpec((a.shape[0], GATE_TILE), lambda q: (0, q))
    wu_cols = pl.BlockSpec((d, GATE_TILE), lambda q: (0, W_COL_ULRU * nq + q))
    return pl.pallas_call(
        _lru_ctx_kernel,
        grid=(nq,),
        in_specs=[full(xp), full(gs), full(sh), wu_cols, cols(cw), cols(cb),
                  pl.BlockSpec((1,) + wg.shape[1:], lambda q: (q, 0, 0)), cols(gb), cols(lam)],
        out_specs=[pl.BlockSpec((8, GATE_TILE), lambda q: (0, q)), pl.BlockSpec((8, GATE_TILE), lambda q: (0, q))],
        out_shape=[jax.ShapeDtypeStruct((8, e), F32), jax.ShapeDtypeStruct((8, e), F32)],
        scratch_shapes=[pltpu.VMEM((rows, GATE_TILE), F32) for _ in range(4)],
        compiler_params=_cparams(("arbitrary",)),
    )(xp, gs, sh, w_all, cw, cb, wg, gb, lam)


def _lru_pass_a_kernel(xa_ref, xb_ref, wu_ref, cw_ref, cb_ref, wg_ref, gb_ref, lam_ref,
                       lar_ref, br_ref, hl_ref, pc_ref, hs_ref, ps_ref, pr_ref, hr_ref,
                       ring_ref, xc_ref, xcb_ref, z_ref, *, nrow):
    i = pl.program_id(0)
    nb, w, d = xa_ref.shape
    rows = nb * w
    nq = wg_ref.shape[0]
    gt = GATE_TILE

    @pl.when(i == 0)
    def _():
        ring_ref[...] = jnp.zeros_like(ring_ref)

    @pl.when(i <= LRU_WARM)
    def _():
        hs_ref[...] = jnp.zeros_like(hs_ref)
        ps_ref[...] = jnp.ones_like(ps_ref)
        hr_ref[...] = jnp.zeros_like(hr_ref)
        pr_ref[...] = jnp.ones_like(pr_ref)

    r = lax.broadcasted_iota(jnp.int32, (rows, rows), 0)
    c = lax.broadcasted_iota(jnp.int32, (rows, rows), 1)
    same_batch = r // w == c // w
    shifted = []
    for k, x_ref in enumerate((xa_ref, xb_ref)):
        s = 2 * i + k
        off = jnp.where(s == 0, -1, jnp.where(s > nrow, 1, 0))
        mix = ((c == r + off) & same_batch).astype(BF16)
        shifted.append(jnp.dot(mix, x_ref[...].reshape(rows, d), preferred_element_type=F32).astype(BF16))
    u = jnp.dot(jnp.concatenate(shifted, axis=0), wu_ref[...], preferred_element_type=F32)
    ring_ref[(2 * i) & 7] = u[0:rows]
    ring_ref[(2 * i + 1) & 7] = u[rows:2 * rows]

    cw = cw_ref[...]
    cb = cb_ref[...]
    for k in range(LRU_RB):
        sl = [(2 * i - 2 * LRU_WARM + k + t) & 7 for t in range(4)]
        for r0 in range(0, rows, LRU_RC):
            rs = slice(r0, r0 + LRU_RC)
            ks = slice(k * rows + r0, k * rows + r0 + LRU_RC)
            xc = (cw[0:1] * ring_ref[sl[0], rs, :] + cw[1:2] * ring_ref[sl[1], rs, :]
                  + cw[2:3] * ring_ref[sl[2], rs, :] + cw[3:4] * ring_ref[sl[3], rs, :] + cb)
            xc_ref[ks, :] = xc
            xcb_ref[ks, :] = xc.astype(BF16)
    for q in range(nq):
        z_ref[q] = jnp.dot(xcb_ref[:, q * gt:(q + 1) * gt], wg_ref[q], preferred_element_type=F32)
    gbh = gb_ref[...]
    c2 = _lru_decay_scale(lam_ref[...])
    for k in range(LRU_RB):
        for q in range(nq):
            cs = slice(q * gt, (q + 1) * gt)
            for r0 in range(0, rows, LRU_RC):
                rs = slice(r0, r0 + LRU_RC)
                ks = slice(k * rows + r0, k * rows + r0 + LRU_RC)
                hx = 0.5 * xc_ref[ks, cs]
                _, af, bf = _lru_gate_math(hx, z_ref[q, ks, 0:gt], z_ref[q, ks, gt:2 * gt],
                                           gbh[0:1, cs], gbh[1:2, cs], c2[0:1, cs])
                lar, ar, br = _lru_gate_math(hx, z_ref[q, ks, 2 * gt:3 * gt], z_ref[q, ks, 3 * gt:4 * gt],
                                             gbh[2:3, cs], gbh[3:4, cs], c2[1:2, cs])
                h = af * hs_ref[rs, cs] + bf
                p = ps_ref[rs, cs] * af
                hs_ref[rs, cs] = h
                ps_ref[rs, cs] = p
                hl_ref[k, rs, cs] = h.astype(BF16)
                pc_ref[k, rs, cs] = p.astype(BF16)
                lar_ref[k, rs, cs] = lar.astype(BF16)
                br_ref[k, rs, cs] = br.astype(BF16)
                prod = pr_ref[rs, cs]
                hr_ref[rs, cs] = hr_ref[rs, cs] + prod * br
                pr_ref[rs, cs] = prod * ar


def _lru_pass_a(xn4, w_all, cw, cb, wg, gb, lam):
    nb, nrow, w, d = xn4.shape
    e = cw.shape[1]
    rows = nb * w
    rb = LRU_RB
    kern = functools.partial(_lru_pass_a_kernel, nrow=nrow)
    const = lambda a: pl.BlockSpec(a.shape, lambda i, nd=a.ndim: (0,) * nd)
    acc = pl.BlockSpec((rows, e), lambda i: (0, 0))
    acc_shape = jax.ShapeDtypeStruct((rows, e), F32)
    per_row = pl.BlockSpec((rb, rows, e), lambda i: (jnp.maximum(i - LRU_WARM, 0), 0, 0))
    per_row_shape = jax.ShapeDtypeStruct((nrow, rows, e), BF16)
    x_row = lambda first: pl.BlockSpec((nb, None, w, d), lambda i: (0, (rb * i + first) % nrow, 0, 0))
    return pl.pallas_call(
        kern,
        grid=(nrow // rb + LRU_WARM,),
        in_specs=[x_row(nrow - 1), x_row(0),
                  pl.BlockSpec((d, e), lambda i: (0, W_COL_ULRU)), const(cw), const(cb),
                  const(wg), const(gb), const(lam)],
        out_specs=[per_row] * 4 + [acc] * 4,
        out_shape=[per_row_shape] * 4 + [acc_shape] * 4,
        scratch_shapes=[pltpu.VMEM((8, rows, e), F32),
                        pltpu.VMEM((rb * rows, e), F32), pltpu.VMEM((rb * rows, e), BF16),
                        pltpu.VMEM((wg.shape[0], rb * rows, wg.shape[2]), F32)],
        compiler_params=_cparams(("arbitrary",)),
        name="lru_pass_a",
    )(xn4, xn4, w_all, cw, cb, wg, gb, lam)


def _lru_stitch_kernel(pf_ref, hf_ref, pr_ref, hr_ref, ff_ref, fr_ref, sf_ref, sr_ref, *, w):
    rows = pf_ref.shape[0]
    col = lax.broadcasted_iota(jnp.int32, pf_ref.shape, 0) % w
    p, h = pf_ref[...], hf_ref[...]
    k = 1
    while k < w:
        keep = col >= k
        ps = jnp.where(keep, pltpu.roll(p, k, 0), 1.0)
        hs = jnp.where(keep, pltpu.roll(h, k, 0), 0.0)
        p, h = p * ps, p * hs + h
        k *= 2
    fin = ff_ref[...]
    sf_ref[...] = jnp.where(col == 0, fin, pltpu.roll(p, 1, 0) * fin + pltpu.roll(h, 1, 0))
    p, h = pr_ref[...], hr_ref[...]
    k = 1
    while k < w:
        keep = col < w - k
        ps = jnp.where(keep, pltpu.roll(p, rows - k, 0), 1.0)
        hs = jnp.where(keep, pltpu.roll(h, rows - k, 0), 0.0)
        p, h = p * ps, p * hs + h
        k *= 2
    fin = fr_ref[...]
    sr_ref[...] = jnp.where(col == w - 1, fin, pltpu.roll(p, rows - 1, 0) * fin + pltpu.roll(h, rows - 1, 0))


def _lru_stitch(pf, hf, pr, hr, ff_rows, fr_rows, w):
    rows, e = pf.shape
    big = pl.BlockSpec((rows, e), lambda i: (0, 0))
    shp = jax.ShapeDtypeStruct((rows, e), F32)
    return pl.pallas_call(
        functools.partial(_lru_stitch_kernel, w=w),
        grid=(1,),
        in_specs=[big] * 6,
        out_specs=[big, big],
        out_shape=[shp, shp],
        compiler_params=_cparams(("arbitrary",)),
        name="lru_stitch",
    )(pf, hf, pr, hr, ff_rows, fr_rows)


def _lru_rev_out_kernel(lar_ref, br_ref, hl_ref, pc_ref, x_ref, m5_ref, gs_ref, sh_ref, gt_ref, hsf_ref, hsr_ref,
                        wgl_ref, wo_ref, fg_ref, o_ref, h_ref, xb_ref, g_ref, mb_ref, mix_ref):
    i = pl.program_id(0)
    nb, nrb, w, d = x_ref.shape
    rows = nb * w
    e = h_ref.shape[-1]

    @pl.when(i == 0)
    def _():
        h_ref[...] = hsr_ref[...]

    for rr in range(nrb):
        for b in range(nb):
            xb_ref[rr * rows + b * w:rr * rows + (b + 1) * w, :] = _norm_mod(
                x_ref[b, rr], gs_ref[b:b + 1], sh_ref[b:b + 1]).astype(BF16)
    g_ref[...] = jnp.dot(xb_ref[...], wgl_ref[...], preferred_element_type=F32)
    for rr in range(nrb - 1, -1, -1):
        for c0 in range(0, e, GATE_TILE):
            cs = slice(c0, c0 + GATE_TILE)
            for r0 in range(0, rows, LRU_RC):
                rs = slice(r0, r0 + LRU_RC)
                gs_rows = slice(rr * rows + r0, rr * rows + r0 + LRU_RC)
                h = jnp.exp2(lar_ref[rr, rs, cs].astype(F32)) * h_ref[rs, cs] + br_ref[rr, rs, cs].astype(F32)
                h_ref[rs, cs] = h
                y = hl_ref[rr, rs, cs].astype(F32) + pc_ref[rr, rs, cs].astype(F32) * hsf_ref[rs, cs] + h
                hg = g_ref[gs_rows, cs]
                mb_ref[gs_rows, cs] = (y * (hg * jnp.tanh(hg) + hg)).astype(BF16)
    m5 = jnp.concatenate([m5_ref[:, rr].reshape(rows, e) for rr in range(nrb)], axis=0)
    mix_ref[...] = (jnp.dot(m5, wo_ref[0:e, :], preferred_element_type=F32)
                    + jnp.dot(mb_ref[...], wo_ref[e:2 * e, :], preferred_element_type=F32))
    fg = fg_ref[...]
    for rr in range(nrb):
        for b in range(nb):
            for r0 in range(0, w, LRU_RC):
                m0 = rr * rows + b * w + r0
                res = x_ref[b, rr, r0:r0 + LRU_RC, :] + gt_ref[b:b + 1] * mix_ref[m0:m0 + LRU_RC, :]
                ms = jnp.mean(res * res, axis=-1, keepdims=True)
                o_ref[b, rr, r0:r0 + LRU_RC, :] = res * lax.rsqrt(ms + EPS) * fg


def _lru_rev_out(lar, br, hloc, pcum, x4, m5, gs8, sh8, gt8, hs_f, hs_r, w_all, wo, fg):
    nb, nrow, w, d = x4.shape
    rows, e = hs_f.shape
    const = lambda a: pl.BlockSpec(a.shape, lambda i, nd=a.ndim: (0,) * nd)
    rb = LRU_RB
    nstep = nrow // rb
    rev3 = pl.BlockSpec((rb, rows, e), lambda i: (nstep - 1 - i, 0, 0))
    rev4 = pl.BlockSpec((nb, rb, w, d), lambda i: (0, nstep - 1 - i, 0, 0))
    rev4e = pl.BlockSpec((nb, rb, w, e), lambda i: (0, nstep - 1 - i, 0, 0))
    return pl.pallas_call(
        _lru_rev_out_kernel,
        grid=(nstep,),
        in_specs=[rev3] * 4 + [rev4, rev4e, const(gs8), const(sh8), const(gt8), const(hs_f), const(hs_r),
                               pl.BlockSpec((d, e), lambda i: (0, W_COL_GLRU)), const(wo), const(fg)],
        out_specs=rev4,
        out_shape=jax.ShapeDtypeStruct((nb, nrow, w, d), F32),
        scratch_shapes=[pltpu.VMEM((rows, e), F32), pltpu.VMEM((rb * rows, d), BF16),
                        pltpu.VMEM((rb * rows, e), F32), pltpu.VMEM((rb * rows, e), BF16),
                        pltpu.VMEM((rb * rows, d), F32)],
        compiler_params=_cparams(("arbitrary",)),
        name="lru_rev_out",
    )(lar, br, hloc, pcum, x4, m5, gs8, sh8, gt8, hs_f, hs_r, w_all, wo, fg)


def _gate_tiles(w_a, w_x):
    ndir, heads, hd, _ = w_a.shape
    per = GATE_TILE // hd
    nq = heads // per

    def tiles(w):
        wq = w.reshape(nq, per, hd, hd)
        eye = jnp.eye(per, dtype=w.dtype)
        blk = wq[:, :, :, None, :] * eye[None, :, None, :, None]
        return blk.reshape(nq, GATE_TILE, GATE_TILE)

    return jnp.concatenate([tiles(w_a[0]), tiles(w_x[0]), tiles(w_a[1]), tiles(w_x[1])], axis=2).astype(BF16)


def kernel(x, c, ctx, c_ctx, w_mod, b_mod, norm_g, w_in, s5_a_re, s5_a_im, s5_log_step, s5_b_re, s5_b_im,
           s5_c_re, s5_c_im, s5_d, s5_w_glu, s5_b_glu, lru_conv_w, lru_conv_b, lru_w_a, lru_b_a, lru_w_x,
           lru_b_x, lru_lam, w_out, final_g):
    bsz, seq, d = x.shape
    nctx_tok = ctx.shape[1]
    e = s5_w_glu.shape[-1]
    assert w_mod.shape[0] == 1, "single-layer block"
    assert bsz <= 8 and seq % S5_SEG == 0 and seq % GRID_W == 0 and nctx_tok % S5_T == 0
    nseg = seq // S5_SEG
    ns = bsz * nseg
    nctx = nctx_tok // S5_T
    assert nctx <= nseg and ns % S5_PB == 0
    nrow = seq // GRID_W
    assert LRU_RB == 2 and nrow % LRU_RB == 0 and nrow >= 4

    c8 = jnp.zeros((8, d), F32).at[:bsz].set(c).at[bsz].set(c_ctx)
    mod = _modulation(c8, w_mod[0], b_mod[0])
    sh, sc, gt = mod[:, :d], mod[:, d:2 * d], mod[:, 2 * d:]
    gs = norm_g[0][None, :] * (1.0 + sc)

    w_in0 = w_in[0]
    assert w_in0.shape == (d, 4 * e)
    wt_us5 = w_in0[:, W_COL_US5 * e:(W_COL_US5 + 1) * e].T.astype(BF16)
    gate_cols = (jnp.arange(4 * e) // e == W_COL_GS5) | (jnp.arange(4 * e) // e == W_COL_GLRU)
    w_all = (w_in0 * jnp.where(gate_cols, 0.5, 1.0)[None, :]).astype(BF16)
    w_glu = (0.5 * s5_w_glu[0]).astype(BF16)
    wo = w_out[0].astype(BF16)

    s5_params = _s5_param_layout(s5_a_re[0], s5_a_im[0], s5_log_step[0], s5_b_re[0], s5_b_im[0],
                                 s5_c_re[0], s5_c_im[0], s5_d[0])
    x3 = x.reshape(ns, S5_SEG, d)
    xt, hg5, xn = _s5_inproj(x3, gs, sh, wt_us5, w_all, bsz)
    ctx_pad = jnp.zeros((bsz, nseg, S5_T, d), F32).at[:, :nctx].set(ctx.reshape(bsz, nctx, S5_T, d))
    gs_c = jnp.broadcast_to(gs[bsz:bsz + 1], (8, d))
    sh_c = jnp.broadcast_to(sh[bsz:bsz + 1], (8, d))
    xct, _, _ = _s5_inproj(ctx_pad.reshape(ns, S5_T, d), gs_c, sh_c, wt_us5, w_all, bsz)
    y5 = _s5_core(xt, xct, s5_params, nseg, nctx)
    nj = S5_SEG // S5_T
    m5 = _s5_post(y5.reshape(nj, S5_T, e, ns), hg5, w_glu, 0.5 * s5_b_glu[0].reshape(1, e))

    wg_all = _gate_tiles(0.5 * lru_w_a[0], 0.5 * lru_w_x[0])
    gb_all = 0.5 * jnp.stack([lru_b_a[0, 0], lru_b_x[0, 0], lru_b_a[0, 1], lru_b_x[0, 1]])
    cw = lru_conv_w[0]
    cb = lru_conv_b[0].reshape(1, e)
    lam = lru_lam[0]
    ctx_p = jnp.zeros((8, nctx_tok, d), F32).at[:bsz].set(ctx).transpose(1, 0, 2).reshape(nctx_tok * 8, d)
    ff, fr = _lru_ctx(ctx_p, gs[bsz:bsz + 1], sh[bsz:bsz + 1], w_all, cw, cb, wg_all, gb_all, lam)
    x4 = x.reshape(bsz, nrow, GRID_W, d)
    lar, br, hloc, pcum, hf, pf, pr, hr = _lru_pass_a(xn.reshape(bsz, nrow, GRID_W, d), w_all, cw, cb, wg_all,
                                                      gb_all, lam)
    hs_f, hs_r = _lru_stitch(pf, hf, pr, hr, jnp.repeat(ff[:bsz], GRID_W, axis=0),
                             jnp.repeat(fr[:bsz], GRID_W, axis=0), GRID_W)
    out4 = _lru_rev_out(lar, br, hloc, pcum, x4, m5.reshape(bsz, nrow, GRID_W, e), gs, sh, gt, hs_f, hs_r,
                        w_all, wo, final_g.reshape(1, d))
    return out4.reshape(bsz, seq, d)
```

```python
import functools
import math

import jax
import jax.numpy as jnp
from jax import lax
from jax.experimental import pallas as pl
from jax.experimental.pallas import tpu as pltpu

F32 = jnp.float32
BF16 = jnp.bfloat16

EPS = 1e-6
TINY = 1e-30
GRID_W = 64
LRU_C = 8.0
S5_H = 16
S5_N = 64
S5_T = 16
S5_SEG = 256
S5_TB = 8
S5_PB = 16
GATE_TILE = 256
LRU_RB = 2
LRU_WARM = 2
LRU_RC = 32
W_COL_US5, W_COL_GS5, W_COL_ULRU, W_COL_GLRU = 0, 1, 2, 3
V7X_VMEM_BYTES = 64 * 1024 * 1024
VMEM_LIMIT = V7X_VMEM_BYTES * 7 // 8


def _cparams(sem):
    return pltpu.CompilerParams(dimension_semantics=sem, vmem_limit_bytes=VMEM_LIMIT)


def _sigmoid(z):
    return 0.5 * (jnp.tanh(0.5 * z) + 1.0)


def _gelu_tanh(y):
    return 0.5 * y * (1.0 + jnp.tanh(math.sqrt(2.0 / math.pi) * (y + 0.044715 * (y * y * y))))


def _norm_mod(x, gs, sh):
    ms = jnp.mean(x * x, axis=-1, keepdims=True)
    return x * lax.rsqrt(ms + EPS) * gs + sh


def _mod_kernel(c_ref, w_ref, b_ref, o_ref):
    c = c_ref[...]
    s = c * _sigmoid(c)
    o_ref[...] = jnp.dot(s, w_ref[...], preferred_element_type=F32) + b_ref[...]


def _modulation(c8, w_mod, b_mod):
    d = c8.shape[1]
    n = w_mod.shape[1]
    nb = n // d
    return pl.pallas_call(
        _mod_kernel,
        grid=(nb,),
        in_specs=[pl.BlockSpec((8, d), lambda i: (0, 0)),
                  pl.BlockSpec((d, d), lambda i: (0, i)),
                  pl.BlockSpec((1, d), lambda i: (0, i))],
        out_specs=pl.BlockSpec((8, d), lambda i: (0, i)),
        out_shape=jax.ShapeDtypeStruct((8, n), F32),
        compiler_params=_cparams(("arbitrary",)),
    )(c8, w_mod, b_mod.reshape(1, n))


def _perm_matrix(n_a, n_b):
    n = n_a * n_b
    r = lax.broadcasted_iota(jnp.int32, (n, n), 0)
    c = lax.broadcasted_iota(jnp.int32, (n, n), 1)
    return ((r // n_b == c % n_a) & (r % n_b == c // n_a)).astype(BF16)


def _s5_inproj_kernel(x_ref, gs_ref, sh_ref, wt_ref, wg_ref, o_ref, hg_ref, xs_ref, *, nbatch):
    ns, tb, d = x_ref.shape
    groups = o_ref.shape[0]
    rows_b = (ns // nbatch) * tb
    blk = S5_PB * tb
    perm = _perm_matrix(tb, S5_PB)
    x = x_ref[...].reshape(ns * tb, d)
    xn = jnp.concatenate(
        [_norm_mod(x[b * rows_b:(b + 1) * rows_b], gs_ref[b:b + 1], sh_ref[b:b + 1]).astype(BF16)
         for b in range(nbatch)], axis=0)
    for k in range(ns // S5_PB):
        q = jnp.dot(perm, xn[k * blk:(k + 1) * blk], preferred_element_type=F32).astype(BF16)
        for t in range(tb):
            xs_ref[t * ns + k * S5_PB:t * ns + (k + 1) * S5_PB, :] = q[t * S5_PB:(t + 1) * S5_PB]
    wt = wt_ref[...]
    for t in range(0, tb, 2):
        ut = lax.dot_general(wt, xs_ref[t * ns:(t + 2) * ns, :], (((1,), (1,)), ((), ())),
                             preferred_element_type=F32)
        o_ref[:, t * S5_H:(t + 1) * S5_H, :] = ut[:, :ns].reshape(groups, S5_H, ns).astype(BF16)
        o_ref[:, (t + 1) * S5_H:(t + 2) * S5_H, :] = ut[:, ns:].reshape(groups, S5_H, ns).astype(BF16)
    hg = jnp.dot(xs_ref[...], wg_ref[...], preferred_element_type=F32)
    hg_ref[...] = hg.astype(BF16).reshape(tb, ns, hg.shape[1])


def _s5_inproj(x3, gs8, sh8, wt, w_all, nbatch):
    ns, seg, d = x3.shape
    e = wt.shape[0]
    groups = e // S5_H
    nj = seg // S5_T
    per = S5_T // S5_TB
    return pl.pallas_call(
        functools.partial(_s5_inproj_kernel, nbatch=nbatch),
        grid=(seg // S5_TB,),
        in_specs=[pl.BlockSpec((ns, S5_TB, d), lambda i: (0, i, 0)),
                  pl.BlockSpec((8, d), lambda i: (0, 0)),
                  pl.BlockSpec((8, d), lambda i: (0, 0)),
                  pl.BlockSpec((e, d), lambda i: (0, 0)),
                  pl.BlockSpec((d, e), lambda i: (0, W_COL_GS5))],
        out_specs=[pl.BlockSpec((groups, S5_TB * S5_H, ns), lambda i: (0, i % per, i // per)),
                   pl.BlockSpec((S5_TB, ns, e), lambda i: (i, 0, 0))],
        out_shape=[jax.ShapeDtypeStruct((groups, S5_T * S5_H, nj * ns), BF16),
                   jax.ShapeDtypeStruct((seg, ns, e), BF16)],
        scratch_shapes=[pltpu.VMEM((S5_TB * ns, d), BF16)],
        compiler_params=_cparams(("arbitrary",)),
        name="s5_inproj",
    )(x3, gs8, sh8, wt, w_all)


def _cmul_add(dr, di, hr, hi, sr, si):
    return dr * hr - di * hi + sr, dr * hi + di * hr + si


def _lane_scan(vr, vi, dr, di, slot, count, reverse):
    n = vr.shape[1]
    k = 1
    while k < count:
        if reverse:
            sr = pltpu.roll(vr, n - k, 1)
            si = pltpu.roll(vi, n - k, 1)
            keep = slot < count - k
        else:
            sr = pltpu.roll(vr, k, 1)
            si = pltpu.roll(vi, k, 1)
            keep = slot >= k
        sr = jnp.where(keep, sr, 0.0)
        si = jnp.where(keep, si, 0.0)
        vr, vi = _cmul_add(dr, di, sr, si, vr, vi)
        dr, di = dr * dr - di * di, 2.0 * dr * di
        k *= 2
    return vr, vi


def _csq(r, i):
    return r * r - i * i, 2.0 * r * i


def _cpow_bits(e, pows):
    pr = jnp.ones(e.shape, F32)
    pi = jnp.zeros(e.shape, F32)
    for bit, (qr, qi) in pows:
        on = (e & bit) != 0
        pr, pi = jnp.where(on, pr * qr - pi * qi, pr), jnp.where(on, pr * qi + pi * qr, pi)
    return pr, pi


def _s5_discretize(ar, ai, ls):
    step = jnp.exp(ls)
    mag = jnp.exp(ar * step)
    pr = mag * jnp.cos(ai * step)
    pi = mag * jnp.sin(ai * step)
    den = ar * ar + ai * ai
    nr = pr - 1.0
    return pr, pi, (nr * ar + pi * ai) / den, (pi * ar - nr * ai) / den


def _s5_operators(pcol_ref, prow_ref, bt_ref, ct1_ref, ct2_ref, dsk_ref):
    n, t_len, th = S5_N, S5_T, S5_T * S5_H
    hp = lax.Precision.HIGHEST
    lane_t = lax.broadcasted_iota(jnp.int32, (1, th), 1) // S5_H
    bits = [1 << b for b in range(t_len.bit_length() - 1)]

    tile_h = (lax.broadcasted_iota(jnp.int32, (S5_H, th), 1) % S5_H
              == lax.broadcasted_iota(jnp.int32, (S5_H, th), 0)).astype(F32)
    bt = jnp.dot(bt_ref[...], tile_h, precision=hp, preferred_element_type=F32)

    blocks, bbs, decay = [], [], []
    for d in range(2):
        pr, pi, cr, ci = _s5_discretize(pcol_ref[3 * d], pcol_ref[3 * d + 1], pcol_ref[3 * d + 2])
        pows, q = [], (pr, pi)
        for bit in bits:
            pows.append((bit, q))
            q = _csq(*q)
        decay.append(q)
        btr = bt[2 * d * n:(2 * d + 1) * n, :]
        bti = bt[(2 * d + 1) * n:(2 * d + 2) * n, :]
        bbr = cr * btr - ci * bti
        bbi = cr * bti + ci * btr
        e = (t_len - 1 - lane_t) if d == 0 else lane_t
        wr, wi = _cpow_bits(jnp.broadcast_to(e, (n, th)), pows)
        blocks += [wr * bbr - wi * bbi, wr * bbi + wi * bbr]
        bbs.append(jnp.concatenate([bbr, bbi], axis=0))
    bpow = jnp.concatenate(blocks, axis=0).astype(BF16)

    qr, qi, _, _ = _s5_discretize(prow_ref[0:1], prow_ref[1:2], prow_ref[2:3])
    pows, q = [], (qr, qi)
    for bit in bits:
        pows.append((bit, q))
        q = _csq(*q)
    q_t = q
    row_t = lax.broadcasted_iota(jnp.int32, (th, 1), 0) // S5_H
    is_f = lax.broadcasted_iota(jnp.int32, (1, 4 * n), 1) < 2 * n
    e_k = jnp.where(is_f, row_t, (t_len - row_t) % t_len)
    kr, ki = _cpow_bits(e_k, pows)
    nr = jnp.where(is_f, kr * qr - ki * qi, jnp.where(row_t == 0, q_t[0], kr))
    ni = jnp.where(is_f, kr * qi + ki * qr, jnp.where(row_t == 0, q_t[1], ki))
    ct1 = jnp.concatenate([ct1_ref[...]] * t_len, axis=0)
    ct2 = jnp.concatenate([ct2_ref[...]] * t_len, axis=0)
    cpow = (ct1 * nr + ct2 * ni).astype(BF16)
    qk = ct1 * kr + ct2 * ki
    mf = jnp.dot(qk[:, :2 * n], bbs[0], precision=hp, preferred_element_type=F32)
    mr = jnp.dot(qk[:, 2 * n:], bbs[1], precision=hp, preferred_element_type=F32)
    m = (jnp.where(row_t <= t_len - 1 - lane_t, mf, 0.0)
         + jnp.where((row_t == 0) | (row_t >= t_len - lane_t), mr, 0.0))
    for bit in bits:
        m = jnp.where((lane_t & bit) != 0, pltpu.roll(m, S5_H * bit, 0), m)
    row = lax.broadcasted_iota(jnp.int32, (th, th), 0)
    col = lax.broadcasted_iota(jnp.int32, (th, th), 1)
    toep = (m + jnp.where(row == col, dsk_ref[...], 0.0)).astype(BF16)
    return toep, bpow, cpow, decay


def _s5_core_kernel(xt_ref, xc_ref, pcol_ref, pp_ref, bt_ref, o_ref, s_ref, *, nseg, nctx):
    n = S5_N
    h = S5_H
    ns = xc_ref.shape[-1]
    nj = xt_ref.shape[-1] // ns
    toep, bpow, cpow, decay = _s5_operators(pcol_ref, pp_ref.at[2 * h:2 * h + 3], bt_ref, pp_ref.at[0:h],
                                            pp_ref.at[h:2 * h], pp_ref.at[2 * h + 3:2 * h + 4])
    dfr = jnp.broadcast_to(decay[0][0], (n, ns))
    dfi = jnp.broadcast_to(decay[0][1], (n, ns))
    drr = jnp.broadcast_to(decay[1][0], (n, ns))
    dri = jnp.broadcast_to(decay[1][1], (n, ns))
    slot = lax.broadcasted_iota(jnp.int32, (n, ns), 1) % nseg

    sc = jnp.dot(bpow, xc_ref[...], preferred_element_type=F32)
    valid = slot < nctx
    cfr, cfi = _lane_scan(jnp.where(valid, sc[0:n], 0.0), jnp.where(valid, sc[n:2 * n], 0.0),
                          dfr, dfi, slot, nseg, False)
    crr, cri = _lane_scan(jnp.where(valid, sc[2 * n:3 * n], 0.0), jnp.where(valid, sc[3 * n:4 * n], 0.0),
                          drr, dri, slot, nseg, True)

    step = 2 if nj % 2 == 0 else 1
    for j in range(0, nj, step):
        cols = slice(j * ns, (j + step) * ns)
        s_ref[:, cols] = jnp.dot(bpow, xt_ref[:, cols], preferred_element_type=F32)

    zero = jnp.zeros((n, ns), F32)
    efr, efi, err, eri = zero, zero, zero, zero
    for j in range(nj):
        cols = slice(j * ns, (j + 1) * ns)
        efr, efi = _cmul_add(dfr, dfi, efr, efi, s_ref[0:n, cols], s_ref[n:2 * n, cols])
    for j in range(nj - 1, -1, -1):
        cols = slice(j * ns, (j + 1) * ns)
        err, eri = _cmul_add(drr, dri, err, eri, s_ref[2 * n:3 * n, cols], s_ref[3 * n:4 * n, cols])

    sfr, sfi, srr, sri = dfr, dfi, drr, dri
    k = 1
    while k < nj:
        sfr, sfi = sfr * sfr - sfi * sfi, 2.0 * sfr * sfi
        srr, sri = srr * srr - sri * sri, 2.0 * srr * sri
        k *= 2

    first = slot == 0
    last = slot == nseg - 1
    vfr = jnp.where(first, pltpu.roll(cfr, ns - (nctx - 1), 1) if nctx > 1 else cfr, pltpu.roll(efr, 1, 1))
    vfi = jnp.where(first, pltpu.roll(cfi, ns - (nctx - 1), 1) if nctx > 1 else cfi, pltpu.roll(efi, 1, 1))
    vrr = jnp.where(last, pltpu.roll(crr, nseg - 1, 1) if nseg > 1 else crr, pltpu.roll(err, ns - 1, 1))
    vri = jnp.where(last, pltpu.roll(cri, nseg - 1, 1) if nseg > 1 else cri, pltpu.roll(eri, ns - 1, 1))
    hfr, hfi = _lane_scan(vfr, vfi, sfr, sfi, slot, nseg, False)
    hrr, hri = _lane_scan(vrr, vri, srr, sri, slot, nseg, True)

    for j in range(nj):
        cols = slice(j * ns, (j + 1) * ns)
        nr, ni = _cmul_add(dfr, dfi, hfr, hfi, s_ref[0:n, cols], s_ref[n:2 * n, cols])
        s_ref[0:n, cols] = hfr
        s_ref[n:2 * n, cols] = hfi
        hfr, hfi = nr, ni
    for j in range(nj - 1, -1, -1):
        cols = slice(j * ns, (j + 1) * ns)
        nr, ni = _cmul_add(drr, dri, hrr, hri, s_ref[2 * n:3 * n, cols], s_ref[3 * n:4 * n, cols])
        s_ref[2 * n:3 * n, cols] = hrr
        s_ref[3 * n:4 * n, cols] = hri
        hrr, hri = nr, ni

    for j in range(0, nj, step):
        cols = slice(j * ns, (j + step) * ns)
        y = (jnp.dot(toep, xt_ref[:, cols], preferred_element_type=F32)
             + jnp.dot(cpow, s_ref[:, cols].astype(BF16), preferred_element_type=F32))
        for q in range(step):
            o_ref[j + q] = y[:, q * ns:(q + 1) * ns].reshape(S5_T, S5_H, ns).astype(BF16)


def _s5_core(xt, xct, params, nseg, nctx):
    groups, rows, lanes = xt.shape
    ns = xct.shape[-1]
    nj = lanes // ns
    kern = functools.partial(_s5_core_kernel, nseg=nseg, nctx=nctx)
    per_group = lambda a: pl.BlockSpec((None,) + a.shape[1:], lambda g, nd=a.ndim: (g,) + (0,) * (nd - 1))
    return pl.pallas_call(
        kern,
        grid=(groups,),
        in_specs=[per_group(xt), per_group(xct)] + [per_group(a) for a in params],
        out_specs=pl.BlockSpec((nj, S5_T, None, S5_H, ns), lambda g: (0, 0, g, 0, 0)),
        out_shape=jax.ShapeDtypeStruct((nj, S5_T, groups, S5_H, ns), BF16),
        scratch_shapes=[pltpu.VMEM((rows, lanes), F32)],
        compiler_params=_cparams(("arbitrary",)),
        name="s5_core",
    )(xt, xct, *params)


def _s5_param_layout(a_re, a_im, log_step, b_re, b_im, c_re, c_im, d_skip):
    ndir, groups, n = a_re.shape
    h = b_re.shape[-1]
    ls = jnp.broadcast_to(log_step[..., None], a_re.shape)
    pcol = jnp.stack([a_re[0], a_im[0], ls[0], a_re[1], a_im[1], ls[1]], axis=1)[..., None]
    lay = lambda v: jnp.concatenate([v[0], v[0], v[1], v[1]], axis=-1)
    prow = jnp.stack([lay(a_re), lay(a_im), lay(ls)], axis=1)
    bt = jnp.concatenate([b_re[0], b_im[0], b_re[1], b_im[1]], axis=1)
    ct1 = jnp.concatenate([c_re[0], -c_im[0], c_re[1], -c_im[1]], axis=-1)
    ct2 = jnp.concatenate([-c_im[0], -c_re[0], -c_im[1], -c_re[1]], axis=-1)
    dsk = jnp.tile(d_skip.reshape(groups, 1, h), (1, 1, S5_T))
    pad = jnp.zeros((groups, (-(2 * h + 4)) % 8, S5_T * h), F32)
    pp = jnp.concatenate([ct1, ct2, prow, dsk, pad], axis=1)
    return tuple(v.astype(F32) for v in (pcol, pp, bt))


def _s5_post_kernel(y_ref, hg_ref, wglu_ref, bglu_ref, o_ref, m_ref):
    q = pl.program_id(1)
    tb, e, ns = y_ref.shape
    ys = jnp.concatenate([y_ref[t].astype(F32).T for t in range(tb)], axis=0)
    yg = _gelu_tanh(ys)
    zh = jnp.dot(yg.astype(BF16), wglu_ref[...], preferred_element_type=F32) + bglu_ref[...]
    hg = hg_ref[...].reshape(tb * ns, e).astype(F32)
    m = ((yg * (0.5 * jnp.tanh(zh) + 0.5)) * (hg * jnp.tanh(hg) + hg)).astype(BF16)
    nq = S5_T // tb
    for half in range(nq):
        @pl.when(q == half)
        def _(half=half):
            m_ref[half * tb * ns:(half + 1) * tb * ns, :] = m

    @pl.when(q == nq - 1)
    def _():
        perm = _perm_matrix(S5_PB, S5_T)
        for k in range(ns // S5_PB):
            g = jnp.concatenate([m_ref[t * ns + k * S5_PB:t * ns + (k + 1) * S5_PB, :] for t in range(S5_T)], axis=0)
            r = jnp.dot(perm, g, preferred_element_type=F32).astype(BF16)
            o_ref[k * S5_PB:(k + 1) * S5_PB] = r.reshape(S5_PB, S5_T, e)


def _s5_post(y4, hg, wglu, bglu):
    nj, t_all, e, ns = y4.shape
    nt = t_all // S5_TB
    return pl.pallas_call(
        _s5_post_kernel,
        grid=(nj, nt),
        in_specs=[pl.BlockSpec((None, S5_TB, e, ns), lambda j, q: (j, q, 0, 0)),
                  pl.BlockSpec((S5_TB, ns, e), lambda j, q: (j * nt + q, 0, 0)),
                  pl.BlockSpec((e, e), lambda j, q: (0, 0)),
                  pl.BlockSpec((1, e), lambda j, q: (0, 0))],
        out_specs=pl.BlockSpec((ns, S5_T, e), lambda j, q: (0, j, 0)),
        out_shape=jax.ShapeDtypeStruct((ns, nj * S5_T, e), BF16),
        scratch_shapes=[pltpu.VMEM((S5_T * ns, e), BF16)],
        compiler_params=_cparams(("arbitrary", "arbitrary")),
        name="s5_post",
    )(y4, hg, wglu, bglu)


def _softplus(z):
    return jnp.maximum(z, 0.0) + jnp.log1p(jnp.exp(-jnp.abs(z)))


def _gate_matmul(xc, wg_ref):
    xb = xc.astype(BF16)
    nq = wg_ref.shape[0]
    return [jnp.dot(xb[:, q * GATE_TILE:(q + 1) * GATE_TILE], wg_ref[q], preferred_element_type=F32)
            for q in range(nq)]


def _gate_cols(zs, k):
    return jnp.concatenate([z[:, k * GATE_TILE:(k + 1) * GATE_TILE] for z in zs], axis=1)


def _lru_decay_scale(lam_row):
    return (-0.5 * LRU_C * math.log2(math.e)) * _softplus(-lam_row)


def _lru_gate_math(hx, za, zx, ba, bx, c2):
    ta = jnp.tanh(za + ba)
    tx = jnp.tanh(zx + bx)
    la = ta * c2 + c2
    a = jnp.exp2(la)
    om = 1.0 - a * a
    return la, a, (om * lax.rsqrt(jnp.maximum(om, TINY))) * (tx * hx + hx)


def _lru_coeffs(hx, zs, k, gbh, c2):
    return _lru_gate_math(hx, _gate_cols(zs, 2 * k), _gate_cols(zs, 2 * k + 1),
                          gbh[2 * k:2 * k + 1], gbh[2 * k + 1:2 * k + 2], c2)[1:]


def _norm_mod_rows(x_ref, gs_ref, sh_ref, dst_ref):
    nb, w, _ = x_ref.shape
    for b in range(nb):
        dst_ref[b * w:(b + 1) * w, :] = _norm_mod(x_ref[b], gs_ref[b:b + 1], sh_ref[b:b + 1]).astype(BF16)


def _lru_ctx_kernel(x_ref, gs_ref, sh_ref, wu_ref, cw_ref, cb_ref, wg_ref, gb_ref, lam_ref,
                    ff_ref, fr_ref, af_s, bf_s, ar_s, br_s, xn_s):
    rows = x_ref.shape[0]
    npos = rows // 8

    @pl.when(pl.program_id(0) == 0)
    def _():
        xn_s[...] = _norm_mod(x_ref[...], gs_ref[...], sh_ref[...]).astype(BF16)

    u = jnp.dot(xn_s[...], wu_ref[...], preferred_element_type=F32)
    pos = lax.broadcasted_iota(jnp.int32, u.shape, 0) // 8
    um1 = jnp.where(pos >= 1, pltpu.roll(u, 8, 0), 0.0)
    up1 = jnp.where(pos < npos - 1, pltpu.roll(u, rows - 8, 0), 0.0)
    up2 = jnp.where(pos < npos - 2, pltpu.roll(u, rows - 16, 0), 0.0)
    cw = cw_ref[...]
    xc = cw[0:1] * um1 + cw[1:2] * u + cw[2:3] * up1 + cw[3:4] * up2 + cb_ref[...]
    zs = _gate_matmul(xc, wg_ref)
    gbh = gb_ref[...]
    c2 = _lru_decay_scale(lam_ref[...])
    hx = 0.5 * xc
    a, b = _lru_coeffs(hx, zs, 0, gbh, c2[0:1])
    af_s[...] = a
    bf_s[...] = b
    a, b = _lru_coeffs(hx, zs, 1, gbh, c2[1:2])
    ar_s[...] = a
    br_s[...] = b

    def body(p, carry):
        hf, hr, pr = carry
        r0 = pl.multiple_of(p * 8, 8)
        hf = af_s[pl.ds(r0, 8), :] * hf + bf_s[pl.ds(r0, 8), :]
        hr = hr + pr * br_s[pl.ds(r0, 8), :]
        pr = pr * ar_s[pl.ds(r0, 8), :]
        return hf, hr, pr

    zero = jnp.zeros((8, u.shape[1]), F32)
    hf, hr, _ = lax.fori_loop(0, npos, body, (zero, zero, zero + 1.0))
    ff_ref[...] = hf
    fr_ref[...] = hr


def _lru_ctx(xp, gs, sh, w_all, cw, cb, wg, gb, lam):
    rows, d = xp.shape
    e = cw.shape[1]
    nq = e // GATE_TILE
    full = lambda a: pl.BlockSpec(a.shape, lambda q, nd=a.ndim: (0,) * nd)
    cols = lambda a: pl.BlockSpec((a.shape[0], GATE_TILE), lambda q: (0, q))
    wu_cols = pl.BlockSpec((d, GATE_TILE), lambda q: (0, W_COL_ULRU * nq + q))
    return pl.pallas_call(
        _lru_ctx_kernel,
        grid=(nq,),
        in_specs=[full(xp), full(gs), full(sh), wu_cols, cols(cw), cols(cb),
                  pl.BlockSpec((1,) + wg.shape[1:], lambda q: (q, 0, 0)), cols(gb), cols(lam)],
        out_specs=[pl.BlockSpec((8, GATE_TILE), lambda q: (0, q)), pl.BlockSpec((8, GATE_TILE), lambda q: (0, q))],
        out_shape=[jax.ShapeDtypeStruct((8, e), F32), jax.ShapeDtypeStruct((8, e), F32)],
        scratch_shapes=[pltpu.VMEM((rows, GATE_TILE), F32) for _ in range(4)] + [pltpu.VMEM((rows, d), BF16)],
        compiler_params=_cparams(("arbitrary",)),
    )(xp, gs, sh, w_all, cw, cb, wg, gb, lam)


def _lru_pass_a_kernel(xa_ref, xb_ref, gs_ref, sh_ref, wu_ref, cw_ref, cb_ref, wg_ref, gb_ref, lam_ref,
                       lar_ref, br_ref, hl_ref, pc_ref, hs_ref, ps_ref, pr_ref, hr_ref,
                       ring_ref, xn_ref, xc_ref, xcb_ref, z_ref, *, nrow):
    i = pl.program_id(0)
    nb, w, d = xa_ref.shape
    rows = nb * w
    nq = wg_ref.shape[0]
    gt = GATE_TILE

    @pl.when(i == 0)
    def _():
        ring_ref[...] = jnp.zeros_like(ring_ref)

    @pl.when(i <= LRU_WARM)
    def _():
        hs_ref[...] = jnp.zeros_like(hs_ref)
        ps_ref[...] = jnp.ones_like(ps_ref)
        hr_ref[...] = jnp.zeros_like(hr_ref)
        pr_ref[...] = jnp.ones_like(pr_ref)

    r = lax.broadcasted_iota(jnp.int32, (rows, rows), 0)
    c = lax.broadcasted_iota(jnp.int32, (rows, rows), 1)
    same_batch = r // w == c // w
    shifted = []
    for k, x_ref in enumerate((xa_ref, xb_ref)):
        _norm_mod_rows(x_ref, gs_ref, sh_ref, xn_ref.at[k * rows:(k + 1) * rows])
        s = 2 * i + k
        off = jnp.where(s == 0, -1, jnp.where(s > nrow, 1, 0))
        mix = ((c == r + off) & same_batch).astype(BF16)
        shifted.append(jnp.dot(mix, xn_ref[k * rows:(k + 1) * rows, :], preferred_element_type=F32).astype(BF16))
    u = jnp.dot(jnp.concatenate(shifted, axis=0), wu_ref[...], preferred_element_type=F32)
    ring_ref[(2 * i) & 7] = u[0:rows]
    ring_ref[(2 * i + 1) & 7] = u[rows:2 * rows]

    cw = cw_ref[...]
    cb = cb_ref[...]
    for k in range(LRU_RB):
        sl = [(2 * i - 2 * LRU_WARM + k + t) & 7 for t in range(4)]
        for r0 in range(0, rows, LRU_RC):
            rs = slice(r0, r0 + LRU_RC)
            ks = slice(k * rows + r0, k * rows + r0 + LRU_RC)
            xc = (cw[0:1] * ring_ref[sl[0], rs, :] + cw[1:2] * ring_ref[sl[1], rs, :]
                  + cw[2:3] * ring_ref[sl[2], rs, :] + cw[3:4] * ring_ref[sl[3], rs, :] + cb)
            xc_ref[ks, :] = xc
            xcb_ref[ks, :] = xc.astype(BF16)
    for q in range(nq):
        z_ref[q] = jnp.dot(xcb_ref[:, q * gt:(q + 1) * gt], wg_ref[q], preferred_element_type=F32)
    gbh = gb_ref[...]
    c2 = _lru_decay_scale(lam_ref[...])
    for k in range(LRU_RB):
        for q in range(nq):
            cs = slice(q * gt, (q + 1) * gt)
            for r0 in range(0, rows, LRU_RC):
                rs = slice(r0, r0 + LRU_RC)
                ks = slice(k * rows + r0, k * rows + r0 + LRU_RC)
                hx = 0.5 * xc_ref[ks, cs]
                _, af, bf = _lru_gate_math(hx, z_ref[q, ks, 0:gt], z_ref[q, ks, gt:2 * gt],
                                           gbh[0:1, cs], gbh[1:2, cs], c2[0:1, cs])
                lar, ar, br = _lru_gate_math(hx, z_ref[q, ks, 2 * gt:3 * gt], z_ref[q, ks, 3 * gt:4 * gt],
                                             gbh[2:3, cs], gbh[3:4, cs], c2[1:2, cs])
                h = af * hs_ref[rs, cs] + bf
                p = ps_ref[rs, cs] * af
                hs_ref[rs, cs] = h
                ps_ref[rs, cs] = p
                hl_ref[k, rs, cs] = h.astype(BF16)
                pc_ref[k, rs, cs] = p.astype(BF16)
                lar_ref[k, rs, cs] = lar.astype(BF16)
                br_ref[k, rs, cs] = br.astype(BF16)
                prod = pr_ref[rs, cs]
                hr_ref[rs, cs] = hr_ref[rs, cs] + prod * br
                pr_ref[rs, cs] = prod * ar


def _lru_pass_a(x4, gs8, sh8, w_all, cw, cb, wg, gb, lam):
    nb, nrow, w, d = x4.shape
    e = cw.shape[1]
    rows = nb * w
    rb = LRU_RB
    kern = functools.partial(_lru_pass_a_kernel, nrow=nrow)
    const = lambda a: pl.BlockSpec(a.shape, lambda i, nd=a.ndim: (0,) * nd)
    acc = pl.BlockSpec((rows, e), lambda i: (0, 0))
    acc_shape = jax.ShapeDtypeStruct((rows, e), F32)
    per_row = pl.BlockSpec((rb, rows, e), lambda i: (jnp.maximum(i - LRU_WARM, 0), 0, 0))
    per_row_shape = jax.ShapeDtypeStruct((nrow, rows, e), BF16)
    x_row = lambda first: pl.BlockSpec((nb, None, w, d), lambda i: (0, (rb * i + first) % nrow, 0, 0))
    return pl.pallas_call(
        kern,
        grid=(nrow // rb + LRU_WARM,),
        in_specs=[x_row(nrow - 1), x_row(0),
                  const(gs8), const(sh8), pl.BlockSpec((d, e), lambda i: (0, W_COL_ULRU)), const(cw), const(cb),
                  const(wg), const(gb), const(lam)],
        out_specs=[per_row] * 4 + [acc] * 4,
        out_shape=[per_row_shape] * 4 + [acc_shape] * 4,
        scratch_shapes=[pltpu.VMEM((8, rows, e), F32), pltpu.VMEM((rb * rows, d), BF16),
                        pltpu.VMEM((rb * rows, e), F32), pltpu.VMEM((rb * rows, e), BF16),
                        pltpu.VMEM((wg.shape[0], rb * rows, wg.shape[2]), F32)],
        compiler_params=_cparams(("arbitrary",)),
        name="lru_pass_a",
    )(x4, x4, gs8, sh8, w_all, cw, cb, wg, gb, lam)


def _lru_stitch_kernel(pf_ref, hf_ref, pr_ref, hr_ref, ff_ref, fr_ref, sf_ref, sr_ref, *, w):
    rows = pf_ref.shape[0]
    col = lax.broadcasted_iota(jnp.int32, pf_ref.shape, 0) % w
    p, h = pf_ref[...], hf_ref[...]
    k = 1
    while k < w:
        keep = col >= k
        ps = jnp.where(keep, pltpu.roll(p, k, 0), 1.0)
        hs = jnp.where(keep, pltpu.roll(h, k, 0), 0.0)
        p, h = p * ps, p * hs + h
        k *= 2
    fin = ff_ref[...]
    sf_ref[...] = jnp.where(col == 0, fin, pltpu.roll(p, 1, 0) * fin + pltpu.roll(h, 1, 0))
    p, h = pr_ref[...], hr_ref[...]
    k = 1
    while k < w:
        keep = col < w - k
        ps = jnp.where(keep, pltpu.roll(p, rows - k, 0), 1.0)
        hs = jnp.where(keep, pltpu.roll(h, rows - k, 0), 0.0)
        p, h = p * ps, p * hs + h
        k *= 2
    fin = fr_ref[...]
    sr_ref[...] = jnp.where(col == w - 1, fin, pltpu.roll(p, rows - 1, 0) * fin + pltpu.roll(h, rows - 1, 0))


def _lru_stitch(pf, hf, pr, hr, ff_rows, fr_rows, w):
    rows, e = pf.shape
    big = pl.BlockSpec((rows, e), lambda i: (0, 0))
    shp = jax.ShapeDtypeStruct((rows, e), F32)
    return pl.pallas_call(
        functools.partial(_lru_stitch_kernel, w=w),
        grid=(1,),
        in_specs=[big] * 6,
        out_specs=[big, big],
        out_shape=[shp, shp],
        compiler_params=_cparams(("arbitrary",)),
        name="lru_stitch",
    )(pf, hf, pr, hr, ff_rows, fr_rows)


def _lru_rev_out_kernel(lar_ref, br_ref, hl_ref, pc_ref, x_ref, m5_ref, gs_ref, sh_ref, gt_ref, hsf_ref, hsr_ref,
                        wgl_ref, wo_ref, fg_ref, o_ref, h_ref, xb_ref, g_ref, mb_ref, mix_ref):
    i = pl.program_id(0)
    nb, nrb, w, d = x_ref.shape
    rows = nb * w
    e = h_ref.shape[-1]

    @pl.when(i == 0)
    def _():
        h_ref[...] = hsr_ref[...]

    for rr in range(nrb):
        for b in range(nb):
            xb_ref[rr * rows + b * w:rr * rows + (b + 1) * w, :] = _norm_mod(
                x_ref[b, rr], gs_ref[b:b + 1], sh_ref[b:b + 1]).astype(BF16)
    g_ref[...] = jnp.dot(xb_ref[...], wgl_ref[...], preferred_element_type=F32)
    for rr in range(nrb - 1, -1, -1):
        for c0 in range(0, e, GATE_TILE):
            cs = slice(c0, c0 + GATE_TILE)
            for r0 in range(0, rows, LRU_RC):
                rs = slice(r0, r0 + LRU_RC)
                gs_rows = slice(rr * rows + r0, rr * rows + r0 + LRU_RC)
                h = jnp.exp2(lar_ref[rr, rs, cs].astype(F32)) * h_ref[rs, cs] + br_ref[rr, rs, cs].astype(F32)
                h_ref[rs, cs] = h
                y = hl_ref[rr, rs, cs].astype(F32) + pc_ref[rr, rs, cs].astype(F32) * hsf_ref[rs, cs] + h
                hg = g_ref[gs_rows, cs]
                mb_ref[gs_rows, cs] = (y * (hg * jnp.tanh(hg) + hg)).astype(BF16)
    m5 = jnp.concatenate([m5_ref[:, rr].reshape(rows, e) for rr in range(nrb)], axis=0)
    mix_ref[...] = (jnp.dot(m5, wo_ref[0:e, :], preferred_element_type=F32)
                    + jnp.dot(mb_ref[...], wo_ref[e:2 * e, :], preferred_element_type=F32))
    fg = fg_ref[...]
    for rr in range(nrb):
        for b in range(nb):
            for r0 in range(0, w, LRU_RC):
                m0 = rr * rows + b * w + r0
                res = x_ref[b, rr, r0:r0 + LRU_RC, :] + gt_ref[b:b + 1] * mix_ref[m0:m0 + LRU_RC, :]
                ms = jnp.mean(res * res, axis=-1, keepdims=True)
                o_ref[b, rr, r0:r0 + LRU_RC, :] = res * lax.rsqrt(ms + EPS) * fg


def _lru_rev_out(lar, br, hloc, pcum, x4, m5, gs8, sh8, gt8, hs_f, hs_r, w_all, wo, fg):
    nb, nrow, w, d = x4.shape
    rows, e = hs_f.shape
    const = lambda a: pl.BlockSpec(a.shape, lambda i, nd=a.ndim: (0,) * nd)
    rb = LRU_RB
    nstep = nrow // rb
    rev3 = pl.BlockSpec((rb, rows, e), lambda i: (nstep - 1 - i, 0, 0))
    rev4 = pl.BlockSpec((nb, rb, w, d), lambda i: (0, nstep - 1 - i, 0, 0))
    rev4e = pl.BlockSpec((nb, rb, w, e), lambda i: (0, nstep - 1 - i, 0, 0))
    return pl.pallas_call(
        _lru_rev_out_kernel,
        grid=(nstep,),
        in_specs=[rev3] * 4 + [rev4, rev4e, const(gs8), const(sh8), const(gt8), const(hs_f), const(hs_r),
                               pl.BlockSpec((d, e), lambda i: (0, W_COL_GLRU)), const(wo), const(fg)],
        out_specs=rev4,
        out_shape=jax.ShapeDtypeStruct((nb, nrow, w, d), F32),
        scratch_shapes=[pltpu.VMEM((rows, e), F32), pltpu.VMEM((rb * rows, d), BF16),
                        pltpu.VMEM((rb * rows, e), F32), pltpu.VMEM((rb * rows, e), BF16),
                        pltpu.VMEM((rb * rows, d), F32)],
        compiler_params=_cparams(("arbitrary",)),
        name="lru_rev_out",
    )(lar, br, hloc, pcum, x4, m5, gs8, sh8, gt8, hs_f, hs_r, w_all, wo, fg)


def _gate_tiles(w_a, w_x):
    ndir, heads, hd, _ = w_a.shape
    per = GATE_TILE // hd
    nq = heads // per

    def tiles(w):
        wq = w.reshape(nq, per, hd, hd)
        eye = jnp.eye(per, dtype=w.dtype)
        blk = wq[:, :, :, None, :] * eye[None, :, None, :, None]
        return blk.reshape(nq, GATE_TILE, GATE_TILE)

    return jnp.concatenate([tiles(w_a[0]), tiles(w_x[0]), tiles(w_a[1]), tiles(w_x[1])], axis=2).astype(BF16)


def kernel(x, c, ctx, c_ctx, w_mod, b_mod, norm_g, w_in, s5_a_re, s5_a_im, s5_log_step, s5_b_re, s5_b_im,
           s5_c_re, s5_c_im, s5_d, s5_w_glu, s5_b_glu, lru_conv_w, lru_conv_b, lru_w_a, lru_b_a, lru_w_x,
           lru_b_x, lru_lam, w_out, final_g):
    bsz, seq, d = x.shape
    nctx_tok = ctx.shape[1]
    e = s5_w_glu.shape[-1]
    assert w_mod.shape[0] == 1, "single-layer block"
    assert bsz <= 8 and seq % S5_SEG == 0 and seq % GRID_W == 0 and nctx_tok % S5_T == 0
    nseg = seq // S5_SEG
    ns = bsz * nseg
    nctx = nctx_tok // S5_T
    assert nctx <= nseg and ns % S5_PB == 0
    nrow = seq // GRID_W
    assert LRU_RB == 2 and nrow % LRU_RB == 0 and nrow >= 4

    c8 = jnp.zeros((8, d), F32).at[:bsz].set(c).at[bsz].set(c_ctx)
    mod = _modulation(c8, w_mod[0], b_mod[0])
    sh, sc, gt = mod[:, :d], mod[:, d:2 * d], mod[:, 2 * d:]
    gs = norm_g[0][None, :] * (1.0 + sc)

    w_in0 = w_in[0]
    assert w_in0.shape == (d, 4 * e)
    wt_us5 = w_in0[:, W_COL_US5 * e:(W_COL_US5 + 1) * e].T.astype(BF16)
    gate_cols = (jnp.arange(4 * e) // e == W_COL_GS5) | (jnp.arange(4 * e) // e == W_COL_GLRU)
    w_all = (w_in0 * jnp.where(gate_cols, 0.5, 1.0)[None, :]).astype(BF16)
    w_glu = (0.5 * s5_w_glu[0]).astype(BF16)
    wo = w_out[0].astype(BF16)

    s5_params = _s5_param_layout(s5_a_re[0], s5_a_im[0], s5_log_step[0], s5_b_re[0], s5_b_im[0],
                                 s5_c_re[0], s5_c_im[0], s5_d[0])
    x3 = x.reshape(ns, S5_SEG, d)
    xt, hg5 = _s5_inproj(x3, gs, sh, wt_us5, w_all, bsz)
    ctx_pad = jnp.zeros((bsz, nseg, S5_T, d), F32).at[:, :nctx].set(ctx.reshape(bsz, nctx, S5_T, d))
    gs_c = jnp.broadcast_to(gs[bsz:bsz + 1], (8, d))
    sh_c = jnp.broadcast_to(sh[bsz:bsz + 1], (8, d))
    xct, _ = _s5_inproj(ctx_pad.reshape(ns, S5_T, d), gs_c, sh_c, wt_us5, w_all, bsz)
    y5 = _s5_core(xt, xct, s5_params, nseg, nctx)
    nj = S5_SEG // S5_T
    m5 = _s5_post(y5.reshape(nj, S5_T, e, ns), hg5, w_glu, 0.5 * s5_b_glu[0].reshape(1, e))

    wg_all = _gate_tiles(0.5 * lru_w_a[0], 0.5 * lru_w_x[0])
    gb_all = 0.5 * jnp.stack([lru_b_a[0, 0], lru_b_x[0, 0], lru_b_a[0, 1], lru_b_x[0, 1]])
    cw = lru_conv_w[0]
    cb = lru_conv_b[0].reshape(1, e)
    lam = lru_lam[0]
    ctx_p = jnp.zeros((8, nctx_tok, d), F32).at[:bsz].set(ctx).transpose(1, 0, 2).reshape(nctx_tok * 8, d)
    ff, fr = _lru_ctx(ctx_p, gs[bsz:bsz + 1], sh[bsz:bsz + 1], w_all, cw, cb, wg_all, gb_all, lam)
    x4 = x.reshape(bsz, nrow, GRID_W, d)
    lar, br, hloc, pcum, hf, pf, pr, hr = _lru_pass_a(x4, gs, sh, w_all, cw, cb, wg_all, gb_all, lam)
    hs_f, hs_r = _lru_stitch(pf, hf, pr, hr, jnp.repeat(ff[:bsz], GRID_W, axis=0),
                             jnp.repeat(fr[:bsz], GRID_W, axis=0), GRID_W)
    out4 = _lru_rev_out(lar, br, hloc, pcum, x4, m5.reshape(bsz, nrow, GRID_W, e), gs, sh, gt, hs_f, hs_r,
                        w_all, wo, final_g.reshape(1, d))
    return out4.reshape(bsz, seq, d)
```
